```python
import math
import jax, jax.numpy as jnp
from jax import lax
import numpy as np

D_MODEL = 4096
BATCH = 2
SEQ = 4096
DEPTH = 1
DEC_BATCH = 32
DEC_SEQ = 1
PAST_LEN = 8192
PAGE_SIZE = 128

MIX_WIDTH = D_MODEL
HEAD_DIM = 128
ATT_WIDTH = MIX_WIDTH // 2
N_HEADS = ATT_WIDTH // HEAD_DIM
N_KV_HEADS = 4
KV_WIDTH = N_KV_HEADS * HEAD_DIM
IDX_HEADS = 16
IDX_DIM = 64
TOPK_MAX = 256
QBLK = 128
REL_BUCKETS = 32
REL_MAX_DIST = 128
CONV_CH = MIX_WIDTH - ATT_WIDTH
CONV_GROUPS = 16
CONV_W = 31
PEER_HEADS = 8
PEER_NKEYS = 128
PEER_EXPERTS = PEER_NKEYS * PEER_NKEYS
PEER_DKEY = 256
PEER_TOPK = 16
PEER_CHUNK = 128
EPS = 1e-6

PROJ_SIZES = (ATT_WIDTH, KV_WIDTH, KV_WIDTH, IDX_HEADS * IDX_DIM, IDX_DIM, IDX_HEADS, CONV_CH, CONV_CH)
PROJ_WIDTH = ATT_WIDTH + 2 * KV_WIDTH + IDX_HEADS * IDX_DIM + IDX_DIM + IDX_HEADS + 2 * CONV_CH

kernel_name = 'hymba_dsa_conformer_peer_step'


def rmsnorm(x, g):
    x32 = x.astype(jnp.float32)
    y = x32 * lax.rsqrt(jnp.mean(x32 * x32, axis=-1, keepdims=True) + EPS)
    return (y * g.astype(jnp.float32)).astype(x.dtype)


def adaln(c, w_ada, b_ada):
    m = jax.nn.silu(c) @ w_ada + b_ada
    return jnp.split(m[:, None, :], 6, axis=-1)


def modulate(x, g, shift, scale):
    return rmsnorm(x, g) * (1 + scale) + shift


def split_proj(h, w_in):
    z = h @ w_in
    out = []
    o = 0
    for s in PROJ_SIZES:
        out.append(z[..., o:o + s])
        o += s
    return out


def rel_bucket(dist):
    n = jnp.maximum(dist, 0)
    max_exact = REL_BUCKETS // 2
    nf = jnp.maximum(n, 1).astype(jnp.float32)
    large = max_exact + (jnp.log(nf / max_exact) / math.log(REL_MAX_DIST / max_exact)
                         * (REL_BUCKETS - max_exact)).astype(jnp.int32)
    large = jnp.minimum(large, REL_BUCKETS - 1)
    return jnp.where(n < max_exact, n, large)


def indexer_select(qi, wi, ki, q_pos, n_sel):
    B, T = qi.shape[:2]
    qi = qi.reshape(B, T, IDX_HEADS, IDX_DIM)
    s = jnp.einsum('bthd,bsd->btsh', qi, ki)
    score = jnp.einsum('btsh,bth->bts', jax.nn.relu(s), wi).astype(jnp.float32)
    score = score * (IDX_DIM ** -0.5 * IDX_HEADS ** -0.5)
    key_pos = jnp.arange(ki.shape[1], dtype=jnp.int32)
    score = jnp.where(key_pos[None, None, :] <= q_pos[None, :, None], score, -jnp.inf)
    _, idx = lax.top_k(score, n_sel)
    return idx.astype(jnp.int32)


def sparse_attend(q, k_sel, v_sel, idx, q_pos, rel_bias):
    B, T = q.shape[:2]
    J = idx.shape[-1]
    R = N_HEADS // N_KV_HEADS
    qg = q.reshape(B, T, N_KV_HEADS, R, HEAD_DIM)
    logits = jnp.einsum('btgrd,btjgd->btgrj', qg, k_sel).astype(jnp.float32) * (HEAD_DIM ** -0.5)
    dist = q_pos[None, :, None] - idx
    bias = rel_bias[rel_bucket(dist)].astype(jnp.float32)
    bias = bias.reshape(B, T, J, N_KV_HEADS, R).transpose(0, 1, 3, 4, 2)
    valid = (dist >= 0)[:, :, None, None, :]
    logits = jnp.where(valid, logits + bias, -jnp.inf)
    p = jax.nn.softmax(logits, axis=-1).astype(v_sel.dtype)
    o = jnp.einsum('btgrj,btjgd->btgrd', p, v_sel)
    return o.reshape(B, T, ATT_WIDTH)


def gather_rows(rows, ix):
    return jax.vmap(lambda r, i: r[i])(rows, ix)


def prompt_attention(q, k4, v4, qi, wi, ki, rel_bias):
    B, S = q.shape[:2]
    n_sel = min(TOPK_MAX, S // 4)
    nb = S // QBLK

    def blocks(a):
        return jnp.moveaxis(a.reshape(B, nb, QBLK, *a.shape[2:]), 1, 0)

    pos = jnp.arange(S, dtype=jnp.int32).reshape(nb, QBLK)

    def one_block(args):
        qb, qib, wib, pb = args
        idx = indexer_select(qib, wib, ki, pb, n_sel)
        return sparse_attend(qb, gather_rows(k4, idx), gather_rows(v4, idx), idx, pb, rel_bias)

    out = lax.map(one_block, (blocks(q), blocks(qi), blocks(wi), pos))
    return jnp.moveaxis(out, 0, 1).reshape(B, S, ATT_WIDTH)


def sample_attention(q, k4, v4, qi, wi, ki, ck, cv, cki, page_table, rel_bias):
    DB, T = q.shape[:2]
    n_pages = page_table.shape[1]
    past = n_pages * PAGE_SIZE
    n_sel = min(TOPK_MAX, (past + T) // 4)
    ki_past = cki[page_table].reshape(DB, past, IDX_DIM)
    ki_all = jnp.concatenate([ki_past, ki.astype(ki_past.dtype)], axis=1)
    q_pos = past + jnp.arange(T, dtype=jnp.int32)
    idx = indexer_select(qi, wi, ki_all, q_pos, n_sel)
    in_past = (idx < past)[..., None, None]
    lp = jnp.minimum(idx // PAGE_SIZE, n_pages - 1)
    phys = jnp.take_along_axis(page_table, lp.reshape(DB, -1), axis=1).reshape(idx.shape)
    off = idx % PAGE_SIZE
    new_i = jnp.clip(idx - past, 0, T - 1)
    k_sel = jnp.where(in_past, ck[phys, off], gather_rows(k4, new_i).astype(ck.dtype))
    v_sel = jnp.where(in_past, cv[phys, off], gather_rows(v4, new_i).astype(cv.dtype))
    return sparse_attend(q, k_sel.astype(q.dtype), v_sel.astype(q.dtype), idx, q_pos, rel_bias)


def conformer_conv(u_pad, conv_w, conv_b, cn_g, cn_b):
    y = lax.conv_general_dilated(u_pad, conv_w.astype(u_pad.dtype)[:, None, :], (1,), 'VALID',
                                 dimension_numbers=('NWC', 'WIO', 'NWC'),
                                 feature_group_count=CONV_CH) + conv_b
    B, T = y.shape[:2]
    yg = y.astype(jnp.float32).reshape(B, T, CONV_GROUPS, CONV_CH // CONV_GROUPS)
    mu = jnp.mean(yg, axis=-1, keepdims=True)
    var = jnp.mean(jnp.square(yg - mu), axis=-1, keepdims=True)
    yn = ((yg - mu) * lax.rsqrt(var + EPS)).reshape(B, T, CONV_CH)
    yn = yn * cn_g.astype(jnp.float32) + cn_b.astype(jnp.float32)
    return jax.nn.silu(yn).astype(u_pad.dtype)


def mixer_prompt(h, w_in, conv_w, conv_b, cn_g, cn_b, w_o, rel_bias):
    B, S = h.shape[:2]
    q, k, v, qi, ki, wi, ua, ub = split_proj(h, w_in)
    k4 = k.reshape(B, S, N_KV_HEADS, HEAD_DIM)
    v4 = v.reshape(B, S, N_KV_HEADS, HEAD_DIM)
    att = prompt_attention(q, k4, v4, qi, wi, ki, rel_bias)
    u = ua * jax.nn.sigmoid(ub)
    u_pad = jnp.pad(u, ((0, 0), (CONV_W - 1, 0), (0, 0)))
    conv = conformer_conv(u_pad, conv_w, conv_b, cn_g, cn_b)
    out = jnp.concatenate([att, conv], axis=-1) @ w_o
    return out, k4, v4, ki, u_pad[:, -(CONV_W - 1):]


def mixer_sample(h, ck, cv, cki, conv_state, page_table, w_in, conv_w, conv_b, cn_g, cn_b, w_o, rel_bias):
    DB, T = h.shape[:2]
    q, k, v, qi, ki, wi, ua, ub = split_proj(h, w_in)
    k4 = k.reshape(DB, T, N_KV_HEADS, HEAD_DIM)
    v4 = v.reshape(DB, T, N_KV_HEADS, HEAD_DIM)
    att = sample_attention(q, k4, v4, qi, wi, ki, ck, cv, cki, page_table, rel_bias)
    u = ua * jax.nn.sigmoid(ub)
    u_pad = jnp.concatenate([conv_state.astype(u.dtype), u], axis=1)
    conv = conformer_conv(u_pad, conv_w, conv_b, cn_g, cn_b)
    out = jnp.concatenate([att, conv], axis=-1) @ w_o
    return out, k4, v4, ki, u_pad[:, -(CONV_W - 1):]


def peer_route(h, wq, subkeys):
    N = h.shape[0]
    half = PEER_DKEY // 2
    q = (h @ wq).reshape(N, PEER_HEADS, PEER_DKEY)
    s1 = jnp.einsum('nhd,hkd->nhk', q[..., :half], subkeys[:, 0]).astype(jnp.float32)
    s2 = jnp.einsum('nhd,hkd->nhk', q[..., half:], subkeys[:, 1]).astype(jnp.float32)
    t1, i1 = lax.top_k(s1, PEER_TOPK)
    t2, i2 = lax.top_k(s2, PEER_TOPK)
    cand = (t1[..., :, None] + t2[..., None, :]).reshape(N, PEER_HEADS, PEER_TOPK * PEER_TOPK)
    cidx = (i1[..., :, None] * PEER_NKEYS + i2[..., None, :]).reshape(N, PEER_HEADS, PEER_TOPK * PEER_TOPK)
    top, pos = lax.top_k(cand, PEER_TOPK)
    eidx = jnp.take_along_axis(cidx, pos, axis=-1)
    g = jax.nn.softmax(top, axis=-1)
    return eidx.reshape(N, -1), g.reshape(N, -1)


def peer_experts(h, eidx, g, u, v):
    a = jax.nn.gelu(jnp.einsum('nd,nkd->nk', h, u[eidx]).astype(jnp.float32))
    return jnp.einsum('nk,nkd->nd', (g * a).astype(h.dtype), v[eidx])


def peer_ffn(h, wq, subkeys, u, v):
    shp = h.shape
    hf = h.reshape(-1, D_MODEL)
    N = hf.shape[0]
    eidx, g = peer_route(hf, wq, subkeys)
    if N > PEER_CHUNK and N % PEER_CHUNK == 0:
        nc = N // PEER_CHUNK
        out = lax.map(lambda a: peer_experts(a[0], a[1], a[2], u, v),
                      (hf.reshape(nc, PEER_CHUNK, D_MODEL), eidx.reshape(nc, PEER_CHUNK, -1),
                       g.reshape(nc, PEER_CHUNK, -1)))
        out = out.reshape(N, D_MODEL)
    else:
        out = peer_experts(hf, eidx, g, u, v)
    return out.reshape(shp)


def setup_inputs(seed: int = 0) -> dict:
    key = jax.random.key(seed)
    ks = jax.random.split(key, 32)
    f32 = jnp.float32
    n_pages = PAST_LEN // PAGE_SIZE
    n_used = DEC_BATCH * n_pages
    n_pool = n_used + n_used // 4

    def nrm(k, shape, s):
        return jax.random.normal(k, shape, f32) * s

    page_table = jax.random.permutation(ks[6], n_pool)[:n_used].reshape(DEC_BATCH, n_pages).astype(jnp.int32)
    return {
        'x_prompt': nrm(ks[0], (BATCH, SEQ, D_MODEL), 1.0),
        'x_sample': nrm(ks[1], (DEC_BATCH, DEC_SEQ, D_MODEL), 1.0),
        'cache_k': nrm(ks[2], (DEPTH, n_pool, PAGE_SIZE, N_KV_HEADS, HEAD_DIM), 1.0),
        'cache_v': nrm(ks[3], (DEPTH, n_pool, PAGE_SIZE, N_KV_HEADS, HEAD_DIM), 1.0),
        'cache_kidx': nrm(ks[4], (DEPTH, n_pool, PAGE_SIZE, IDX_DIM), 1.0),
        'state_conv': nrm(ks[5], (DEPTH, DEC_BATCH, CONV_W - 1, CONV_CH), 0.5),
        'page_table': page_table,
        'c_prompt': nrm(ks[7], (BATCH, D_MODEL), 1.0),
        'c_sample': nrm(ks[8], (DEC_BATCH, D_MODEL), 1.0),
        'rel_bias': nrm(ks[9], (REL_BUCKETS, N_HEADS), 0.5),
        'w_ada': nrm(ks[10], (DEPTH, D_MODEL, 6 * D_MODEL), 0.5 * D_MODEL ** -0.5),
        'b_ada': nrm(ks[11], (DEPTH, 6 * D_MODEL), 0.02),
        'g_mix': 1.0 + nrm(ks[12], (DEPTH, D_MODEL), 0.02),
        'w_in': nrm(ks[13], (DEPTH, D_MODEL, PROJ_WIDTH), D_MODEL ** -0.5),
        'conv_w': nrm(ks[14], (DEPTH, CONV_W, CONV_CH), CONV_W ** -0.5),
        'conv_b': nrm(ks[15], (DEPTH, CONV_CH), 0.02),
        'cn_g': 1.0 + nrm(ks[16], (DEPTH, CONV_CH), 0.02),
        'cn_b': nrm(ks[17], (DEPTH, CONV_CH), 0.02),
        'w_o': nrm(ks[18], (DEPTH, MIX_WIDTH, D_MODEL), MIX_WIDTH ** -0.5),
        'g_ch': 1.0 + nrm(ks[19], (DEPTH, D_MODEL), 0.02),
        'peer_wq': nrm(ks[20], (DEPTH, D_MODEL, PEER_HEADS * PEER_DKEY), D_MODEL ** -0.5),
        'peer_subkeys': nrm(ks[21], (DEPTH, PEER_HEADS, 2, PEER_NKEYS, PEER_DKEY // 2), (PEER_DKEY // 2) ** -0.5),
        'peer_u': nrm(ks[22], (DEPTH, PEER_EXPERTS, D_MODEL), D_MODEL ** -0.5),
        'peer_v': nrm(ks[23], (DEPTH, PEER_EXPERTS, D_MODEL), 0.5),
        'g_final': 1.0 + nrm(ks[24], (D_MODEL,), 0.02),
    }


def reference(x_prompt, x_sample, cache_k, cache_v, cache_kidx, state_conv, page_table,
              c_prompt, c_sample, rel_bias, w_ada, b_ada, g_mix, w_in, conv_w, conv_b,
              cn_g, cn_b, w_o, g_ch, peer_wq, peer_subkeys, peer_u, peer_v, g_final):
    xp = x_prompt
    xs = x_sample
    kp_l, vp_l, kip_l, cp_l = [], [], [], []
    ks_l, vs_l, kis_l, cs_l = [], [], [], []
    for l in range(DEPTH):
        p_sh1, p_sc1, p_gt1, p_sh2, p_sc2, p_gt2 = adaln(c_prompt, w_ada[l], b_ada[l])
        s_sh1, s_sc1, s_gt1, s_sh2, s_sc2, s_gt2 = adaln(c_sample, w_ada[l], b_ada[l])
        hp = modulate(xp, g_mix[l], p_sh1, p_sc1)
        hs = modulate(xs, g_mix[l], s_sh1, s_sc1)
        mp, kp, vp, kip, cp = mixer_prompt(hp, w_in[l], conv_w[l], conv_b[l], cn_g[l], cn_b[l],
                                           w_o[l], rel_bias)
        ms, ks, vs, kis, cs = mixer_sample(hs, cache_k[l], cache_v[l], cache_kidx[l], state_conv[l],
                                           page_table, w_in[l], conv_w[l], conv_b[l], cn_g[l],
                                           cn_b[l], w_o[l], rel_bias)
        xp = xp + p_gt1 * mp
        xs = xs + s_gt1 * ms
        hp = modulate(xp, g_ch[l], p_sh2, p_sc2)
        hs = modulate(xs, g_ch[l], s_sh2, s_sc2)
        xp = xp + p_gt2 * peer_ffn(hp, peer_wq[l], peer_subkeys[l], peer_u[l], peer_v[l])
        xs = xs + s_gt2 * peer_ffn(hs, peer_wq[l], peer_subkeys[l], peer_u[l], peer_v[l])
        kp_l.append(kp); vp_l.append(vp); kip_l.append(kip); cp_l.append(cp)
        ks_l.append(ks); vs_l.append(vs); kis_l.append(kis); cs_l.append(cs)
    y_prompt = rmsnorm(xp, g_final)
    y_sample = rmsnorm(xs, g_final)
    k_prompt = jnp.stack(kp_l)
    v_prompt = jnp.stack(vp_l)
    kidx_prompt = jnp.stack(kip_l)
    conv_prompt = jnp.stack(cp_l)
    k_sample = jnp.stack(ks_l)
    v_sample = jnp.stack(vs_l)
    kidx_sample = jnp.stack(kis_l)
    conv_sample = jnp.stack(cs_l)
    return (y_prompt, y_sample, k_prompt, v_prompt, kidx_prompt, conv_prompt,
            k_sample, v_sample, kidx_sample, conv_sample)
```

```python
import functools
import math

import jax
import jax.numpy as jnp
from jax import lax
from jax.experimental import pallas as pl
from jax.experimental.pallas import tpu as pltpu

F32 = jnp.float32
BF16 = jnp.bfloat16
I32 = jnp.int32

HEAD_DIM = 128
N_HEADS = 16
N_KV_HEADS = 4
KV_GROUP = N_HEADS // N_KV_HEADS
ATT_WIDTH = N_HEADS * HEAD_DIM
KV_WIDTH = N_KV_HEADS * HEAD_DIM
IDX_HEADS = 16
IDX_DIM = 64
TOPK_MAX = 256
REL_BUCKETS = 32
REL_MAX_DIST = 128
CONV_W = 31
PEER_HEADS = 8
PEER_NKEYS = 128
PEER_TOPK = 16
EPS = 1e-6
PAGE_SIZE = 128

LANES = 128
SUBLANES = 8
VMEM_LIMIT = 56 * 1024 * 1024

NEG = -1e30
INT_MIN = -(2 ** 31)
CONV_HALO = 32
SAMPLE_ROWS = 128
PAGES_PER_STEP = 8


def _cparams(sem):
    return pltpu.CompilerParams(dimension_semantics=sem, vmem_limit_bytes=VMEM_LIMIT)


def _dot_nt(a, b):
    return lax.dot_general(a, b, (((1,), (1,)), ((), ())), preferred_element_type=F32)


def _sortable(x):
    bits = pltpu.bitcast(x, I32)
    return bits ^ ((bits >> 31) & jnp.int32(0x7FFFFFFF))


def _adaln_kernel(c_ref, w_ref, b_ref, o_ref):
    c = c_ref[...]
    a = (c * jax.nn.sigmoid(c)).astype(BF16)
    o_ref[...] = jnp.dot(a, w_ref[...].astype(BF16), preferred_element_type=F32) + b_ref[...]


def _adaln(c, w_ada, b_ada, tn=512):
    r, d = c.shape
    n = w_ada.shape[1]
    return pl.pallas_call(
        _adaln_kernel,
        grid=(n // tn,),
        in_specs=[pl.BlockSpec((r, d), lambda j: (0, 0)),
                  pl.BlockSpec((d, tn), lambda j: (0, j)),
                  pl.BlockSpec((1, tn), lambda j: (0, j))],
        out_specs=pl.BlockSpec((r, tn), lambda j: (0, j)),
        out_shape=jax.ShapeDtypeStruct((r, n), F32),
        compiler_params=_cparams(("arbitrary",)),
        name="adaln",
    )(c, w_ada, b_ada.reshape(1, n))


def _modulate_kernel(x_ref, g_ref, sc_ref, sh_ref, o_ref):
    x = x_ref[0]
    ms = jnp.mean(x * x, axis=-1, keepdims=True)
    y = x * lax.rsqrt(ms + EPS) * g_ref[...]
    o_ref[0] = (y * (1.0 + sc_ref[0]) + sh_ref[0]).astype(o_ref.dtype)


def _row_mod_spec(mod, tr):
    d = mod.shape[-1]
    if mod.shape[1] == 1:
        return pl.BlockSpec((1, 1, d), lambda g, r, *_: (g, 0, 0))
    return pl.BlockSpec((1, tr, d), lambda g, r, *_: (g, r, 0))


def _modulate(x3, gain, sc3, sh3, tr):
    g, r, d = x3.shape
    return pl.pallas_call(
        _modulate_kernel,
        grid=(g, r // tr),
        in_specs=[pl.BlockSpec((1, tr, d), lambda a, b: (a, b, 0)),
                  pl.BlockSpec((1, d), lambda a, b: (0, 0)),
                  _row_mod_spec(sc3, tr),
                  _row_mod_spec(sh3, tr)],
        out_specs=pl.BlockSpec((1, tr, d), lambda a, b: (a, b, 0)),
        out_shape=jax.ShapeDtypeStruct((g, r, d), BF16),
        compiler_params=_cparams(("arbitrary", "arbitrary")),
        name="modulate",
    )(x3, gain.reshape(1, d), sc3, sh3)


def _mm_kernel(h_ref, w_ref, o_ref, *, glu, head_major):
    acc = jnp.dot(h_ref[...], w_ref[...], preferred_element_type=F32)
    if glu:
        half = acc.shape[1] // 2
        acc = acc[:, :half] * jax.nn.sigmoid(acc[:, half:])
    if head_major:
        for c in range(acc.shape[1] // LANES):
            o_ref[c] = acc[:, c * LANES:(c + 1) * LANES].astype(o_ref.dtype)
    else:
        o_ref[...] = acc.astype(o_ref.dtype)


def _matmul(h, w, out_dtype, tm, tn, glu=False, head_major=False):
    m, k = h.shape
    n = w.shape[1]
    n_out = n // 2 if glu else n
    tn_out = tn // 2 if glu else tn
    if head_major:
        out_shape = jax.ShapeDtypeStruct((n_out // LANES, m, LANES), out_dtype)
        out_spec = pl.BlockSpec((tn_out // LANES, tm, LANES), lambda i, j: (j, i, 0))
    else:
        out_shape = jax.ShapeDtypeStruct((m, n_out), out_dtype)
        out_spec = pl.BlockSpec((tm, tn_out), lambda i, j: (i, j))
    return pl.pallas_call(
        functools.partial(_mm_kernel, glu=glu, head_major=head_major),
        grid=(m // tm, n // tn),
        in_specs=[pl.BlockSpec((tm, k), lambda i, j: (i, 0)),
                  pl.BlockSpec((k, tn), lambda i, j: (0, j))],
        out_specs=out_spec,
        out_shape=out_shape,
        compiler_params=_cparams(("arbitrary", "arbitrary")),
        name="proj_matmul",
    )(h, w)


def _indexer_kernel(qi_ref, ka_ref, kb_ref, wi_ref, o_ref, keys_ref, cst_ref, *, tq, tk, nk, topk):
    i = pl.program_id(1)
    q0 = i * tq
    nvis = (q0 + tq + tk - 1) // tk
    qpos = q0 + lax.broadcasted_iota(I32, (1, tq), 1)
    w = wi_ref[...]
    score_scale = IDX_DIM ** -0.5 * IDX_HEADS ** -0.5

    def kpos_of(j):
        return j * tk + lax.broadcasted_iota(I32, (tk, 1), 0)

    def score_body(j, carry):
        k0 = pl.multiple_of(j * tk, tk)
        ka = ka_ref[pl.ds(k0, tk), :]
        kb = kb_ref[pl.ds(k0, tk), :]
        acc = jnp.zeros((tk, tq), F32)
        for p in range(IDX_HEADS // 2):
            qp = qi_ref[p]
            sa = _dot_nt(ka, qp)
            sb = _dot_nt(kb, qp)
            acc = acc + jnp.maximum(sa, 0.0) * w[2 * p:2 * p + 1]
            acc = acc + jnp.maximum(sb, 0.0) * w[2 * p + 1:2 * p + 2]
        acc = acc * score_scale
        acc = jnp.where(kpos_of(j) <= qpos, acc, -jnp.inf)
        keys_ref[j] = _sortable(acc)
        return carry

    lax.fori_loop(0, nvis, score_body, 0)

    def count(pred):
        def body(j, c):
            return c + jnp.sum(pred(keys_ref[j], j), axis=0, keepdims=True)
        return lax.fori_loop(0, nvis, body, jnp.zeros((1, tq), I32))

    def bit_body(t, thr):
        cand = thr ^ lax.shift_left(jnp.int32(1), 31 - t)
        cnt = count(lambda k, j: jnp.where(k >= cand, 1, 0))
        return jnp.where(cnt >= topk, cand, thr)

    thr = lax.fori_loop(0, 32, bit_body, jnp.full((1, tq), INT_MIN, I32))

    need = topk - count(lambda k, j: jnp.where(k > thr, 1, 0))
    n_eq = count(lambda k, j: jnp.where(k == thr, 1, 0))
    cst_ref[...] = jnp.full((1, tq), nk * tk, I32)

    @pl.when(jnp.max(jnp.where(n_eq > need, 1, 0)) > 0)
    def _():
        def idx_body(t, c):
            cand = c | lax.shift_left(jnp.int32(1), 15 - t)
            f = count(lambda k, j: jnp.where(k == thr, jnp.where(kpos_of(j) < cand, 1, 0), 0))
            return jnp.where(f < need, cand, c)
        cst_ref[...] = lax.fori_loop(0, 16, idx_body, jnp.zeros((1, tq), I32))

    cstar = cst_ref[...]

    def write_body(j, carry):
        k = keys_ref[j]
        kpos = kpos_of(j)
        sel = jnp.where(k > thr, 0.0, jnp.where(k == thr, jnp.where(kpos <= cstar, 0.0, NEG), NEG))
        o_ref[0, j] = jnp.where(kpos <= qpos, sel, NEG).astype(o_ref.dtype)
        return carry

    lax.fori_loop(0, nvis, write_body, 0)

    def fill_body(j, carry):
        o_ref[0, j] = jnp.full((tk, tq), NEG, o_ref.dtype)
        return carry

    lax.fori_loop(nvis, nk, fill_body, 0)


def _indexer(qh, kvb, wi_t, batch, seq, tq, tk, qi_blk, ka_blk):
    nq, nk = seq // tq, seq // tk
    topk = min(TOPK_MAX, seq // 4)
    return pl.pallas_call(
        functools.partial(_indexer_kernel, tq=tq, tk=tk, nk=nk, topk=topk),
        grid=(batch, nq),
        in_specs=[pl.BlockSpec((IDX_HEADS // 2, tq, LANES), lambda b, i: (qi_blk, b * nq + i, 0)),
                  pl.BlockSpec((seq, LANES), lambda b, i: (b, ka_blk)),
                  pl.BlockSpec((seq, LANES), lambda b, i: (b, ka_blk + 1)),
                  pl.BlockSpec((IDX_HEADS, tq), lambda b, i: (0, b * nq + i))],
        out_specs=pl.BlockSpec((1, nk, tk, tq), lambda b, i: (b * nq + i, 0, 0, 0)),
        out_shape=jax.ShapeDtypeStruct((batch * nq, nk, tk, tq), BF16),
        scratch_shapes=[pltpu.VMEM((nk, tk, tq), I32), pltpu.VMEM((1, tq), I32)],
        compiler_params=_cparams(("arbitrary", "arbitrary")),
        name="indexer_topk_mask",
    )(qh, kvb, kvb, wi_t)


def _attn_kernel(q_ref, k_ref, vt_ref, m_ref, bt_ref, o_ref, acc_ref, mx_ref, l_ref, *, tq, tk):
    i = pl.program_id(1)
    sm_scale = HEAD_DIM ** -0.5
    mx_ref[...] = jnp.full(mx_ref.shape, NEG, F32)
    l_ref[...] = jnp.zeros(l_ref.shape, F32)
    acc_ref[...] = jnp.zeros(acc_ref.shape, F32)
    zero_blk = jnp.zeros((LANES, LANES), F32)

    def bias_tile(g, near):
        cols = []
        for r in range(KV_GROUP):
            h = g * KV_GROUP + r
            b0 = bt_ref[0, h]
            b1 = bt_ref[1, h]
            if near == 0:
                top = jnp.concatenate([b0, b1], axis=1)
                bot = jnp.concatenate([zero_blk, b0], axis=1)
            else:
                top = jnp.concatenate([zero_blk, zero_blk], axis=1)
                bot = jnp.concatenate([b1, zero_blk], axis=1)
            cols.append(jnp.concatenate([top, bot], axis=0))
        return jnp.concatenate(cols, axis=1)

    def update(j, near):
        k0 = pl.multiple_of(j * tk, tk)
        mt = m_ref[0, j].astype(F32)
        mt4 = jnp.concatenate([mt] * KV_GROUP, axis=1)
        for g in range(N_KV_HEADS):
            kt = k_ref[pl.ds(k0, tk), g * HEAD_DIM:(g + 1) * HEAD_DIM]
            qs = q_ref[g * KV_GROUP:(g + 1) * KV_GROUP].reshape(KV_GROUP * tq, HEAD_DIM)
            s = _dot_nt(kt, qs) * sm_scale + mt4
            if near is not None:
                s = s + bias_tile(g, near)
            m_old = mx_ref[g]
            m_new = jnp.maximum(m_old, jnp.max(s, axis=0, keepdims=True))
            alpha = jnp.exp(m_old - m_new)
            p = jnp.exp(s - m_new)
            l_ref[g] = alpha * l_ref[g] + jnp.sum(p, axis=0, keepdims=True)
            vt = vt_ref[0, j, g * HEAD_DIM:(g + 1) * HEAD_DIM, :]
            pv = jnp.dot(vt, p.astype(BF16), preferred_element_type=F32)
            acc_ref[g] = alpha * acc_ref[g] + pv
            mx_ref[g] = m_new

    def far_body(j, carry):
        update(j, None)
        return carry

    lax.fori_loop(0, jnp.maximum(i - 1, 0), far_body, 0)

    @pl.when(i >= 1)
    def _():
        update(i - 1, 1)

    update(i, 0)

    for g in range(N_KV_HEADS):
        o = acc_ref[g] / l_ref[g]
        for r in range(KV_GROUP):
            h = g * KV_GROUP + r
            o_ref[:, h * HEAD_DIM:(h + 1) * HEAD_DIM] = o[:, r * tq:(r + 1) * tq].T.astype(o_ref.dtype)


def _attention(qh, kvb, vt_tiles, mask, bias_t, batch, seq, tq, tk):
    nq, nk = seq // tq, seq // tk
    return pl.pallas_call(
        functools.partial(_attn_kernel, tq=tq, tk=tk),
        grid=(batch, nq),
        in_specs=[pl.BlockSpec((N_HEADS, tq, HEAD_DIM), lambda b, i: (0, b * nq + i, 0)),
                  pl.BlockSpec((seq, KV_WIDTH), lambda b, i: (b, 0)),
                  pl.BlockSpec((1, nk, KV_WIDTH, tk), lambda b, i: (b, 0, 0, 0)),
                  pl.BlockSpec((1, nk, tk, tq), lambda b, i: (b * nq + i, 0, 0, 0)),
                  pl.BlockSpec((2, N_HEADS, LANES, LANES), lambda b, i: (0, 0, 0, 0))],
        out_specs=pl.BlockSpec((tq, ATT_WIDTH), lambda b, i: (b * nq + i, 0)),
        out_shape=jax.ShapeDtypeStruct((batch * seq, ATT_WIDTH), BF16),
        scratch_shapes=[pltpu.VMEM((N_KV_HEADS, HEAD_DIM, KV_GROUP * tq), F32),
                        pltpu.VMEM((N_KV_HEADS, 1, KV_GROUP * tq), F32),
                        pltpu.VMEM((N_KV_HEADS, 1, KV_GROUP * tq), F32)],
        compiler_params=_cparams(("arbitrary", "arbitrary")),
        name="masked_attention",
    )(qh, kvb, vt_tiles, mask, bias_t)


def _conv_kernel(halo_ref, cur_ref, w_ref, b_ref, g_ref, bb_ref, o_ref, ext_ref, *, tt, tc, rc, zero_first):
    t = pl.program_id(1)
    halo = halo_ref[0]
    if zero_first:
        halo = jnp.where(t == 0, 0.0, halo)
    ext_ref[0:CONV_HALO, :] = halo
    ext_ref[CONV_HALO:CONV_HALO + tt, :] = cur_ref[0]
    first = CONV_HALO - (CONV_W - 1)
    for c in range(tc // LANES):
        cs = slice(c * LANES, (c + 1) * LANES)
        for r in range(tt // rc):
            acc = jnp.zeros((rc, LANES), F32) + b_ref[:, cs]
            for j in range(CONV_W):
                acc = acc + w_ref[j:j + 1, cs] * ext_ref[r * rc + first + j:r * rc + first + j + rc, cs]
            mu = jnp.mean(acc, axis=-1, keepdims=True)
            dv = acc - mu
            var = jnp.mean(dv * dv, axis=-1, keepdims=True)
            yn = dv * lax.rsqrt(var + EPS) * g_ref[:, cs] + bb_ref[:, cs]
            o_ref[0, r * rc:(r + 1) * rc, cs] = (yn * jax.nn.sigmoid(yn)).astype(o_ref.dtype)


def _conv(halo_src, cur, conv_w, conv_b, cn_g, cn_b, tt, tc, zero_first):
    b, t, c = cur.shape
    hb = tt // CONV_HALO
    if zero_first:
        halo_spec = pl.BlockSpec((1, CONV_HALO, tc), lambda a, i, j: (a, jnp.maximum(i * hb - 1, 0), j))
    else:
        halo_spec = pl.BlockSpec((1, CONV_HALO, tc), lambda a, i, j: (a, 0, j))
    vec = lambda: pl.BlockSpec((1, tc), lambda a, i, j: (0, j))
    return pl.pallas_call(
        functools.partial(_conv_kernel, tt=tt, tc=tc, rc=min(tt, 64), zero_first=zero_first),
        grid=(b, t // tt, c // tc),
        in_specs=[halo_spec,
                  pl.BlockSpec((1, tt, tc), lambda a, i, j: (a, i, j)),
                  pl.BlockSpec((CONV_W, tc), lambda a, i, j: (0, j)),
                  vec(), vec(), vec()],
        out_specs=pl.BlockSpec((1, tt, tc), lambda a, i, j: (a, i, j)),
        out_shape=jax.ShapeDtypeStruct((b, t, c), BF16),
        scratch_shapes=[pltpu.VMEM((CONV_HALO + tt, tc), F32)],
        compiler_params=_cparams(("arbitrary", "arbitrary", "arbitrary")),
        name="conformer_conv",
    )(halo_src, cur, conv_w, conv_b.reshape(1, c), cn_g.reshape(1, c), cn_b.reshape(1, c))


def _outproj_kernel(a_ref, c_ref, wa_ref, wc_ref, x_ref, gt_ref, o_ref):
    acc = jnp.dot(a_ref[0], wa_ref[...], preferred_element_type=F32)
    acc = acc + jnp.dot(c_ref[0], wc_ref[...], preferred_element_type=F32)
    o_ref[0] = x_ref[0] + gt_ref[0] * acc


def _outproj(att3, conv3, wo_a, wo_c, x3, gt3, tm, tn):
    g, r, d = x3.shape
    ka, kc = att3.shape[-1], conv3.shape[-1]
    gt_spec = (pl.BlockSpec((1, 1, tn), lambda a, i, j: (a, 0, j)) if gt3.shape[1] == 1
               else pl.BlockSpec((1, tm, tn), lambda a, i, j: (a, i, j)))
    return pl.pallas_call(
        _outproj_kernel,
        grid=(g, r // tm, d // tn),
        in_specs=[pl.BlockSpec((1, tm, ka), lambda a, i, j: (a, i, 0)),
                  pl.BlockSpec((1, tm, kc), lambda a, i, j: (a, i, 0)),
                  pl.BlockSpec((ka, tn), lambda a, i, j: (0, j)),
                  pl.BlockSpec((kc, tn), lambda a, i, j: (0, j)),
                  pl.BlockSpec((1, tm, tn), lambda a, i, j: (a, i, j)),
                  gt_spec],
        out_specs=pl.BlockSpec((1, tm, tn), lambda a, i, j: (a, i, j)),
        out_shape=jax.ShapeDtypeStruct((g, r, d), F32),
        compiler_params=_cparams(("arbitrary", "arbitrary", "arbitrary")),
        name="outproj_residual",
    )(att3, conv3, wo_a, wo_c, x3, gt3)


def _peer_scores_kernel(h_ref, wq_ref, sk_ref, o_ref):
    q = jnp.dot(h_ref[...], wq_ref[...], preferred_element_type=F32).astype(BF16)
    half = q.shape[1] // 2
    o_ref[0:PEER_NKEYS, :] = _dot_nt(sk_ref[0, 0], q[:, :half])
    o_ref[PEER_NKEYS:2 * PEER_NKEYS, :] = _dot_nt(sk_ref[0, 1], q[:, half:])


def _peer_scores(h2, wq, sk, tm):
    n, d = h2.shape
    dk = wq.shape[1] // PEER_HEADS
    return pl.pallas_call(
        _peer_scores_kernel,
        grid=(n // tm, PEER_HEADS),
        in_specs=[pl.BlockSpec((tm, d), lambda i, h: (i, 0)),
                  pl.BlockSpec((d, dk), lambda i, h: (0, h)),
                  pl.BlockSpec((1, 2, PEER_NKEYS, dk // 2), lambda i, h: (h, 0, 0, 0))],
        out_specs=pl.BlockSpec((2 * PEER_NKEYS, tm), lambda i, h: (h, i)),
        out_shape=jax.ShapeDtypeStruct((PEER_HEADS * 2 * PEER_NKEYS, n), F32),
        compiler_params=_cparams(("arbitrary", "arbitrary")),
        name="peer_subkey_scores",
    )(h2, wq, sk)


def _top16(x):
    rows = x.shape[0]
    rid = lax.broadcasted_iota(I32, x.shape, 0)
    vals = []
    for k in range(PEER_TOPK):
        m = jnp.max(x, axis=0, keepdims=True)
        vals.append(m)
        if k + 1 < PEER_TOPK:
            first = jnp.min(jnp.where(x == m, rid, rows), axis=0, keepdims=True)
            x = jnp.where(rid == first, -jnp.inf, x)
    return vals


def _stack_rows(rows):
    shape = (len(rows), rows[0].shape[1])
    rid = lax.broadcasted_iota(I32, shape, 0)
    out = jnp.zeros(shape, rows[0].dtype)
    for k, row in enumerate(rows):
        out = jnp.where(rid == k, row, out)
    return out


def _peer_route_kernel(s_ref, s1m_ref, s2m_ref, c1_ref, e2_ref, thr_ref):
    s1 = s_ref[0:PEER_NKEYS, :]
    s2 = s_ref[PEER_NKEYS:2 * PEER_NKEYS, :]
    t1 = _top16(s1)
    t2 = _top16(s2)
    t2_all = _stack_rows(t2)
    blocks = [t1[0] + t2_all]
    for a in range(1, 8):
        blocks.append(t1[a] + t2_all[0:8])
    blocks.append(_stack_rows(t1[8:16]) + t2[0])
    top = _top16(jnp.concatenate(blocks, axis=0))
    z = jnp.ones_like(top[0])
    for k in range(1, PEER_TOPK):
        z = z + jnp.exp(top[k] - top[0])
    s1m_ref[...] = jnp.where(s1 >= t1[PEER_TOPK - 1], s1, -jnp.inf)
    s2m_ref[...] = jnp.where(s2 >= t2[PEER_TOPK - 1], s2, -jnp.inf)
    c1_ref[...] = jnp.exp(s1 - t1[0]) / z
    e2_ref[...] = jnp.exp(s2 - t2[0])
    thr_ref[0] = top[PEER_TOPK - 1]


def _peer_route(s_t, tl):
    rows, n = s_t.shape
    big = lambda: pl.BlockSpec((PEER_NKEYS, tl), lambda h, t: (h, t))
    big_shape = jax.ShapeDtypeStruct((PEER_HEADS * PEER_NKEYS, n), F32)
    return pl.pallas_call(
        _peer_route_kernel,
        grid=(PEER_HEADS, n // tl),
        in_specs=[pl.BlockSpec((2 * PEER_NKEYS, tl), lambda h, t: (h, t))],
        out_specs=[big(), big(), big(), big(), pl.BlockSpec((1, 1, tl), lambda h, t: (h, 0, t))],
        out_shape=[big_shape, big_shape, big_shape, big_shape,
                   jax.ShapeDtypeStruct((PEER_HEADS, 1, n), F32)],
        compiler_params=_cparams(("arbitrary", "arbitrary")),
        name="peer_route",
    )(s_t)


def _peer_dense_kernel(h_ref, u_ref, vt_ref, s1_ref, c1_ref, s2_ref, e2_ref, thr_ref, o_ref, w_ref, *, te):
    e = pl.program_id(1)

    @pl.when(e == 0)
    def _():
        o_ref[...] = jnp.zeros(o_ref.shape, F32)

    a_t = _dot_nt(u_ref[...], h_ref[...])
    rows_per_tile = te // PEER_NKEYS
    for r in range(rows_per_tile):
        i1 = e * rows_per_tile + r
        gate = jnp.zeros((PEER_NKEYS, a_t.shape[1]), F32)
        for h in range(PEER_HEADS):
            hs = slice(h * PEER_NKEYS, (h + 1) * PEER_NKEYS)
            s1row = s1_ref[pl.ds(h * PEER_NKEYS + i1, 1), :]
            c1row = c1_ref[pl.ds(h * PEER_NKEYS + i1, 1), :]
            cand = s2_ref[hs, :] + s1row
            gate = gate + jnp.where(cand >= thr_ref[h], e2_ref[hs, :], 0.0) * c1row
        rs = slice(r * PEER_NKEYS, (r + 1) * PEER_NKEYS)
        w_ref[rs, :] = (gate * jax.nn.gelu(a_t[rs, :])).astype(BF16)
    o_ref[...] += jnp.dot(vt_ref[...], w_ref[...], preferred_element_type=F32)


def _peer_dense(h2, u_bf, vt_bf, s1m, c1, s2m, e2, thr, tm, te):
    n, d = h2.shape
    ne = u_bf.shape[0]
    rows = PEER_HEADS * PEER_NKEYS
    once = pl.Buffered(1)
    tok = lambda: pl.BlockSpec((rows, tm), lambda i, e: (0, i), pipeline_mode=once)
    return pl.pallas_call(
        functools.partial(_peer_dense_kernel, te=te),
        grid=(n // tm, ne // te),
        in_specs=[pl.BlockSpec((tm, d), lambda i, e: (i, 0), pipeline_mode=once),
                  pl.BlockSpec((te, d), lambda i, e: (e, 0)),
                  pl.BlockSpec((d, te), lambda i, e: (0, e)),
                  tok(), tok(), tok(), tok(),
                  pl.BlockSpec((PEER_HEADS, 1, tm), lambda i, e: (0, 0, i), pipeline_mode=once)],
        out_specs=pl.BlockSpec((d, tm), lambda i, e: (0, i)),
        out_shape=jax.ShapeDtypeStruct((d, n), F32),
        scratch_shapes=[pltpu.VMEM((te, tm), BF16)],
        compiler_params=_cparams(("arbitrary", "arbitrary")),
        name="peer_dense_experts",
    )(h2, u_bf, vt_bf, s1m, c1, s2m, e2, thr)


def _final_kernel(x_ref, gt_ref, p_ref, g_ref, o_ref, *, normalize):
    xx = x_ref[0] + gt_ref[0] * p_ref[...].T
    if normalize:
        ms = jnp.mean(xx * xx, axis=-1, keepdims=True)
        xx = xx * lax.rsqrt(ms + EPS) * g_ref[...]
    o_ref[0] = xx


def _final(x3, gt3, peer_t, g_final, tm, normalize):
    g, r, d = x3.shape
    nb = r // tm
    return pl.pallas_call(
        functools.partial(_final_kernel, normalize=normalize),
        grid=(g, nb),
        in_specs=[pl.BlockSpec((1, tm, d), lambda a, i: (a, i, 0)),
                  _row_mod_spec(gt3, tm),
                  pl.BlockSpec((d, tm), lambda a, i: (0, a * nb + i)),
                  pl.BlockSpec((1, d), lambda a, i: (0, 0))],
        out_specs=pl.BlockSpec((1, tm, d), lambda a, i: (a, i, 0)),
        out_shape=jax.ShapeDtypeStruct((g, r, d), F32),
        compiler_params=_cparams(("arbitrary", "arbitrary")),
        name="final_norm",
    )(x3, gt3, peer_t, g_final.reshape(1, d))


def _sample_index_kernel(pt_ref, qi_ref, wi_ref, kself_ref, *rest, n_pages, topk):
    pages = rest[:PAGES_PER_STEP]
    mask_ref, self_ref, sc_ref = rest[PAGES_PER_STEP:]
    step = pl.program_id(1)
    n_steps = n_pages // PAGES_PER_STEP
    score_scale = IDX_DIM ** -0.5 * IDX_HEADS ** -0.5
    qi = qi_ref[0]
    wi = wi_ref[0]
    for k in range(PAGES_PER_STEP):
        s = _dot_nt(qi, pages[k][0].astype(BF16))
        row = jnp.sum(jnp.maximum(s, 0.0) * wi, axis=0, keepdims=True) * score_scale
        sc_ref[pl.ds(step * PAGES_PER_STEP + k, 1), :] = row

    @pl.when(step == n_steps - 1)
    def _():
        ks = kself_ref[0].astype(BF16).astype(F32)
        s_self = jnp.sum(qi.astype(F32) * ks, axis=1, keepdims=True)
        self_score = jnp.sum(jnp.maximum(s_self, 0.0) * wi, axis=0, keepdims=True) * score_scale
        keys = _sortable(sc_ref[...])
        kself = _sortable(self_score)
        pos = (lax.broadcasted_iota(I32, keys.shape, 0) * PAGE_SIZE
               + lax.broadcasted_iota(I32, keys.shape, 1))
        self_pos = n_pages * PAGE_SIZE

        def total(x, xs):
            return jnp.sum(jnp.sum(x, axis=1, keepdims=True), axis=0, keepdims=True) + xs

        def bit_body(t, thr):
            cand = thr ^ lax.shift_left(jnp.int32(1), 31 - t)
            cnt = total(jnp.where(keys >= cand, 1, 0), jnp.where(kself >= cand, 1, 0))
            return jnp.where(cnt >= topk, cand, thr)

        thr = lax.fori_loop(0, 32, bit_body, jnp.full((1, 1), INT_MIN, I32))
        need = topk - total(jnp.where(keys > thr, 1, 0), jnp.where(kself > thr, 1, 0))

        def idx_body(t, c):
            cand = c | lax.shift_left(jnp.int32(1), 15 - t)
            f = total(jnp.where(keys == thr, jnp.where(pos < cand, 1, 0), 0),
                      jnp.where(kself == thr, jnp.where(self_pos < cand, 1, 0), 0))
            return jnp.where(f < need, cand, c)

        cstar = lax.fori_loop(0, 16, idx_body, jnp.zeros((1, 1), I32))
        sel = jnp.where(keys > thr, 0.0, jnp.where(keys == thr, jnp.where(pos <= cstar, 0.0, NEG), NEG))
        mask_ref[0] = sel
        ssel = jnp.where(kself > thr, 0.0, jnp.where(kself == thr, jnp.where(self_pos <= cstar, 0.0, NEG), NEG))
        self_ref[0] = jnp.zeros((SUBLANES, LANES), F32) + ssel


def _sample_index(page_table, qi3, wi3, kself3, cki):
    db, n_pages = page_table.shape
    topk = min(TOPK_MAX, (n_pages * PAGE_SIZE + 1) // 4)
    n_steps = n_pages // PAGES_PER_STEP

    def page_spec(k):
        return pl.BlockSpec((1, PAGE_SIZE, IDX_DIM),
                            lambda b, s, pt: (pt[b, s * PAGES_PER_STEP + k], 0, 0))

    grid_spec = pltpu.PrefetchScalarGridSpec(
        num_scalar_prefetch=1,
        grid=(db, n_steps),
        in_specs=[pl.BlockSpec((1, IDX_HEADS, IDX_DIM), lambda b, s, pt: (b, 0, 0)),
                  pl.BlockSpec((1, IDX_HEADS, 1), lambda b, s, pt: (b, 0, 0)),
                  pl.BlockSpec((1, 1, IDX_DIM), lambda b, s, pt: (b, 0, 0))]
                 + [page_spec(k) for k in range(PAGES_PER_STEP)],
        out_specs=[pl.BlockSpec((1, n_pages, PAGE_SIZE), lambda b, s, pt: (b, 0, 0)),
                   pl.BlockSpec((1, SUBLANES, LANES), lambda b, s, pt: (b, 0, 0))],
        scratch_shapes=[pltpu.VMEM((n_pages, PAGE_SIZE), F32)],
    )
    return pl.pallas_call(
        functools.partial(_sample_index_kernel, n_pages=n_pages, topk=topk),
        grid_spec=grid_spec,
        out_shape=[jax.ShapeDtypeStruct((db, n_pages, PAGE_SIZE), F32),
                   jax.ShapeDtypeStruct((db, SUBLANES, LANES), F32)],
        compiler_params=_cparams(("arbitrary", "arbitrary")),
        name="sample_indexer",
    )(page_table, qi3, wi3, kself3, *([cki] * PAGES_PER_STEP))


def _sample_attn_kernel(pt_ref, q_ref, kself_ref, vself_ref, mask_ref, self_ref, bias_ref, bself_ref,
                        *rest, n_pages):
    kpages = rest[:PAGES_PER_STEP]
    vpages = rest[PAGES_PER_STEP:2 * PAGES_PER_STEP]
    o_ref, acc_ref, mx_ref, l_ref = rest[2 * PAGES_PER_STEP:]
    step = pl.program_id(1)
    n_steps = n_pages // PAGES_PER_STEP
    sm_scale = HEAD_DIM ** -0.5
    q = q_ref[0]
    row_group = lax.broadcasted_iota(I32, (N_HEADS, KV_WIDTH), 0) // KV_GROUP
    lane_group = lax.broadcasted_iota(I32, (N_HEADS, KV_WIDTH), 1) // HEAD_DIM
    own = row_group == lane_group
    q_bd = jnp.where(own, jnp.concatenate([q] * N_KV_HEADS, axis=1), jnp.zeros((), BF16))

    def pick_group(full):
        kept = jnp.where(own, full, 0.0)
        out = kept[:, 0:HEAD_DIM]
        for g in range(1, N_KV_HEADS):
            out = out + kept[:, g * HEAD_DIM:(g + 1) * HEAD_DIM]
        return out

    @pl.when(step == 0)
    def _():
        ks = kself_ref[0].astype(BF16).astype(F32)
        logit = jnp.sum(q_bd.astype(F32) * ks, axis=1, keepdims=True)
        logit = logit * sm_scale + bself_ref[...] + self_ref[0, 0:1, 0:1]
        mx_ref[...] = logit
        l_ref[...] = jnp.ones(l_ref.shape, F32)
        vs = vself_ref[0].astype(BF16).astype(F32)
        acc_ref[...] = pick_group(jnp.zeros((N_HEADS, KV_WIDTH), F32) + vs)

    for k in range(PAGES_PER_STEP):
        page = step * PAGES_PER_STEP + k
        kp = kpages[k][0].astype(BF16)
        s = _dot_nt(q_bd, kp) * sm_scale
        s = s + bias_ref[page] + mask_ref[0, pl.ds(page, 1), :]
        m_old = mx_ref[...]
        m_new = jnp.maximum(m_old, jnp.max(s, axis=1, keepdims=True))
        alpha = jnp.exp(m_old - m_new)
        p = jnp.exp(s - m_new)
        l_ref[...] = alpha * l_ref[...] + jnp.sum(p, axis=1, keepdims=True)
        pv = jnp.dot(p.astype(BF16), vpages[k][0].astype(BF16), preferred_element_type=F32)
        acc_ref[...] = alpha * acc_ref[...] + pick_group(pv)
        mx_ref[...] = m_new

    @pl.when(step == n_steps - 1)
    def _():
        o_ref[0] = (acc_ref[...] / l_ref[...]).astype(o_ref.dtype)


def _sample_attn(page_table, q3, kself3, vself3, mask, selfsel, bias_pages, bias_self, ck, cv):
    db, n_pages = page_table.shape
    n_steps = n_pages // PAGES_PER_STEP

    def page_spec(k):
        return pl.BlockSpec((1, PAGE_SIZE, KV_WIDTH),
                            lambda b, s, pt: (pt[b, s * PAGES_PER_STEP + k], 0, 0))

    per_b = lambda shape: pl.BlockSpec((1,) + shape, lambda b, s, pt: (b, 0, 0))
    grid_spec = pltpu.PrefetchScalarGridSpec(
        num_scalar_prefetch=1,
        grid=(db, n_steps),
        in_specs=[per_b((N_HEADS, HEAD_DIM)), per_b((1, KV_WIDTH)), per_b((1, KV_WIDTH)),
                  per_b((n_pages, PAGE_SIZE)), per_b((SUBLANES, LANES)),
                  pl.BlockSpec((n_pages, N_HEADS, PAGE_SIZE), lambda b, s, pt: (0, 0, 0)),
                  pl.BlockSpec((N_HEADS, 1), lambda b, s, pt: (0, 0))]
                 + [page_spec(k) for k in range(PAGES_PER_STEP)] * 2,
        out_specs=per_b((N_HEADS, HEAD_DIM)),
        scratch_shapes=[pltpu.VMEM((N_HEADS, HEAD_DIM), F32),
                        pltpu.VMEM((N_HEADS, 1), F32),
                        pltpu.VMEM((N_HEADS, 1), F32)],
    )
    return pl.pallas_call(
        functools.partial(_sample_attn_kernel, n_pages=n_pages),
        grid_spec=grid_spec,
        out_shape=jax.ShapeDtypeStruct((db, N_HEADS, HEAD_DIM), BF16),
        compiler_params=_cparams(("arbitrary", "arbitrary")),
        name="sample_attention",
    )(page_table, q3, kself3, vself3, mask, selfsel, bias_pages, bias_self,
      *([ck] * PAGES_PER_STEP), *([cv] * PAGES_PER_STEP))


def _rel_bucket(dist):
    n = jnp.maximum(dist, 0)
    max_exact = REL_BUCKETS // 2
    nf = jnp.maximum(n, 1).astype(F32)
    large = max_exact + (jnp.log(nf / max_exact) / math.log(REL_MAX_DIST / max_exact)
                         * (REL_BUCKETS - max_exact)).astype(I32)
    large = jnp.minimum(large, REL_BUCKETS - 1)
    return jnp.where(n < max_exact, n, large)


def _prompt_bias_tiles(rel_bias):
    kc = jnp.arange(LANES, dtype=I32)[:, None]
    qr = jnp.arange(LANES, dtype=I32)[None, :]
    far = rel_bias[REL_BUCKETS - 1]
    tiles = []
    for off in (0, LANES):
        dist = off + qr - kc
        t = rel_bias[_rel_bucket(dist)] - far
        tiles.append(jnp.where((dist >= 0)[..., None], t, 0.0).transpose(2, 0, 1))
    return jnp.stack(tiles).astype(F32)


def _split_in_proj(w_in_l):
    sizes = (ATT_WIDTH, KV_WIDTH, KV_WIDTH, IDX_HEADS * IDX_DIM, IDX_DIM, IDX_HEADS)
    offs = [0]
    for s in sizes:
        offs.append(offs[-1] + s)
    conv_ch = (w_in_l.shape[1] - offs[-1]) // 2
    d = w_in_l.shape[0]
    wq, wk, wv, wqi, wki, wwi = (w_in_l[:, offs[i]:offs[i + 1]] for i in range(6))
    wua = w_in_l[:, offs[-1]:offs[-1] + conv_ch]
    wub = w_in_l[:, offs[-1] + conv_ch:]
    z = lambda n: jnp.zeros((d, n), w_in_l.dtype)
    w_a = jnp.concatenate([wk, wv, wki, z(LANES - IDX_DIM), z(LANES - IDX_DIM), wki,
                           wwi, z(LANES - IDX_HEADS), z(LANES)], axis=1)
    w_b = jnp.concatenate([wq, wqi], axis=1)
    chunk = 256
    parts = []
    for c in range(conv_ch // chunk):
        parts += [wua[:, c * chunk:(c + 1) * chunk], wub[:, c * chunk:(c + 1) * chunk]]
    w_c = jnp.concatenate(parts, axis=1)
    return w_a.astype(BF16), w_b.astype(BF16), w_c.astype(BF16)


COL_K, COL_V, COL_KA, COL_WI = 0, KV_WIDTH, 2 * KV_WIDTH, 2 * KV_WIDTH + 2 * LANES


def _mixer_projections(h, w_a, w_b, w_c, tm):
    z_a = _matmul(h, w_a, F32, tm, 512)
    qh = _matmul(h, w_b, BF16, tm, 512, head_major=True)
    u = _matmul(h, w_c, F32, tm, 512, glu=True)
    return z_a, qh, u


def _peer_block(h2, wq_bf, sk_bf, u_bf, vt_bf, tm, te):
    s_t = _peer_scores(h2, wq_bf, sk_bf, tm)
    s1m, s2m, c1, e2, thr = _peer_route(s_t, min(tm, 256))
    return _peer_dense(h2, u_bf, vt_bf, s1m, c1, s2m, e2, thr, tm, te)


def kernel(x_prompt, x_sample, cache_k, cache_v, cache_kidx, state_conv, page_table, c_prompt, c_sample,
           rel_bias, w_ada, b_ada, g_mix, w_in, conv_w, conv_b, cn_g, cn_b, w_o, g_ch, peer_wq,
           peer_subkeys, peer_u, peer_v, g_final):
    batch, seq, d = x_prompt.shape
    db = x_sample.shape[0]
    depth = w_ada.shape[0]
    n_pages = page_table.shape[1]
    past = n_pages * PAGE_SIZE
    conv_ch = conv_w.shape[-1]
    n_prompt = batch * seq
    tq = tk = 256

    xp = x_prompt
    xs = jnp.pad(x_sample.reshape(1, db, d), ((0, 0), (0, SAMPLE_ROWS - db), (0, 0)))
    c_rows = batch + db
    c_pad = (-c_rows) % 16
    c_all = jnp.pad(jnp.concatenate([c_prompt, c_sample], axis=0), ((0, c_pad), (0, 0)))
    bias_t = _prompt_bias_tiles(rel_bias)
    key_pos = jnp.arange(past, dtype=I32)
    bias_pages = rel_bias[_rel_bucket(past - key_pos)].reshape(n_pages, PAGE_SIZE, N_HEADS).transpose(0, 2, 1)
    bias_self = rel_bias[_rel_bucket(jnp.zeros((), I32))].reshape(N_HEADS, 1)

    outs = {k: [] for k in ("kp", "vp", "kip", "cp", "ks", "vs", "kis", "cs")}
    for l in range(depth):
        mods = _adaln(c_all, w_ada[l], b_ada[l])
        p_mod = [m.reshape(batch, 1, d) for m in jnp.split(mods[:batch], 6, axis=-1)]
        s_mod = [jnp.pad(m.reshape(1, db, d), ((0, 0), (0, SAMPLE_ROWS - db), (0, 0)))
                 for m in jnp.split(mods[batch:c_rows], 6, axis=-1)]
        w_a, w_b, w_c = _split_in_proj(w_in[l])
        wo_a = w_o[l][:ATT_WIDTH].astype(BF16)
        wo_c = w_o[l][ATT_WIDTH:].astype(BF16)

        hp = _modulate(xp, g_mix[l], p_mod[1], p_mod[0], 512).reshape(n_prompt, d)
        z_a, qh, u = _mixer_projections(hp, w_a, w_b, w_c, 1024)
        kvb = z_a[:, :COL_WI].astype(BF16)
        wi_t = z_a[:, COL_WI:COL_WI + IDX_HEADS].T
        vt_tiles = (kvb[:, COL_V:COL_V + KV_WIDTH].reshape(batch, seq // tk, tk, KV_WIDTH)
                    .transpose(0, 1, 3, 2))
        mask = _indexer(qh, kvb, wi_t, batch, seq, tq, tk, qi_blk=N_HEADS // (IDX_HEADS // 2),
                        ka_blk=COL_KA // LANES)
        att = _attention(qh, kvb, vt_tiles, mask, bias_t, batch, seq, tq, tk)
        u3 = u.reshape(batch, seq, conv_ch)
        conv = _conv(u3, u3, conv_w[l], conv_b[l], cn_g[l], cn_b[l], 256, 512, zero_first=True)
        xp = _outproj(att.reshape(batch, seq, ATT_WIDTH), conv, wo_a, wo_c, xp, p_mod[2], 1024, 512)
        outs["kp"].append(z_a[:, COL_K:COL_K + KV_WIDTH].reshape(batch, seq, N_KV_HEADS, HEAD_DIM))
        outs["vp"].append(z_a[:, COL_V:COL_V + KV_WIDTH].reshape(batch, seq, N_KV_HEADS, HEAD_DIM))
        outs["kip"].append(z_a[:, COL_KA:COL_KA + IDX_DIM].reshape(batch, seq, IDX_DIM))
        outs["cp"].append(u3[:, seq - (CONV_W - 1):])

        hs = _modulate(xs, g_mix[l], s_mod[1], s_mod[0], SAMPLE_ROWS).reshape(SAMPLE_ROWS, d)
        zs_a, qhs, us = _mixer_projections(hs, w_a, w_b, w_c, SAMPLE_ROWS)
        k_new = zs_a[:db, COL_K:COL_K + KV_WIDTH]
        v_new = zs_a[:db, COL_V:COL_V + KV_WIDTH]
        ki_new = zs_a[:db, COL_KA:COL_KA + IDX_DIM]
        wi_new = zs_a[:db, COL_WI:COL_WI + IDX_HEADS]
        q_s = qhs[:N_HEADS, :db].transpose(1, 0, 2)
        qi_s = (qhs[N_HEADS:, :db].transpose(1, 0, 2)
                .reshape(db, IDX_HEADS // 2, 2, IDX_DIM).reshape(db, IDX_HEADS, IDX_DIM))
        smask, sself = _sample_index(page_table, qi_s, wi_new.reshape(db, IDX_HEADS, 1),
                                     ki_new.reshape(db, 1, IDX_DIM), cache_kidx[l])
        n_pool = cache_k.shape[1]
        att_s = _sample_attn(page_table, q_s, k_new.reshape(db, 1, KV_WIDTH), v_new.reshape(db, 1, KV_WIDTH),
                             smask, sself, bias_pages, bias_self,
                             cache_k[l].reshape(n_pool, PAGE_SIZE, KV_WIDTH),
                             cache_v[l].reshape(n_pool, PAGE_SIZE, KV_WIDTH))
        att_s = jnp.pad(att_s.reshape(1, db, ATT_WIDTH), ((0, 0), (0, SAMPLE_ROWS - db), (0, 0)))
        u_new = us[:db]
        state = state_conv[l].astype(F32)
        halo = jnp.pad(state, ((0, 0), (CONV_HALO - (CONV_W - 1), 0), (0, 0)))
        cur = jnp.pad(u_new.reshape(db, 1, conv_ch), ((0, 0), (0, SUBLANES - 1), (0, 0)))
        conv_s = _conv(halo, cur, conv_w[l], conv_b[l], cn_g[l], cn_b[l], SUBLANES, 512, zero_first=False)
        conv_s = jnp.pad(conv_s[:, 0].reshape(1, db, conv_ch), ((0, 0), (0, SAMPLE_ROWS - db), (0, 0)))
        xs = _outproj(att_s, conv_s, wo_a, wo_c, xs, s_mod[2], SAMPLE_ROWS, 512)
        outs["ks"].append(k_new.reshape(db, 1, N_KV_HEADS, HEAD_DIM))
        outs["vs"].append(v_new.reshape(db, 1, N_KV_HEADS, HEAD_DIM))
        outs["kis"].append(ki_new.reshape(db, 1, IDX_DIM))
        outs["cs"].append(jnp.concatenate([state[:, 1:], u_new.reshape(db, 1, conv_ch)], axis=1))

        wq_bf = peer_wq[l].astype(BF16)
        sk_bf = peer_subkeys[l].astype(BF16)
        u_bf = peer_u[l].astype(BF16)
        vt_bf = peer_v[l].astype(BF16).T
        hp2 = _modulate(xp, g_ch[l], p_mod[4], p_mod[3], 512).reshape(n_prompt, d)
        peer_p = _peer_block(hp2, wq_bf, sk_bf, u_bf, vt_bf, 512, 512)
        hs2 = _modulate(xs, g_ch[l], s_mod[4], s_mod[3], SAMPLE_ROWS).reshape(SAMPLE_ROWS, d)
        peer_s = _peer_block(hs2, wq_bf, sk_bf, u_bf, vt_bf, SAMPLE_ROWS, 512)
        last = l == depth - 1
        xp = _final(xp, p_mod[5], peer_p, g_final, 256, normalize=last)
        xs = _final(xs, s_mod[5], peer_s, g_final, SAMPLE_ROWS, normalize=last)

    st = lambda k: jnp.stack(outs[k])
    y_sample = xs[0, :db].reshape(db, 1, d)
    return (xp, y_sample, st("kp"), st("vp"), st("kip"), st("cp"),
            st("ks"), st("vs"), st("kis"), st("cs"))
```

```python
import functools
import math

import jax
import jax.numpy as jnp
from jax import lax
from jax.experimental import pallas as pl
from jax.experimental.pallas import tpu as pltpu

F32 = jnp.float32
BF16 = jnp.bfloat16
I32 = jnp.int32

HEAD_DIM = 128
N_HEADS = 16
N_KV_HEADS = 4
KV_GROUP = N_HEADS // N_KV_HEADS
ATT_WIDTH = N_HEADS * HEAD_DIM
KV_WIDTH = N_KV_HEADS * HEAD_DIM
IDX_HEADS = 16
IDX_DIM = 64
TOPK_MAX = 256
REL_BUCKETS = 32
REL_MAX_DIST = 128
CONV_W = 31
PEER_HEADS = 8
PEER_NKEYS = 128
PEER_TOPK = 16
EPS = 1e-6
PAGE_SIZE = 128

LANES = 128
SUBLANES = 8
VMEM_LIMIT = 56 * 1024 * 1024

NEG = -1e30
INT_MIN = -(2 ** 31)
CONV_HALO = 32
SAMPLE_ROWS = 128
PAGES_PER_STEP = 8


def _cparams(sem, flags=None):
    return pltpu.CompilerParams(dimension_semantics=sem, vmem_limit_bytes=VMEM_LIMIT, flags=flags)


def _dot_nt(a, b):
    return lax.dot_general(a, b, (((1,), (1,)), ((), ())), preferred_element_type=F32)


def _sortable(x):
    bits = pltpu.bitcast(x, I32)
    return bits ^ ((bits >> 31) & jnp.int32(0x7FFFFFFF))


def _adaln_kernel(c_ref, w_ref, b_ref, o_ref):
    c = c_ref[...]
    a = (c * jax.nn.sigmoid(c)).astype(BF16)
    o_ref[...] = jnp.dot(a, w_ref[...].astype(BF16), preferred_element_type=F32) + b_ref[...]


def _adaln(c, w_ada, b_ada, tn=512):
    r, d = c.shape
    n = w_ada.shape[1]
    return pl.pallas_call(
        _adaln_kernel,
        grid=(n // tn,),
        in_specs=[pl.BlockSpec((r, d), lambda j: (0, 0)),
                  pl.BlockSpec((d, tn), lambda j: (0, j)),
                  pl.BlockSpec((1, tn), lambda j: (0, j))],
        out_specs=pl.BlockSpec((r, tn), lambda j: (0, j)),
        out_shape=jax.ShapeDtypeStruct((r, n), F32),
        compiler_params=_cparams(("arbitrary",)),
        name="adaln",
    )(c, w_ada, b_ada.reshape(1, n))


def _modulate_kernel(x_ref, g_ref, sc_ref, sh_ref, o_ref, *, transposed):
    x = x_ref[0]
    ms = jnp.mean(x * x, axis=-1, keepdims=True)
    y = x * lax.rsqrt(ms + EPS) * g_ref[...]
    y = y * (1.0 + sc_ref[0]) + sh_ref[0]
    if transposed:
        o_ref[...] = y.T.astype(o_ref.dtype)
    else:
        o_ref[0] = y.astype(o_ref.dtype)


def _row_mod_spec(mod, tr):
    d = mod.shape[-1]
    if mod.shape[1] == 1:
        return pl.BlockSpec((1, 1, d), lambda g, r, *_: (g, 0, 0))
    return pl.BlockSpec((1, tr, d), lambda g, r, *_: (g, r, 0))


def _modulate(x3, gain, sc3, sh3, tr, transposed=False):
    g, r, d = x3.shape
    nb = r // tr
    if transposed:
        out_spec = pl.BlockSpec((d, tr), lambda a, b: (0, a * nb + b))
        out_shape = jax.ShapeDtypeStruct((d, g * r), BF16)
    else:
        out_spec = pl.BlockSpec((1, tr, d), lambda a, b: (a, b, 0))
        out_shape = jax.ShapeDtypeStruct((g, r, d), BF16)
    return pl.pallas_call(
        functools.partial(_modulate_kernel, transposed=transposed),
        grid=(g, nb),
        in_specs=[pl.BlockSpec((1, tr, d), lambda a, b: (a, b, 0)),
                  pl.BlockSpec((1, d), lambda a, b: (0, 0)),
                  _row_mod_spec(sc3, tr),
                  _row_mod_spec(sh3, tr)],
        out_specs=out_spec,
        out_shape=out_shape,
        compiler_params=_cparams(("arbitrary", "arbitrary")),
        name="modulate",
    )(x3, gain.reshape(1, d), sc3, sh3)


def _mm_kernel(h_ref, w_ref, o_ref, *, glu, head_major):
    acc = jnp.dot(h_ref[...], w_ref[...], preferred_element_type=F32)
    if glu:
        half = acc.shape[1] // 2
        acc = acc[:, :half] * jax.nn.sigmoid(acc[:, half:])
    if head_major:
        for c in range(acc.shape[1] // LANES):
            o_ref[c] = acc[:, c * LANES:(c + 1) * LANES].astype(o_ref.dtype)
    else:
        o_ref[...] = acc.astype(o_ref.dtype)


def _matmul(h, w, out_dtype, tm, tn, glu=False, head_major=False):
    m, k = h.shape
    n = w.shape[1]
    n_out = n // 2 if glu else n
    tn_out = tn // 2 if glu else tn
    if head_major:
        out_shape = jax.ShapeDtypeStruct((n_out // LANES, m, LANES), out_dtype)
        out_spec = pl.BlockSpec((tn_out // LANES, tm, LANES), lambda i, j: (j, i, 0))
    else:
        out_shape = jax.ShapeDtypeStruct((m, n_out), out_dtype)
        out_spec = pl.BlockSpec((tm, tn_out), lambda i, j: (i, j))
    return pl.pallas_call(
        functools.partial(_mm_kernel, glu=glu, head_major=head_major),
        grid=(m // tm, n // tn),
        in_specs=[pl.BlockSpec((tm, k), lambda i, j: (i, 0)),
                  pl.BlockSpec((k, tn), lambda i, j: (0, j))],
        out_specs=out_spec,
        out_shape=out_shape,
        compiler_params=_cparams(("arbitrary", "arbitrary")),
        name="proj_matmul",
    )(h, w)


def _indexer_kernel(qi_ref, ka_ref, kb_ref, wi_ref, o_ref, keys_ref, cst_ref, *, tq, tk, nk, topk):
    i = pl.program_id(1)
    q0 = i * tq
    nvis = (q0 + tq + tk - 1) // tk
    qpos = q0 + lax.broadcasted_iota(I32, (1, tq), 1)
    w = wi_ref[...]
    score_scale = IDX_DIM ** -0.5 * IDX_HEADS ** -0.5

    def kpos_of(j):
        return j * tk + lax.broadcasted_iota(I32, (tk, 1), 0)

    def score_body(j, carry):
        k0 = pl.multiple_of(j * tk, tk)
        ka = ka_ref[pl.ds(k0, tk), :]
        kb = kb_ref[pl.ds(k0, tk), :]
        acc = jnp.zeros((tk, tq), F32)
        for p in range(IDX_HEADS // 2):
            qp = qi_ref[p]
            sa = _dot_nt(ka, qp)
            sb = _dot_nt(kb, qp)
            acc = acc + jnp.maximum(sa, 0.0) * w[2 * p:2 * p + 1]
            acc = acc + jnp.maximum(sb, 0.0) * w[2 * p + 1:2 * p + 2]
        acc = acc * score_scale
        acc = jnp.where(kpos_of(j) <= qpos, acc, -jnp.inf)
        keys_ref[j] = _sortable(acc)
        return carry

    lax.fori_loop(0, nvis, score_body, 0)

    def count(pred):
        def body(j, c):
            return c + jnp.sum(pred(keys_ref[j], j), axis=0, keepdims=True)
        return lax.fori_loop(0, nvis, body, jnp.zeros((1, tq), I32))

    def bit_body(t, thr):
        cand = thr ^ lax.shift_left(jnp.int32(1), 31 - t)
        cnt = count(lambda k, j: jnp.where(k >= cand, 1, 0))
        return jnp.where(cnt >= topk, cand, thr)

    thr = lax.fori_loop(0, 32, bit_body, jnp.full((1, tq), INT_MIN, I32))

    need = topk - count(lambda k, j: jnp.where(k > thr, 1, 0))
    n_eq = count(lambda k, j: jnp.where(k == thr, 1, 0))
    cst_ref[...] = jnp.full((1, tq), nk * tk, I32)

    @pl.when(jnp.max(jnp.where(n_eq > need, 1, 0)) > 0)
    def _():
        def idx_body(t, c):
            cand = c | lax.shift_left(jnp.int32(1), 15 - t)
            f = count(lambda k, j: jnp.where(k == thr, jnp.where(kpos_of(j) < cand, 1, 0), 0))
            return jnp.where(f < need, cand, c)
        cst_ref[...] = lax.fori_loop(0, 16, idx_body, jnp.zeros((1, tq), I32))

    cstar = cst_ref[...]

    def write_body(j, carry):
        k = keys_ref[j]
        kpos = kpos_of(j)
        sel = jnp.where(k > thr, 0.0, jnp.where(k == thr, jnp.where(kpos <= cstar, 0.0, NEG), NEG))
        o_ref[0, j] = jnp.where(kpos <= qpos, sel, NEG).astype(o_ref.dtype)
        return carry

    lax.fori_loop(0, nvis, write_body, 0)

    def fill_body(j, carry):
        o_ref[0, j] = jnp.full((tk, tq), NEG, o_ref.dtype)
        return carry

    lax.fori_loop(nvis, nk, fill_body, 0)


def _indexer(qh, kvb, wi_t, batch, seq, tq, tk, qi_blk, ka_blk):
    nq, nk = seq // tq, seq // tk
    topk = min(TOPK_MAX, seq // 4)
    return pl.pallas_call(
        functools.partial(_indexer_kernel, tq=tq, tk=tk, nk=nk, topk=topk),
        grid=(batch, nq),
        in_specs=[pl.BlockSpec((IDX_HEADS // 2, tq, LANES), lambda b, i: (qi_blk, b * nq + i, 0)),
                  pl.BlockSpec((seq, LANES), lambda b, i: (b, ka_blk)),
                  pl.BlockSpec((seq, LANES), lambda b, i: (b, ka_blk + 1)),
                  pl.BlockSpec((IDX_HEADS, tq), lambda b, i: (0, b * nq + i))],
        out_specs=pl.BlockSpec((1, nk, tk, tq), lambda b, i: (b * nq + i, 0, 0, 0)),
        out_shape=jax.ShapeDtypeStruct((batch * nq, nk, tk, tq), BF16),
        scratch_shapes=[pltpu.VMEM((nk, tk, tq), I32), pltpu.VMEM((1, tq), I32)],
        compiler_params=_cparams(("arbitrary", "arbitrary")),
        name="indexer_topk_mask",
    )(qh, kvb, kvb, wi_t)


def _attn_kernel(q_ref, k_ref, vt_ref, m_ref, bt_ref, o_ref, acc_ref, mx_ref, l_ref, *, tq, tk):
    i = pl.program_id(1)
    sm_scale = HEAD_DIM ** -0.5
    mx_ref[...] = jnp.full(mx_ref.shape, NEG, F32)
    l_ref[...] = jnp.zeros(l_ref.shape, F32)
    acc_ref[...] = jnp.zeros(acc_ref.shape, F32)
    zero_blk = jnp.zeros((LANES, LANES), F32)

    def bias_tile(g, near):
        cols = []
        for r in range(KV_GROUP):
            h = g * KV_GROUP + r
            b0 = bt_ref[0, h]
            b1 = bt_ref[1, h]
            if near == 0:
                top = jnp.concatenate([b0, b1], axis=1)
                bot = jnp.concatenate([zero_blk, b0], axis=1)
            else:
                top = jnp.concatenate([zero_blk, zero_blk], axis=1)
                bot = jnp.concatenate([b1, zero_blk], axis=1)
            cols.append(jnp.concatenate([top, bot], axis=0))
        return jnp.concatenate(cols, axis=1)

    def update(j, near):
        k0 = pl.multiple_of(j * tk, tk)
        mt = m_ref[0, j].astype(F32)
        mt4 = jnp.concatenate([mt] * KV_GROUP, axis=1)
        for g in range(N_KV_HEADS):
            kt = k_ref[pl.ds(k0, tk), g * HEAD_DIM:(g + 1) * HEAD_DIM]
            qs = q_ref[g * KV_GROUP:(g + 1) * KV_GROUP].reshape(KV_GROUP * tq, HEAD_DIM)
            s = _dot_nt(kt, qs) * sm_scale + mt4
            if near is not None:
                s = s + bias_tile(g, near)
            m_old = mx_ref[g]
            m_new = jnp.maximum(m_old, jnp.max(s, axis=0, keepdims=True))
            alpha = jnp.exp(m_old - m_new)
            p = jnp.exp(s - m_new)
            l_ref[g] = alpha * l_ref[g] + jnp.sum(p, axis=0, keepdims=True)
            vt = vt_ref[0, j, g * HEAD_DIM:(g + 1) * HEAD_DIM, :]
            pv = jnp.dot(vt, p.astype(BF16), preferred_element_type=F32)
            acc_ref[g] = alpha * acc_ref[g] + pv
            mx_ref[g] = m_new

    def far_body(j, carry):
        update(j, None)
        return carry

    lax.fori_loop(0, jnp.maximum(i - 1, 0), far_body, 0)

    @pl.when(i >= 1)
    def _():
        update(i - 1, 1)

    update(i, 0)

    for g in range(N_KV_HEADS):
        o = acc_ref[g] / l_ref[g]
        for r in range(KV_GROUP):
            h = g * KV_GROUP + r
            o_ref[:, h * HEAD_DIM:(h + 1) * HEAD_DIM] = o[:, r * tq:(r + 1) * tq].T.astype(o_ref.dtype)


def _attention(qh, kvb, vt_tiles, mask, bias_t, batch, seq, tq, tk):
    nq, nk = seq // tq, seq // tk
    return pl.pallas_call(
        functools.partial(_attn_kernel, tq=tq, tk=tk),
        grid=(batch, nq),
        in_specs=[pl.BlockSpec((N_HEADS, tq, HEAD_DIM), lambda b, i: (0, b * nq + i, 0)),
                  pl.BlockSpec((seq, KV_WIDTH), lambda b, i: (b, 0)),
                  pl.BlockSpec((1, nk, KV_WIDTH, tk), lambda b, i: (b, 0, 0, 0)),
                  pl.BlockSpec((1, nk, tk, tq), lambda b, i: (b * nq + i, 0, 0, 0)),
                  pl.BlockSpec((2, N_HEADS, LANES, LANES), lambda b, i: (0, 0, 0, 0))],
        out_specs=pl.BlockSpec((tq, ATT_WIDTH), lambda b, i: (b * nq + i, 0)),
        out_shape=jax.ShapeDtypeStruct((batch * seq, ATT_WIDTH), BF16),
        scratch_shapes=[pltpu.VMEM((N_KV_HEADS, HEAD_DIM, KV_GROUP * tq), F32),
                        pltpu.VMEM((N_KV_HEADS, 1, KV_GROUP * tq), F32),
                        pltpu.VMEM((N_KV_HEADS, 1, KV_GROUP * tq), F32)],
        compiler_params=_cparams(("arbitrary", "arbitrary")),
        name="masked_attention",
    )(qh, kvb, vt_tiles, mask, bias_t)


def _conv_kernel(halo_ref, cur_ref, w_ref, b_ref, g_ref, bb_ref, o_ref, ext_ref, *, tt, tc, rc, zero_first):
    t = pl.program_id(1)
    halo = halo_ref[0]
    if zero_first:
        halo = jnp.where(t == 0, 0.0, halo)
    ext_ref[0:CONV_HALO, :] = halo
    ext_ref[CONV_HALO:CONV_HALO + tt, :] = cur_ref[0]
    first = CONV_HALO - (CONV_W - 1)
    for c in range(tc // LANES):
        cs = slice(c * LANES, (c + 1) * LANES)
        for r in range(tt // rc):
            acc = jnp.zeros((rc, LANES), F32) + b_ref[:, cs]
            for j in range(CONV_W):
                acc = acc + w_ref[j:j + 1, cs] * ext_ref[r * rc + first + j:r * rc + first + j + rc, cs]
            mu = jnp.mean(acc, axis=-1, keepdims=True)
            dv = acc - mu
            var = jnp.mean(dv * dv, axis=-1, keepdims=True)
            yn = dv * lax.rsqrt(var + EPS) * g_ref[:, cs] + bb_ref[:, cs]
            o_ref[0, r * rc:(r + 1) * rc, cs] = (yn * jax.nn.sigmoid(yn)).astype(o_ref.dtype)


def _conv(halo_src, cur, conv_w, conv_b, cn_g, cn_b, tt, tc, zero_first):
    b, t, c = cur.shape
    hb = tt // CONV_HALO
    if zero_first:
        halo_spec = pl.BlockSpec((1, CONV_HALO, tc), lambda a, i, j: (a, jnp.maximum(i * hb - 1, 0), j))
    else:
        halo_spec = pl.BlockSpec((1, CONV_HALO, tc), lambda a, i, j: (a, 0, j))
    vec = lambda: pl.BlockSpec((1, tc), lambda a, i, j: (0, j))
    return pl.pallas_call(
        functools.partial(_conv_kernel, tt=tt, tc=tc, rc=min(tt, 64), zero_first=zero_first),
        grid=(b, t // tt, c // tc),
        in_specs=[halo_spec,
                  pl.BlockSpec((1, tt, tc), lambda a, i, j: (a, i, j)),
                  pl.BlockSpec((CONV_W, tc), lambda a, i, j: (0, j)),
                  vec(), vec(), vec()],
        out_specs=pl.BlockSpec((1, tt, tc), lambda a, i, j: (a, i, j)),
        out_shape=jax.ShapeDtypeStruct((b, t, c), BF16),
        scratch_shapes=[pltpu.VMEM((CONV_HALO + tt, tc), F32)],
        compiler_params=_cparams(("arbitrary", "arbitrary", "arbitrary")),
        name="conformer_conv",
    )(halo_src, cur, conv_w, conv_b.reshape(1, c), cn_g.reshape(1, c), cn_b.reshape(1, c))


def _outproj_kernel(a_ref, c_ref, wa_ref, wc_ref, x_ref, gt_ref, o_ref):
    acc = jnp.dot(a_ref[0], wa_ref[...], preferred_element_type=F32)
    acc = acc + jnp.dot(c_ref[0], wc_ref[...], preferred_element_type=F32)
    o_ref[0] = x_ref[0] + gt_ref[0] * acc


def _outproj(att3, conv3, wo_a, wo_c, x3, gt3, tm, tn):
    g, r, d = x3.shape
    ka, kc = att3.shape[-1], conv3.shape[-1]
    gt_spec = (pl.BlockSpec((1, 1, tn), lambda a, i, j: (a, 0, j)) if gt3.shape[1] == 1
               else pl.BlockSpec((1, tm, tn), lambda a, i, j: (a, i, j)))
    return pl.pallas_call(
        _outproj_kernel,
        grid=(g, r // tm, d // tn),
        in_specs=[pl.BlockSpec((1, tm, ka), lambda a, i, j: (a, i, 0)),
                  pl.BlockSpec((1, tm, kc), lambda a, i, j: (a, i, 0)),
                  pl.BlockSpec((ka, tn), lambda a, i, j: (0, j)),
                  pl.BlockSpec((kc, tn), lambda a, i, j: (0, j)),
                  pl.BlockSpec((1, tm, tn), lambda a, i, j: (a, i, j)),
                  gt_spec],
        out_specs=pl.BlockSpec((1, tm, tn), lambda a, i, j: (a, i, j)),
        out_shape=jax.ShapeDtypeStruct((g, r, d), F32),
        compiler_params=_cparams(("arbitrary", "arbitrary", "arbitrary")),
        name="outproj_residual",
    )(att3, conv3, wo_a, wo_c, x3, gt3)


def _peer_scores_kernel(ht_ref, wqt_ref, sk_ref, o_ref):
    q_t = jnp.dot(wqt_ref[...], ht_ref[...], preferred_element_type=F32).astype(BF16)
    half = q_t.shape[0] // 2
    o_ref[0:PEER_NKEYS, :] = jnp.dot(sk_ref[0, 0], q_t[:half], preferred_element_type=F32)
    o_ref[PEER_NKEYS:2 * PEER_NKEYS, :] = jnp.dot(sk_ref[0, 1], q_t[half:], preferred_element_type=F32)


def _peer_scores(h2t, wq_t, sk, tm):
    d, n = h2t.shape
    dk = wq_t.shape[0] // PEER_HEADS
    return pl.pallas_call(
        _peer_scores_kernel,
        grid=(n // tm, PEER_HEADS),
        in_specs=[pl.BlockSpec((d, tm), lambda i, h: (0, i)),
                  pl.BlockSpec((dk, d), lambda i, h: (h, 0)),
                  pl.BlockSpec((1, 2, PEER_NKEYS, dk // 2), lambda i, h: (h, 0, 0, 0))],
        out_specs=pl.BlockSpec((2 * PEER_NKEYS, tm), lambda i, h: (h, i)),
        out_shape=jax.ShapeDtypeStruct((PEER_HEADS * 2 * PEER_NKEYS, n), F32),
        compiler_params=_cparams(("arbitrary", "arbitrary")),
        name="peer_subkey_scores",
    )(h2t, wq_t, sk)


def _top16(x):
    rows = x.shape[0]
    rid = lax.broadcasted_iota(I32, x.shape, 0)
    vals = []
    for k in range(PEER_TOPK):
        m = jnp.max(x, axis=0, keepdims=True)
        vals.append(m)
        if k + 1 < PEER_TOPK:
            first = jnp.min(jnp.where(x == m, rid, rows), axis=0, keepdims=True)
            x = jnp.where(rid == first, -jnp.inf, x)
    return vals


def _stack_rows(rows):
    shape = (len(rows), rows[0].shape[1])
    rid = lax.broadcasted_iota(I32, shape, 0)
    out = jnp.zeros(shape, rows[0].dtype)
    for k, row in enumerate(rows):
        out = jnp.where(rid == k, row, out)
    return out


def _peer_route_kernel(s_ref, s1m_ref, s2m_ref, c1_ref, e2_ref, thr_ref):
    s1 = s_ref[0:PEER_NKEYS, :]
    s2 = s_ref[PEER_NKEYS:2 * PEER_NKEYS, :]
    t1 = _top16(s1)
    t2 = _top16(s2)
    t2_all = _stack_rows(t2)
    blocks = [t1[0] + t2_all]
    for a in range(1, 8):
        blocks.append(t1[a] + t2_all[0:8])
    blocks.append(_stack_rows(t1[8:16]) + t2[0])
    top = _top16(jnp.concatenate(blocks, axis=0))
    z = jnp.ones_like(top[0])
    for k in range(1, PEER_TOPK):
        z = z + jnp.exp(top[k] - top[0])
    s1m_ref[...] = jnp.where(s1 >= t1[PEER_TOPK - 1], s1, -jnp.inf)
    s2m_ref[...] = jnp.where(s2 >= t2[PEER_TOPK - 1], s2, -jnp.inf)
    c1_ref[...] = jnp.exp(s1 - t1[0]) / z
    e2_ref[...] = jnp.exp(s2 - t2[0])
    thr_ref[0] = top[PEER_TOPK - 1]


def _peer_route(s_t, tl):
    rows, n = s_t.shape
    big = lambda: pl.BlockSpec((PEER_NKEYS, tl), lambda h, t: (h, t))
    big_shape = jax.ShapeDtypeStruct((PEER_HEADS * PEER_NKEYS, n), F32)
    return pl.pallas_call(
        _peer_route_kernel,
        grid=(PEER_HEADS, n // tl),
        in_specs=[pl.BlockSpec((2 * PEER_NKEYS, tl), lambda h, t: (h, t))],
        out_specs=[big(), big(), big(), big(), pl.BlockSpec((1, 1, tl), lambda h, t: (h, 0, t))],
        out_shape=[big_shape, big_shape, big_shape, big_shape,
                   jax.ShapeDtypeStruct((PEER_HEADS, 1, n), F32)],
        compiler_params=_cparams(("arbitrary", "arbitrary")),
        name="peer_route",
    )(s_t)


GATE_ROWS = 64


MXU_COLS = 256


def _peer_dense_kernel(ht_ref, u_ref, vt_ref, s1_ref, c1_ref, s2_ref, e2_ref, thr_ref, o_ref, *, te):
    e = pl.program_id(1)
    tm = ht_ref.shape[1]
    rows_per_tile = te // PEER_NKEYS

    @pl.when(e == 0)
    def _():
        o_ref[...] = jnp.zeros(o_ref.shape, F32)

    chunk = min(MXU_COLS, tm)
    a_chunks = [jnp.dot(u_ref[...], ht_ref[:, c * chunk:(c + 1) * chunk], preferred_element_type=F32)
                for c in range(tm // chunk)]

    s1rows = [[s1_ref[pl.ds(h * PEER_NKEYS + e * rows_per_tile + r, 1), :] for r in range(rows_per_tile)]
              for h in range(PEER_HEADS)]
    c1rows = [[c1_ref[pl.ds(h * PEER_NKEYS + e * rows_per_tile + r, 1), :] for r in range(rows_per_tile)]
              for h in range(PEER_HEADS)]
    for c in range(tm // chunk):
        w_cols = []
        for lc in range(chunk // LANES):
            ls = slice(c * chunk + lc * LANES, c * chunk + (lc + 1) * LANES)
            als = slice(lc * LANES, (lc + 1) * LANES)
            blocks = [[None] * (PEER_NKEYS // GATE_ROWS) for _ in range(rows_per_tile)]
            for part in range(PEER_NKEYS // GATE_ROWS):
                rs = [slice(r * PEER_NKEYS + part * GATE_ROWS, r * PEER_NKEYS + (part + 1) * GATE_ROWS)
                      for r in range(rows_per_tile)]
                acc = [jnp.zeros((GATE_ROWS, LANES), F32) for _ in range(rows_per_tile)]
                for h in range(PEER_HEADS):
                    row0 = h * PEER_NKEYS + part * GATE_ROWS
                    s2 = s2_ref[row0:row0 + GATE_ROWS, ls]
                    e2 = e2_ref[row0:row0 + GATE_ROWS, ls]
                    thr = thr_ref[h, :, ls]
                    for r in range(rows_per_tile):
                        cand = s2 + s1rows[h][r][:, ls]
                        acc[r] = acc[r] + jnp.where(cand >= thr, e2, 0.0) * c1rows[h][r][:, ls]
                for r in range(rows_per_tile):
                    blocks[r][part] = (acc[r] * jax.nn.gelu(a_chunks[c][rs[r], als])).astype(BF16)
            w_cols.append(jnp.concatenate([b for row in blocks for b in row], axis=0))
        cs = slice(c * chunk, (c + 1) * chunk)
        o_ref[:, cs] += jnp.dot(vt_ref[...], jnp.concatenate(w_cols, axis=1), preferred_element_type=F32)


def _peer_dense(h2t, u_bf, vt_bf, s1m, c1, s2m, e2, thr, tm, te):
    d, n = h2t.shape
    rows = PEER_HEADS * PEER_NKEYS
    once = pl.Buffered(1)
    tok = lambda: pl.BlockSpec((rows, tm), lambda i, e: (0, i), pipeline_mode=once)
    return pl.pallas_call(
        functools.partial(_peer_dense_kernel, te=te),
        grid=(n // tm, u_bf.shape[0] // te),
        in_specs=[pl.BlockSpec((d, tm), lambda i, e: (0, i), pipeline_mode=once),
                  pl.BlockSpec((te, d), lambda i, e: (e, 0)),
                  pl.BlockSpec((d, te), lambda i, e: (0, e)),
                  tok(), tok(), tok(), tok(),
                  pl.BlockSpec((PEER_HEADS, 1, tm), lambda i, e: (0, 0, i), pipeline_mode=once)],
        out_specs=pl.BlockSpec((d, tm), lambda i, e: (0, i)),
        out_shape=jax.ShapeDtypeStruct((d, n), F32),
        compiler_params=_cparams(("arbitrary", "arbitrary")),
        name="peer_dense_experts",
    )(h2t, u_bf, vt_bf, s1m, c1, s2m, e2, thr)


def _final_kernel(x_ref, gt_ref, p_ref, g_ref, o_ref, *, normalize):
    xx = x_ref[0] + gt_ref[0] * p_ref[...].T
    if normalize:
        ms = jnp.mean(xx * xx, axis=-1, keepdims=True)
        xx = xx * lax.rsqrt(ms + EPS) * g_ref[...]
    o_ref[0] = xx


def _final(x3, gt3, peer_t, g_final, tm, normalize):
    g, r, d = x3.shape
    nb = r // tm
    return pl.pallas_call(
        functools.partial(_final_kernel, normalize=normalize),
        grid=(g, nb),
        in_specs=[pl.BlockSpec((1, tm, d), lambda a, i: (a, i, 0)),
                  _row_mod_spec(gt3, tm),
                  pl.BlockSpec((d, tm), lambda a, i: (0, a * nb + i)),
                  pl.BlockSpec((1, d), lambda a, i: (0, 0))],
        out_specs=pl.BlockSpec((1, tm, d), lambda a, i: (a, i, 0)),
        out_shape=jax.ShapeDtypeStruct((g, r, d), F32),
        compiler_params=_cparams(("arbitrary", "arbitrary")),
        name="final_norm",
    )(x3, gt3, peer_t, g_final.reshape(1, d))


def _sample_index_kernel(pt_ref, qi_ref, wi_ref, kself_ref, expand_ref, *rest, n_pages, topk):
    pages = rest[:PAGES_PER_STEP]
    mask_ref, self_ref, sc_ref = rest[PAGES_PER_STEP:]
    step = pl.program_id(1)
    n_steps = n_pages // PAGES_PER_STEP
    step_keys = PAGES_PER_STEP * PAGE_SIZE
    score_scale = IDX_DIM ** -0.5 * IDX_HEADS ** -0.5
    qi = qi_ref[0]
    wi = wi_ref[0]
    kcat = jnp.concatenate([pages[k][0].astype(BF16) for k in range(PAGES_PER_STEP)], axis=0)
    s = _dot_nt(qi, kcat)
    sc_ref[pl.ds(step, 1), :] = jnp.sum(jnp.maximum(s, 0.0) * wi, axis=0, keepdims=True) * score_scale

    @pl.when(step == n_steps - 1)
    def _():
        ks = kself_ref[0].astype(BF16).astype(F32)
        s_self = jnp.sum(qi.astype(F32) * ks, axis=1, keepdims=True)
        self_score = jnp.sum(jnp.maximum(s_self, 0.0) * wi, axis=0, keepdims=True) * score_scale
        keys = _sortable(sc_ref[...])
        kself = _sortable(self_score)
        pos = (lax.broadcasted_iota(I32, keys.shape, 0) * step_keys
               + lax.broadcasted_iota(I32, keys.shape, 1))
        self_pos = n_pages * PAGE_SIZE

        def total(x, xs):
            return jnp.sum(jnp.sum(x, axis=1, keepdims=True), axis=0, keepdims=True) + xs

        def bit_body(t, thr):
            cand = thr ^ lax.shift_left(jnp.int32(1), 31 - t)
            cnt = total(jnp.where(keys >= cand, 1, 0), jnp.where(kself >= cand, 1, 0))
            return jnp.where(cnt >= topk, cand, thr)

        thr = lax.fori_loop(0, 32, bit_body, jnp.full((1, 1), INT_MIN, I32))
        need = topk - total(jnp.where(keys > thr, 1, 0), jnp.where(kself > thr, 1, 0))

        def idx_body(t, c):
            cand = c | lax.shift_left(jnp.int32(1), 15 - t)
            f = total(jnp.where(keys == thr, jnp.where(pos < cand, 1, 0), 0),
                      jnp.where(kself == thr, jnp.where(self_pos < cand, 1, 0), 0))
            return jnp.where(f < need, cand, c)

        cstar = lax.fori_loop(0, 16, idx_body, jnp.zeros((1, 1), I32))
        picked = jnp.where(keys > thr, 1.0, jnp.where(keys == thr, jnp.where(pos <= cstar, 1.0, 0.0), 0.0))
        parts = []
        for k in range(PAGES_PER_STEP):
            flags = picked[:, k * PAGE_SIZE:(k + 1) * PAGE_SIZE].astype(BF16)
            parts.append(jnp.dot(flags, expand_ref[...], preferred_element_type=F32))
        mask_ref[0] = jnp.where(jnp.concatenate(parts, axis=1) > 0.5, 0.0, NEG)
        ssel = jnp.where(kself > thr, 0.0, jnp.where(kself == thr, jnp.where(self_pos <= cstar, 0.0, NEG), NEG))
        self_ref[0] = jnp.zeros((SUBLANES, LANES), F32) + ssel


def _sample_index(page_table, qi3, wi3, kself3, cki, page_base):
    db, n_pages = page_table.shape
    topk = min(TOPK_MAX, (n_pages * PAGE_SIZE + 1) // 4)
    n_steps = n_pages // PAGES_PER_STEP
    step_keys = PAGES_PER_STEP * PAGE_SIZE
    page_rows = PAGE_SIZE * N_KV_HEADS
    expand = (jnp.arange(page_rows, dtype=I32)[None, :] // N_KV_HEADS
              == jnp.arange(PAGE_SIZE, dtype=I32)[:, None]).astype(BF16)

    def page_spec(k):
        return pl.BlockSpec((1, PAGE_SIZE, IDX_DIM),
                            lambda b, s, pt: (page_base + pt[b, s * PAGES_PER_STEP + k], 0, 0))

    grid_spec = pltpu.PrefetchScalarGridSpec(
        num_scalar_prefetch=1,
        grid=(db, n_steps),
        in_specs=[pl.BlockSpec((1, IDX_HEADS, IDX_DIM), lambda b, s, pt: (b, 0, 0)),
                  pl.BlockSpec((1, IDX_HEADS, 1), lambda b, s, pt: (b, 0, 0)),
                  pl.BlockSpec((1, 1, IDX_DIM), lambda b, s, pt: (b, 0, 0)),
                  pl.BlockSpec((PAGE_SIZE, page_rows), lambda b, s, pt: (0, 0))]
                 + [page_spec(k) for k in range(PAGES_PER_STEP)],
        out_specs=[pl.BlockSpec((1, n_steps, PAGES_PER_STEP * page_rows), lambda b, s, pt: (b, 0, 0)),
                   pl.BlockSpec((1, SUBLANES, LANES), lambda b, s, pt: (b, 0, 0))],
        scratch_shapes=[pltpu.VMEM((n_steps, step_keys), F32)],
    )
    return pl.pallas_call(
        functools.partial(_sample_index_kernel, n_pages=n_pages, topk=topk),
        grid_spec=grid_spec,
        out_shape=[jax.ShapeDtypeStruct((db, n_steps, PAGES_PER_STEP * page_rows), F32),
                   jax.ShapeDtypeStruct((db, SUBLANES, LANES), F32)],
        compiler_params=_cparams(("arbitrary", "arbitrary")),
        name="sample_indexer",
    )(page_table, qi3, wi3, kself3, expand, *([cki] * PAGES_PER_STEP))


def _sample_attn_kernel(pt_ref, q_ref, kself_ref, vself_ref, mask_ref, self_ref, bias_ref, bself_ref,
                        *rest, n_pages):
    kpages = rest[:PAGES_PER_STEP]
    vpages = rest[PAGES_PER_STEP:2 * PAGES_PER_STEP]
    o_ref, acc_ref, mx_ref, l_ref = rest[2 * PAGES_PER_STEP:]
    step = pl.program_id(1)
    n_steps = n_pages // PAGES_PER_STEP
    sm_scale = HEAD_DIM ** -0.5
    q = q_ref[0]
    head_group = lax.broadcasted_iota(I32, (N_HEADS, HEAD_DIM), 0) // KV_GROUP

    def own_group_rows(x_ref):
        out = jnp.zeros((N_HEADS, HEAD_DIM), F32)
        for g in range(N_KV_HEADS):
            out = jnp.where(head_group == g, x_ref[0, g:g + 1, :].astype(BF16).astype(F32), out)
        return out

    @pl.when(step == 0)
    def _():
        logit = jnp.sum(q.astype(F32) * own_group_rows(kself_ref), axis=1, keepdims=True)
        mx_ref[...] = logit * sm_scale + bself_ref[...] + self_ref[0, 0:1, 0:1]
        l_ref[...] = jnp.ones(l_ref.shape, F32)
        acc_ref[...] = own_group_rows(vself_ref)

    kcat = jnp.concatenate([kpages[k][0].astype(BF16) for k in range(PAGES_PER_STEP)], axis=0)
    vcat = jnp.concatenate([vpages[k][0].astype(BF16) for k in range(PAGES_PER_STEP)], axis=0)
    s = _dot_nt(q, kcat) * sm_scale + bias_ref[step] + mask_ref[0, pl.ds(step, 1), :]
    m_old = mx_ref[...]
    m_new = jnp.maximum(m_old, jnp.max(s, axis=1, keepdims=True))
    alpha = jnp.exp(m_old - m_new)
    p = jnp.exp(s - m_new)
    l_ref[...] = alpha * l_ref[...] + jnp.sum(p, axis=1, keepdims=True)
    acc_ref[...] = alpha * acc_ref[...] + jnp.dot(p.astype(BF16), vcat, preferred_element_type=F32)
    mx_ref[...] = m_new

    @pl.when(step == n_steps - 1)
    def _():
        o_ref[0] = (acc_ref[...] / l_ref[...]).astype(o_ref.dtype)


def _sample_attn(page_table, q3, kself3, vself3, mask, selfsel, bias_steps, bias_self, ck, cv, page_base):
    db, n_pages = page_table.shape
    n_steps = n_pages // PAGES_PER_STEP
    page_rows = PAGE_SIZE * N_KV_HEADS
    step_rows = PAGES_PER_STEP * page_rows

    def page_spec(k):
        return pl.BlockSpec((1, page_rows, HEAD_DIM),
                            lambda b, s, pt: (page_base + pt[b, s * PAGES_PER_STEP + k], 0, 0))

    per_b = lambda shape: pl.BlockSpec((1,) + shape, lambda b, s, pt: (b, 0, 0))
    grid_spec = pltpu.PrefetchScalarGridSpec(
        num_scalar_prefetch=1,
        grid=(db, n_steps),
        in_specs=[per_b((N_HEADS, HEAD_DIM)), per_b((N_KV_HEADS, HEAD_DIM)), per_b((N_KV_HEADS, HEAD_DIM)),
                  per_b((n_steps, step_rows)), per_b((SUBLANES, LANES)),
                  pl.BlockSpec((n_steps, N_HEADS, step_rows), lambda b, s, pt: (0, 0, 0)),
                  pl.BlockSpec((N_HEADS, 1), lambda b, s, pt: (0, 0))]
                 + [page_spec(k) for k in range(PAGES_PER_STEP)] * 2,
        out_specs=per_b((N_HEADS, HEAD_DIM)),
        scratch_shapes=[pltpu.VMEM((N_HEADS, HEAD_DIM), F32),
                        pltpu.VMEM((N_HEADS, 1), F32),
                        pltpu.VMEM((N_HEADS, 1), F32)],
    )
    return pl.pallas_call(
        functools.partial(_sample_attn_kernel, n_pages=n_pages),
        grid_spec=grid_spec,
        out_shape=jax.ShapeDtypeStruct((db, N_HEADS, HEAD_DIM), BF16),
        compiler_params=_cparams(("arbitrary", "arbitrary")),
        name="sample_attention",
    )(page_table, q3, kself3, vself3, mask, selfsel, bias_steps, bias_self,
      *([ck] * PAGES_PER_STEP), *([cv] * PAGES_PER_STEP))


def _rel_bucket(dist):
    n = jnp.maximum(dist, 0)
    max_exact = REL_BUCKETS // 2
    nf = jnp.maximum(n, 1).astype(F32)
    large = max_exact + (jnp.log(nf / max_exact) / math.log(REL_MAX_DIST / max_exact)
                         * (REL_BUCKETS - max_exact)).astype(I32)
    large = jnp.minimum(large, REL_BUCKETS - 1)
    return jnp.where(n < max_exact, n, large)


def _prompt_bias_tiles(rel_bias):
    kc = jnp.arange(LANES, dtype=I32)[:, None]
    qr = jnp.arange(LANES, dtype=I32)[None, :]
    far = rel_bias[REL_BUCKETS - 1]
    tiles = []
    for off in (0, LANES):
        dist = off + qr - kc
        t = rel_bias[_rel_bucket(dist)] - far
        tiles.append(jnp.where((dist >= 0)[..., None], t, 0.0).transpose(2, 0, 1))
    return jnp.stack(tiles).astype(F32)


def _sample_bias_steps(rel_bias, n_pages):
    past = n_pages * PAGE_SIZE
    bias = rel_bias[_rel_bucket(past - jnp.arange(past, dtype=I32))]
    own = (jnp.arange(N_HEADS, dtype=I32)[:, None] // KV_GROUP) == jnp.arange(N_KV_HEADS, dtype=I32)[None, :]
    rows = jnp.where(own[None], bias[:, :, None], NEG)
    rows = rows.transpose(1, 0, 2).reshape(N_HEADS, n_pages // PAGES_PER_STEP, -1)
    return rows.transpose(1, 0, 2).astype(F32)


def _split_in_proj(w_in_l):
    sizes = (ATT_WIDTH, KV_WIDTH, KV_WIDTH, IDX_HEADS * IDX_DIM, IDX_DIM, IDX_HEADS)
    offs = [0]
    for s in sizes:
        offs.append(offs[-1] + s)
    conv_ch = (w_in_l.shape[1] - offs[-1]) // 2
    d = w_in_l.shape[0]
    wq, wk, wv, wqi, wki, wwi = (w_in_l[:, offs[i]:offs[i + 1]] for i in range(6))
    wua = w_in_l[:, offs[-1]:offs[-1] + conv_ch]
    wub = w_in_l[:, offs[-1] + conv_ch:]
    z = lambda n: jnp.zeros((d, n), w_in_l.dtype)
    w_a = jnp.concatenate([wk, wv, wki, z(LANES - IDX_DIM), z(LANES - IDX_DIM), wki,
                           wwi, z(LANES - IDX_HEADS), z(LANES)], axis=1)
    w_b = jnp.concatenate([wq, wqi], axis=1)
    chunk = 256
    parts = []
    for c in range(conv_ch // chunk):
        parts += [wua[:, c * chunk:(c + 1) * chunk], wub[:, c * chunk:(c + 1) * chunk]]
    w_c = jnp.concatenate(parts, axis=1)
    return w_a.astype(BF16), w_b.astype(BF16), w_c.astype(BF16)


COL_K, COL_V, COL_KA, COL_WI = 0, KV_WIDTH, 2 * KV_WIDTH, 2 * KV_WIDTH + 2 * LANES


def _mixer_projections(h, w_a, w_b, w_c, tm):
    z_a = _matmul(h, w_a, F32, tm, 512)
    qh = _matmul(h, w_b, BF16, tm, 512, head_major=True)
    u = _matmul(h, w_c, F32, tm, 512, glu=True)
    return z_a, qh, u


def _peer_block(h2t, wq_t, sk_bf, u_bf, vt_bf, tm, te):
    s_t = _peer_scores(h2t, wq_t, sk_bf, tm)
    s1m, s2m, c1, e2, thr = _peer_route(s_t, min(tm, 256))
    return _peer_dense(h2t, u_bf, vt_bf, s1m, c1, s2m, e2, thr, tm, te)


def kernel(x_prompt, x_sample, cache_k, cache_v, cache_kidx, state_conv, page_table, c_prompt, c_sample,
           rel_bias, w_ada, b_ada, g_mix, w_in, conv_w, conv_b, cn_g, cn_b, w_o, g_ch, peer_wq,
           peer_subkeys, peer_u, peer_v, g_final):
    batch, seq, d = x_prompt.shape
    db = x_sample.shape[0]
    depth = w_ada.shape[0]
    n_pages = page_table.shape[1]
    past = n_pages * PAGE_SIZE
    conv_ch = conv_w.shape[-1]
    n_prompt = batch * seq
    tq = tk = 256

    xp = x_prompt
    xs = jnp.pad(x_sample.reshape(1, db, d), ((0, 0), (0, SAMPLE_ROWS - db), (0, 0)))
    c_rows = batch + db
    c_pad = (-c_rows) % 16
    c_all = jnp.pad(jnp.concatenate([c_prompt, c_sample], axis=0), ((0, c_pad), (0, 0)))
    bias_t = _prompt_bias_tiles(rel_bias)
    bias_steps = _sample_bias_steps(rel_bias, n_pages)
    n_pool = cache_k.shape[1]
    page_rows = PAGE_SIZE * N_KV_HEADS
    ck_rows = cache_k.reshape(depth * n_pool, page_rows, HEAD_DIM)
    cv_rows = cache_v.reshape(depth * n_pool, page_rows, HEAD_DIM)
    cki_rows = cache_kidx.reshape(depth * n_pool, PAGE_SIZE, IDX_DIM)
    bias_self = rel_bias[_rel_bucket(jnp.zeros((), I32))].reshape(N_HEADS, 1)

    outs = {k: [] for k in ("kp", "vp", "kip", "cp", "ks", "vs", "kis", "cs")}
    for l in range(depth):
        mods = _adaln(c_all, w_ada[l], b_ada[l])
        p_mod = [m.reshape(batch, 1, d) for m in jnp.split(mods[:batch], 6, axis=-1)]
        s_mod = [jnp.pad(m.reshape(1, db, d), ((0, 0), (0, SAMPLE_ROWS - db), (0, 0)))
                 for m in jnp.split(mods[batch:c_rows], 6, axis=-1)]
        w_a, w_b, w_c = _split_in_proj(w_in[l])
        wo_a = w_o[l][:ATT_WIDTH].astype(BF16)
        wo_c = w_o[l][ATT_WIDTH:].astype(BF16)

        hp = _modulate(xp, g_mix[l], p_mod[1], p_mod[0], 512).reshape(n_prompt, d)
        z_a, qh, u = _mixer_projections(hp, w_a, w_b, w_c, 1024)
        kvb = z_a[:, :COL_WI].astype(BF16)
        wi_t = z_a[:, COL_WI:COL_WI + IDX_HEADS].T
        vt_tiles = (kvb[:, COL_V:COL_V + KV_WIDTH].reshape(batch, seq // tk, tk, KV_WIDTH)
                    .transpose(0, 1, 3, 2))
        mask = _indexer(qh, kvb, wi_t, batch, seq, tq, tk, qi_blk=N_HEADS // (IDX_HEADS // 2),
                        ka_blk=COL_KA // LANES)
        att = _attention(qh, kvb, vt_tiles, mask, bias_t, batch, seq, tq, tk)
        u3 = u.reshape(batch, seq, conv_ch)
        conv = _conv(u3, u3, conv_w[l], conv_b[l], cn_g[l], cn_b[l], 256, 512, zero_first=True)
        xp = _outproj(att.reshape(batch, seq, ATT_WIDTH), conv, wo_a, wo_c, xp, p_mod[2], 1024, 512)
        outs["kp"].append(z_a[:, COL_K:COL_K + KV_WIDTH].reshape(batch, seq, N_KV_HEADS, HEAD_DIM))
        outs["vp"].append(z_a[:, COL_V:COL_V + KV_WIDTH].reshape(batch, seq, N_KV_HEADS, HEAD_DIM))
        outs["kip"].append(z_a[:, COL_KA:COL_KA + IDX_DIM].reshape(batch, seq, IDX_DIM))
        outs["cp"].append(u3[:, seq - (CONV_W - 1):])

        hs = _modulate(xs, g_mix[l], s_mod[1], s_mod[0], SAMPLE_ROWS).reshape(SAMPLE_ROWS, d)
        zs_a, qhs, us = _mixer_projections(hs, w_a, w_b, w_c, SAMPLE_ROWS)
        k_new = zs_a[:db, COL_K:COL_K + KV_WIDTH]
        v_new = zs_a[:db, COL_V:COL_V + KV_WIDTH]
        ki_new = zs_a[:db, COL_KA:COL_KA + IDX_DIM]
        wi_new = zs_a[:db, COL_WI:COL_WI + IDX_HEADS]
        q_s = qhs[:N_HEADS, :db].transpose(1, 0, 2)
        qi_s = (qhs[N_HEADS:, :db].transpose(1, 0, 2)
                .reshape(db, IDX_HEADS // 2, 2, IDX_DIM).reshape(db, IDX_HEADS, IDX_DIM))
        smask, sself = _sample_index(page_table, qi_s, wi_new.reshape(db, IDX_HEADS, 1),
                                     ki_new.reshape(db, 1, IDX_DIM), cki_rows, l * n_pool)
        att_s = _sample_attn(page_table, q_s, k_new.reshape(db, N_KV_HEADS, HEAD_DIM),
                             v_new.reshape(db, N_KV_HEADS, HEAD_DIM), smask, sself, bias_steps, bias_self,
                             ck_rows, cv_rows, l * n_pool)
        att_s = jnp.pad(att_s.reshape(1, db, ATT_WIDTH), ((0, 0), (0, SAMPLE_ROWS - db), (0, 0)))
        u_new = us[:db]
        state = state_conv[l].astype(F32)
        halo = jnp.pad(state, ((0, 0), (CONV_HALO - (CONV_W - 1), 0), (0, 0)))
        cur = jnp.pad(u_new.reshape(db, 1, conv_ch), ((0, 0), (0, SUBLANES - 1), (0, 0)))
        conv_s = _conv(halo, cur, conv_w[l], conv_b[l], cn_g[l], cn_b[l], SUBLANES, 512, zero_first=False)
        conv_s = jnp.pad(conv_s[:, 0].reshape(1, db, conv_ch), ((0, 0), (0, SAMPLE_ROWS - db), (0, 0)))
        xs = _outproj(att_s, conv_s, wo_a, wo_c, xs, s_mod[2], SAMPLE_ROWS, 512)
        outs["ks"].append(k_new.reshape(db, 1, N_KV_HEADS, HEAD_DIM))
        outs["vs"].append(v_new.reshape(db, 1, N_KV_HEADS, HEAD_DIM))
        outs["kis"].append(ki_new.reshape(db, 1, IDX_DIM))
        outs["cs"].append(jnp.concatenate([state[:, 1:], u_new.reshape(db, 1, conv_ch)], axis=1))

        wq_t = peer_wq[l].astype(BF16).T
        sk_bf = peer_subkeys[l].astype(BF16)
        u_bf = peer_u[l].astype(BF16)
        vt_bf = peer_v[l].astype(BF16).T
        hp2 = _modulate(xp, g_ch[l], p_mod[4], p_mod[3], 512, transposed=True)
        peer_p = _peer_block(hp2, wq_t, sk_bf, u_bf, vt_bf, 512, 512)
        hs2 = _modulate(xs, g_ch[l], s_mod[4], s_mod[3], SAMPLE_ROWS, transposed=True)
        peer_s = _peer_block(hs2, wq_t, sk_bf, u_bf, vt_bf, SAMPLE_ROWS, 512)
        last = l == depth - 1
        xp = _final(xp, p_mod[5], peer_p, g_final, 256, normalize=last)
        xs = _final(xs, s_mod[5], peer_s, g_final, SAMPLE_ROWS, normalize=last)

    st = lambda k: jnp.stack(outs[k])
    y_sample = xs[0, :db].reshape(db, 1, d)
    return (xp, y_sample, st("kp"), st("vp"), st("kip"), st("cp"),
            st("ks"), st("vs"), st("kis"), st("cs"))
```

```python
import functools
import math

import jax
import jax.numpy as jnp
from jax import lax
from jax.experimental import pallas as pl
from jax.experimental.pallas import tpu as pltpu

F32 = jnp.float32
BF16 = jnp.bfloat16
I32 = jnp.int32

HEAD_DIM = 128
N_HEADS = 16
N_KV_HEADS = 4
KV_GROUP = N_HEADS // N_KV_HEADS
ATT_WIDTH = N_HEADS * HEAD_DIM
KV_WIDTH = N_KV_HEADS * HEAD_DIM
IDX_HEADS = 16
IDX_DIM = 64
TOPK_MAX = 256
REL_BUCKETS = 32
REL_MAX_DIST = 128
CONV_W = 31
PEER_HEADS = 8
PEER_NKEYS = 128
PEER_TOPK = 16
EPS = 1e-6
PAGE_SIZE = 128

LANES = 128
SUBLANES = 8
VMEM_LIMIT = 56 * 1024 * 1024

NEG = -1e30
INT_MIN = -(2 ** 31)
CONV_HALO = 32
SAMPLE_ROWS = 128
INDEX_PAGES_PER_STEP = 32
ATTN_PAGES_PER_STEP = 16


def _cparams(sem, flags=None):
    return pltpu.CompilerParams(dimension_semantics=sem, vmem_limit_bytes=VMEM_LIMIT, flags=flags)


def _dot_nt(a, b):
    return lax.dot_general(a, b, (((1,), (1,)), ((), ())), preferred_element_type=F32)


def _sortable(x):
    bits = pltpu.bitcast(x, I32)
    return bits ^ ((bits >> 31) & jnp.int32(0x7FFFFFFF))


def _adaln_kernel(c_ref, w_ref, b_ref, o_ref):
    c = c_ref[...]
    a = (c * jax.nn.sigmoid(c)).astype(BF16)
    o_ref[...] = jnp.dot(a, w_ref[...].astype(BF16), preferred_element_type=F32) + b_ref[...]


def _adaln(c, w_ada, b_ada, tn=512):
    r, d = c.shape
    n = w_ada.shape[1]
    return pl.pallas_call(
        _adaln_kernel,
        grid=(n // tn,),
        in_specs=[pl.BlockSpec((r, d), lambda j: (0, 0)),
                  pl.BlockSpec((d, tn), lambda j: (0, j)),
                  pl.BlockSpec((1, tn), lambda j: (0, j))],
        out_specs=pl.BlockSpec((r, tn), lambda j: (0, j)),
        out_shape=jax.ShapeDtypeStruct((r, n), F32),
        compiler_params=_cparams(("arbitrary",)),
        name="adaln",
    )(c, w_ada, b_ada.reshape(1, n))


def _modulate_kernel(x_ref, g_ref, sc_ref, sh_ref, o_ref, *, transposed):
    x = x_ref[0]
    ms = jnp.mean(x * x, axis=-1, keepdims=True)
    y = x * lax.rsqrt(ms + EPS) * g_ref[...]
    y = y * (1.0 + sc_ref[0]) + sh_ref[0]
    if transposed:
        o_ref[...] = y.T.astype(o_ref.dtype)
    else:
        o_ref[0] = y.astype(o_ref.dtype)


def _row_mod_spec(mod, tr):
    d = mod.shape[-1]
    if mod.shape[1] == 1:
        return pl.BlockSpec((1, 1, d), lambda g, r, *_: (g, 0, 0))
    return pl.BlockSpec((1, tr, d), lambda g, r, *_: (g, r, 0))


def _modulate(x3, gain, sc3, sh3, tr, transposed=False):
    g, r, d = x3.shape
    nb = r // tr
    if transposed:
        out_spec = pl.BlockSpec((d, tr), lambda a, b: (0, a * nb + b))
        out_shape = jax.ShapeDtypeStruct((d, g * r), BF16)
    else:
        out_spec = pl.BlockSpec((1, tr, d), lambda a, b: (a, b, 0))
        out_shape = jax.ShapeDtypeStruct((g, r, d), BF16)
    return pl.pallas_call(
        functools.partial(_modulate_kernel, transposed=transposed),
        grid=(g, nb),
        in_specs=[pl.BlockSpec((1, tr, d), lambda a, b: (a, b, 0)),
                  pl.BlockSpec((1, d), lambda a, b: (0, 0)),
                  _row_mod_spec(sc3, tr),
                  _row_mod_spec(sh3, tr)],
        out_specs=out_spec,
        out_shape=out_shape,
        compiler_params=_cparams(("arbitrary", "arbitrary")),
        name="modulate",
    )(x3, gain.reshape(1, d), sc3, sh3)


def _mm_kernel(h_ref, w_ref, o_ref, *, glu, head_major):
    acc = jnp.dot(h_ref[...], w_ref[...], preferred_element_type=F32)
    if glu:
        half = acc.shape[1] // 2
        acc = acc[:, :half] * jax.nn.sigmoid(acc[:, half:])
    if head_major:
        for c in range(acc.shape[1] // LANES):
            o_ref[c] = acc[:, c * LANES:(c + 1) * LANES].astype(o_ref.dtype)
    else:
        o_ref[...] = acc.astype(o_ref.dtype)


def _matmul(h, w, out_dtype, tm, tn, glu=False, head_major=False):
    m, k = h.shape
    n = w.shape[1]
    n_out = n // 2 if glu else n
    tn_out = tn // 2 if glu else tn
    if head_major:
        out_shape = jax.ShapeDtypeStruct((n_out // LANES, m, LANES), out_dtype)
        out_spec = pl.BlockSpec((tn_out // LANES, tm, LANES), lambda i, j: (j, i, 0))
    else:
        out_shape = jax.ShapeDtypeStruct((m, n_out), out_dtype)
        out_spec = pl.BlockSpec((tm, tn_out), lambda i, j: (i, j))
    return pl.pallas_call(
        functools.partial(_mm_kernel, glu=glu, head_major=head_major),
        grid=(m // tm, n // tn),
        in_specs=[pl.BlockSpec((tm, k), lambda i, j: (i, 0)),
                  pl.BlockSpec((k, tn), lambda i, j: (0, j))],
        out_specs=out_spec,
        out_shape=out_shape,
        compiler_params=_cparams(("arbitrary", "arbitrary")),
        name="proj_matmul",
    )(h, w)


def _indexer_kernel(qi_ref, ka_ref, kb_ref, wi_ref, o_ref, keys_ref, cst_ref, *, tq, tk, nk, topk):
    i = pl.program_id(1)
    q0 = i * tq
    nvis = (q0 + tq + tk - 1) // tk
    qpos = q0 + lax.broadcasted_iota(I32, (1, tq), 1)
    w = wi_ref[...]
    score_scale = IDX_DIM ** -0.5 * IDX_HEADS ** -0.5

    def kpos_of(j):
        return j * tk + lax.broadcasted_iota(I32, (tk, 1), 0)

    def score_body(j, carry):
        k0 = pl.multiple_of(j * tk, tk)
        ka = ka_ref[pl.ds(k0, tk), :]
        kb = kb_ref[pl.ds(k0, tk), :]
        acc = jnp.zeros((tk, tq), F32)
        for p in range(IDX_HEADS // 2):
            qp = qi_ref[p]
            sa = _dot_nt(ka, qp)
            sb = _dot_nt(kb, qp)
            acc = acc + jnp.maximum(sa, 0.0) * w[2 * p:2 * p + 1]
            acc = acc + jnp.maximum(sb, 0.0) * w[2 * p + 1:2 * p + 2]
        acc = acc * score_scale
        acc = jnp.where(kpos_of(j) <= qpos, acc, -jnp.inf)
        keys_ref[j] = _sortable(acc)
        return carry

    lax.fori_loop(0, nvis, score_body, 0)

    def count(pred):
        def body(j, c):
            return c + jnp.sum(pred(keys_ref[j], j), axis=0, keepdims=True)
        return lax.fori_loop(0, nvis, body, jnp.zeros((1, tq), I32))

    def bit_body(t, thr):
        cand = thr ^ lax.shift_left(jnp.int32(1), 31 - t)
        cnt = count(lambda k, j: jnp.where(k >= cand, 1, 0))
        return jnp.where(cnt >= topk, cand, thr)

    thr = lax.fori_loop(0, 32, bit_body, jnp.full((1, tq), INT_MIN, I32))

    need = topk - count(lambda k, j: jnp.where(k > thr, 1, 0))
    n_eq = count(lambda k, j: jnp.where(k == thr, 1, 0))
    cst_ref[...] = jnp.full((1, tq), nk * tk, I32)

    @pl.when(jnp.max(jnp.where(n_eq > need, 1, 0)) > 0)
    def _():
        def idx_body(t, c):
            cand = c | lax.shift_left(jnp.int32(1), 15 - t)
            f = count(lambda k, j: jnp.where(k == thr, jnp.where(kpos_of(j) < cand, 1, 0), 0))
            return jnp.where(f < need, cand, c)
        cst_ref[...] = lax.fori_loop(0, 16, idx_body, jnp.zeros((1, tq), I32))

    cstar = cst_ref[...]

    def write_body(j, carry):
        k = keys_ref[j]
        kpos = kpos_of(j)
        sel = jnp.where(k > thr, 0.0, jnp.where(k == thr, jnp.where(kpos <= cstar, 0.0, NEG), NEG))
        o_ref[0, j] = jnp.where(kpos <= qpos, sel, NEG).astype(o_ref.dtype)
        return carry

    lax.fori_loop(0, nvis, write_body, 0)

    def fill_body(j, carry):
        o_ref[0, j] = jnp.full((tk, tq), NEG, o_ref.dtype)
        return carry

    lax.fori_loop(nvis, nk, fill_body, 0)


def _indexer(qh, kvb, wi_t, batch, seq, tq, tk, qi_blk, ka_blk):
    nq, nk = seq // tq, seq // tk
    topk = min(TOPK_MAX, seq // 4)
    return pl.pallas_call(
        functools.partial(_indexer_kernel, tq=tq, tk=tk, nk=nk, topk=topk),
        grid=(batch, nq),
        in_specs=[pl.BlockSpec((IDX_HEADS // 2, tq, LANES), lambda b, i: (qi_blk, b * nq + i, 0)),
                  pl.BlockSpec((seq, LANES), lambda b, i: (b, ka_blk)),
                  pl.BlockSpec((seq, LANES), lambda b, i: (b, ka_blk + 1)),
                  pl.BlockSpec((IDX_HEADS, tq), lambda b, i: (0, b * nq + i))],
        out_specs=pl.BlockSpec((1, nk, tk, tq), lambda b, i: (b * nq + i, 0, 0, 0)),
        out_shape=jax.ShapeDtypeStruct((batch * nq, nk, tk, tq), BF16),
        scratch_shapes=[pltpu.VMEM((nk, tk, tq), I32), pltpu.VMEM((1, tq), I32)],
        compiler_params=_cparams(("arbitrary", "arbitrary")),
        name="indexer_topk_mask",
    )(qh, kvb, kvb, wi_t)


def _attn_kernel(q_ref, k_ref, vt_ref, m_ref, bt_ref, o_ref, acc_ref, mx_ref, l_ref, *, tq, tk):
    i = pl.program_id(1)
    exp2_scale = HEAD_DIM ** -0.5 * math.log2(math.e)
    mx_ref[...] = jnp.full(mx_ref.shape, NEG, F32)
    l_ref[...] = jnp.zeros(l_ref.shape, F32)
    acc_ref[...] = jnp.zeros(acc_ref.shape, F32)
    zero_blk = jnp.zeros((LANES, LANES), F32)

    def bias_tile(g, near):
        cols = []
        for r in range(KV_GROUP):
            h = g * KV_GROUP + r
            b0 = bt_ref[0, h]
            b1 = bt_ref[1, h]
            if near == 0:
                top = jnp.concatenate([b0, b1], axis=1)
                bot = jnp.concatenate([zero_blk, b0], axis=1)
            else:
                top = jnp.concatenate([zero_blk, zero_blk], axis=1)
                bot = jnp.concatenate([b1, zero_blk], axis=1)
            cols.append(jnp.concatenate([top, bot], axis=0))
        return jnp.concatenate(cols, axis=1)

    def update(j, near):
        k0 = pl.multiple_of(j * tk, tk)
        mt = m_ref[0, j].astype(F32)
        mt4 = jnp.concatenate([mt] * KV_GROUP, axis=1)
        for g in range(N_KV_HEADS):
            kt = k_ref[pl.ds(k0, tk), g * HEAD_DIM:(g + 1) * HEAD_DIM]
            qs = q_ref[g * KV_GROUP:(g + 1) * KV_GROUP].reshape(KV_GROUP * tq, HEAD_DIM)
            s = _dot_nt(kt, qs) + mt4
            if near is not None:
                s = s + bias_tile(g, near)
            m_old = mx_ref[g]
            m_new = jnp.maximum(m_old, jnp.max(s, axis=0, keepdims=True))
            alpha = jnp.exp2((m_old - m_new) * exp2_scale)
            p = jnp.exp2((s - m_new) * exp2_scale)
            l_ref[g] = alpha * l_ref[g] + jnp.sum(p, axis=0, keepdims=True)
            vt = vt_ref[0, j, g * HEAD_DIM:(g + 1) * HEAD_DIM, :]
            pv = jnp.dot(vt, p.astype(BF16), preferred_element_type=F32)
            acc_ref[g] = alpha * acc_ref[g] + pv
            mx_ref[g] = m_new

    def far_body(j, carry):
        update(j, None)
        return carry

    lax.fori_loop(0, jnp.maximum(i - 1, 0), far_body, 0)

    @pl.when(i >= 1)
    def _():
        update(i - 1, 1)

    update(i, 0)

    for g in range(N_KV_HEADS):
        o = acc_ref[g] / l_ref[g]
        for r in range(KV_GROUP):
            h = g * KV_GROUP + r
            o_ref[:, h * HEAD_DIM:(h + 1) * HEAD_DIM] = o[:, r * tq:(r + 1) * tq].T.astype(o_ref.dtype)


def _attention(qh, kvb, vt_tiles, mask, bias_t, batch, seq, tq, tk):
    nq, nk = seq // tq, seq // tk
    return pl.pallas_call(
        functools.partial(_attn_kernel, tq=tq, tk=tk),
        grid=(batch, nq),
        in_specs=[pl.BlockSpec((N_HEADS, tq, HEAD_DIM), lambda b, i: (0, b * nq + i, 0)),
                  pl.BlockSpec((seq, KV_WIDTH), lambda b, i: (b, 0)),
                  pl.BlockSpec((1, nk, KV_WIDTH, tk), lambda b, i: (b, 0, 0, 0)),
                  pl.BlockSpec((1, nk, tk, tq), lambda b, i: (b * nq + i, 0, 0, 0)),
                  pl.BlockSpec((2, N_HEADS, LANES, LANES), lambda b, i: (0, 0, 0, 0))],
        out_specs=pl.BlockSpec((tq, ATT_WIDTH), lambda b, i: (b * nq + i, 0)),
        out_shape=jax.ShapeDtypeStruct((batch * seq, ATT_WIDTH), BF16),
        scratch_shapes=[pltpu.VMEM((N_KV_HEADS, HEAD_DIM, KV_GROUP * tq), F32),
                        pltpu.VMEM((N_KV_HEADS, 1, KV_GROUP * tq), F32),
                        pltpu.VMEM((N_KV_HEADS, 1, KV_GROUP * tq), F32)],
        compiler_params=_cparams(("arbitrary", "arbitrary")),
        name="masked_attention",
    )(qh, kvb, vt_tiles, mask, bias_t)


def _conv_kernel(halo_ref, cur_ref, w_ref, b_ref, g_ref, bb_ref, o_ref, ext_ref, ph_ref, *, tt, tc, rc, zero_first):
    t = pl.program_id(1)
    halo = halo_ref[0]
    if zero_first:
        halo = jnp.where(t == 0, 0.0, halo)
    ext_ref[0:CONV_HALO, :] = halo
    ext_ref[CONV_HALO:CONV_HALO + tt, :] = cur_ref[0]
    first = CONV_HALO - (CONV_W - 1)
    ph_rows = ph_ref.shape[1]
    for p in range(1, SUBLANES):
        ph_ref[p - 1] = ext_ref[p:p + ph_rows, :]

    def tap_rows(start, cs):
        p, base = start % SUBLANES, start - start % SUBLANES
        if p == 0:
            return ext_ref[base:base + rc, cs]
        return ph_ref[p - 1, base:base + rc, cs]

    for c in range(tc // LANES):
        cs = slice(c * LANES, (c + 1) * LANES)
        for r in range(tt // rc):
            acc = jnp.zeros((rc, LANES), F32) + b_ref[:, cs]
            for j in range(CONV_W):
                acc = acc + w_ref[j:j + 1, cs] * tap_rows(r * rc + first + j, cs)
            mu = jnp.mean(acc, axis=-1, keepdims=True)
            dv = acc - mu
            var = jnp.mean(dv * dv, axis=-1, keepdims=True)
            yn = dv * lax.rsqrt(var + EPS) * g_ref[:, cs] + bb_ref[:, cs]
            o_ref[0, r * rc:(r + 1) * rc, cs] = (yn * jax.nn.sigmoid(yn)).astype(o_ref.dtype)


def _conv(halo_src, cur, conv_w, conv_b, cn_g, cn_b, tt, tc, zero_first):
    b, t, c = cur.shape
    hb = tt // CONV_HALO
    if zero_first:
        halo_spec = pl.BlockSpec((1, CONV_HALO, tc), lambda a, i, j: (a, jnp.maximum(i * hb - 1, 0), j))
    else:
        halo_spec = pl.BlockSpec((1, CONV_HALO, tc), lambda a, i, j: (a, 0, j))
    vec = lambda: pl.BlockSpec((1, tc), lambda a, i, j: (0, j))
    return pl.pallas_call(
        functools.partial(_conv_kernel, tt=tt, tc=tc, rc=min(tt, 64), zero_first=zero_first),
        grid=(b, t // tt, c // tc),
        in_specs=[halo_spec,
                  pl.BlockSpec((1, tt, tc), lambda a, i, j: (a, i, j)),
                  pl.BlockSpec((CONV_W, tc), lambda a, i, j: (0, j)),
                  vec(), vec(), vec()],
        out_specs=pl.BlockSpec((1, tt, tc), lambda a, i, j: (a, i, j)),
        out_shape=jax.ShapeDtypeStruct((b, t, c), BF16),
        scratch_shapes=[pltpu.VMEM((CONV_HALO + tt, tc), F32),
                        pltpu.VMEM((SUBLANES - 1, CONV_HALO + tt - SUBLANES, tc), F32)],
        compiler_params=_cparams(("arbitrary", "arbitrary", "arbitrary")),
        name="conformer_conv",
    )(halo_src, cur, conv_w, conv_b.reshape(1, c), cn_g.reshape(1, c), cn_b.reshape(1, c))


def _outproj_kernel(a_ref, c_ref, wa_ref, wc_ref, x_ref, gt_ref, o_ref):
    acc = jnp.dot(a_ref[0], wa_ref[...], preferred_element_type=F32)
    acc = acc + jnp.dot(c_ref[0], wc_ref[...], preferred_element_type=F32)
    o_ref[0] = x_ref[0] + gt_ref[0] * acc


def _outproj(att3, conv3, wo_a, wo_c, x3, gt3, tm, tn):
    g, r, d = x3.shape
    ka, kc = att3.shape[-1], conv3.shape[-1]
    gt_spec = (pl.BlockSpec((1, 1, tn), lambda a, i, j: (a, 0, j)) if gt3.shape[1] == 1
               else pl.BlockSpec((1, tm, tn), lambda a, i, j: (a, i, j)))
    return pl.pallas_call(
        _outproj_kernel,
        grid=(g, r // tm, d // tn),
        in_specs=[pl.BlockSpec((1, tm, ka), lambda a, i, j: (a, i, 0)),
                  pl.BlockSpec((1, tm, kc), lambda a, i, j: (a, i, 0)),
                  pl.BlockSpec((ka, tn), lambda a, i, j: (0, j)),
                  pl.BlockSpec((kc, tn), lambda a, i, j: (0, j)),
                  pl.BlockSpec((1, tm, tn), lambda a, i, j: (a, i, j)),
                  gt_spec],
        out_specs=pl.BlockSpec((1, tm, tn), lambda a, i, j: (a, i, j)),
        out_shape=jax.ShapeDtypeStruct((g, r, d), F32),
        compiler_params=_cparams(("arbitrary", "arbitrary", "arbitrary")),
        name="outproj_residual",
    )(att3, conv3, wo_a, wo_c, x3, gt3)


def _peer_scores_kernel(ht_ref, wqt_ref, sk_ref, o_ref):
    q_t = jnp.dot(wqt_ref[...], ht_ref[...], preferred_element_type=F32).astype(BF16)
    half = q_t.shape[0] // 2
    o_ref[0:PEER_NKEYS, :] = jnp.dot(sk_ref[0, 0], q_t[:half], preferred_element_type=F32)
    o_ref[PEER_NKEYS:2 * PEER_NKEYS, :] = jnp.dot(sk_ref[0, 1], q_t[half:], preferred_element_type=F32)


def _peer_scores(h2t, wq_t, sk, tm):
    d, n = h2t.shape
    dk = wq_t.shape[0] // PEER_HEADS
    return pl.pallas_call(
        _peer_scores_kernel,
        grid=(n // tm, PEER_HEADS),
        in_specs=[pl.BlockSpec((d, tm), lambda i, h: (0, i)),
                  pl.BlockSpec((dk, d), lambda i, h: (h, 0)),
                  pl.BlockSpec((1, 2, PEER_NKEYS, dk // 2), lambda i, h: (h, 0, 0, 0))],
        out_specs=pl.BlockSpec((2 * PEER_NKEYS, tm), lambda i, h: (h, i)),
        out_shape=jax.ShapeDtypeStruct((PEER_HEADS * 2 * PEER_NKEYS, n), F32),
        compiler_params=_cparams(("arbitrary", "arbitrary")),
        name="peer_subkey_scores",
    )(h2t, wq_t, sk)


def _top16(x):
    rows = x.shape[0]
    rid = lax.broadcasted_iota(I32, x.shape, 0).astype(F32)
    vals = []
    for k in range(PEER_TOPK):
        m = jnp.max(x, axis=0, keepdims=True)
        vals.append(m)
        if k + 1 < PEER_TOPK:
            first = jnp.min(jnp.where(x == m, rid, float(rows)), axis=0, keepdims=True)
            x = jnp.where(rid == first, -jnp.inf, x)
    return vals


def _stack_rows(rows):
    shape = (len(rows), rows[0].shape[1])
    rid = lax.broadcasted_iota(I32, shape, 0)
    out = jnp.zeros(shape, rows[0].dtype)
    for k, row in enumerate(rows):
        out = jnp.where(rid == k, row, out)
    return out


def _peer_route_kernel(s_ref, s1m_ref, s2m_ref, c1_ref, e2_ref, thr_ref):
    s1 = s_ref[0:PEER_NKEYS, :]
    s2 = s_ref[PEER_NKEYS:2 * PEER_NKEYS, :]
    t1 = _top16(s1)
    t2 = _top16(s2)
    t2_all = _stack_rows(t2)
    blocks = [t1[0] + t2_all]
    for a in range(1, 8):
        blocks.append(t1[a] + t2_all[0:8])
    blocks.append(_stack_rows(t1[8:16]) + t2[0])
    top = _top16(jnp.concatenate(blocks, axis=0))
    z = jnp.ones_like(top[0])
    for k in range(1, PEER_TOPK):
        z = z + jnp.exp(top[k] - top[0])
    s1m_ref[...] = jnp.where(s1 >= t1[PEER_TOPK - 1], s1, -jnp.inf)
    s2m_ref[...] = jnp.where(s2 >= t2[PEER_TOPK - 1], s2, -jnp.inf)
    c1_ref[...] = jnp.exp(s1 - t1[0]) / z
    e2_ref[...] = jnp.exp(s2 - t2[0])
    thr_ref[0] = top[PEER_TOPK - 1]


def _peer_route(s_t, tl):
    rows, n = s_t.shape
    big = lambda: pl.BlockSpec((PEER_NKEYS, tl), lambda h, t: (h, t))
    big_shape = jax.ShapeDtypeStruct((PEER_HEADS * PEER_NKEYS, n), F32)
    return pl.pallas_call(
        _peer_route_kernel,
        grid=(PEER_HEADS, n // tl),
        in_specs=[pl.BlockSpec((2 * PEER_NKEYS, tl), lambda h, t: (h, t))],
        out_specs=[big(), big(), big(), big(), pl.BlockSpec((1, 1, tl), lambda h, t: (h, 0, t))],
        out_shape=[big_shape, big_shape, big_shape, big_shape,
                   jax.ShapeDtypeStruct((PEER_HEADS, 1, n), F32)],
        compiler_params=_cparams(("arbitrary", "arbitrary")),
        name="peer_route",
    )(s_t)


GATE_ROWS = 64


MXU_COLS = 256


def _peer_dense_kernel(ht_ref, u_ref, vt_ref, s1_ref, c1_ref, s2_ref, e2_ref, thr_ref, o_ref, *, te):
    e = pl.program_id(1)
    tm = ht_ref.shape[1]
    rows_per_tile = te // PEER_NKEYS

    @pl.when(e == 0)
    def _():
        o_ref[...] = jnp.zeros(o_ref.shape, F32)

    chunk = min(MXU_COLS, tm)
    a_chunks = [jnp.dot(u_ref[...], ht_ref[:, c * chunk:(c + 1) * chunk], preferred_element_type=F32)
                for c in range(tm // chunk)]

    s1rows = [[s1_ref[pl.ds(h * PEER_NKEYS + e * rows_per_tile + r, 1), :] for r in range(rows_per_tile)]
              for h in range(PEER_HEADS)]
    c1rows = [[c1_ref[pl.ds(h * PEER_NKEYS + e * rows_per_tile + r, 1), :] for r in range(rows_per_tile)]
              for h in range(PEER_HEADS)]
    for c in range(tm // chunk):
        w_cols = []
        for lc in range(chunk // LANES):
            ls = slice(c * chunk + lc * LANES, c * chunk + (lc + 1) * LANES)
            als = slice(lc * LANES, (lc + 1) * LANES)
            blocks = [[None] * (PEER_NKEYS // GATE_ROWS) for _ in range(rows_per_tile)]
            for part in range(PEER_NKEYS // GATE_ROWS):
                rs = [slice(r * PEER_NKEYS + part * GATE_ROWS, r * PEER_NKEYS + (part + 1) * GATE_ROWS)
                      for r in range(rows_per_tile)]
                acc = [jnp.zeros((GATE_ROWS, LANES), F32) for _ in range(rows_per_tile)]
                for h in range(PEER_HEADS):
                    row0 = h * PEER_NKEYS + part * GATE_ROWS
                    s2 = s2_ref[row0:row0 + GATE_ROWS, ls]
                    e2 = e2_ref[row0:row0 + GATE_ROWS, ls]
                    thr = thr_ref[h, :, ls]
                    for r in range(rows_per_tile):
                        cand = s2 + s1rows[h][r][:, ls]
                        acc[r] = acc[r] + jnp.where(cand >= thr, e2, 0.0) * c1rows[h][r][:, ls]
                for r in range(rows_per_tile):
                    blocks[r][part] = (acc[r] * jax.nn.gelu(a_chunks[c][rs[r], als])).astype(BF16)
            w_cols.append(jnp.concatenate([b for row in blocks for b in row], axis=0))
        cs = slice(c * chunk, (c + 1) * chunk)
        o_ref[:, cs] += jnp.dot(vt_ref[...], jnp.concatenate(w_cols, axis=1), preferred_element_type=F32)


def _peer_dense(h2t, u_bf, vt_bf, s1m, c1, s2m, e2, thr, tm, te):
    d, n = h2t.shape
    rows = PEER_HEADS * PEER_NKEYS
    once = pl.Buffered(1)
    tok = lambda: pl.BlockSpec((rows, tm), lambda i, e: (0, i), pipeline_mode=once)
    return pl.pallas_call(
        functools.partial(_peer_dense_kernel, te=te),
        grid=(n // tm, u_bf.shape[0] // te),
        in_specs=[pl.BlockSpec((d, tm), lambda i, e: (0, i), pipeline_mode=once),
                  pl.BlockSpec((te, d), lambda i, e: (e, 0)),
                  pl.BlockSpec((d, te), lambda i, e: (0, e)),
                  tok(), tok(), tok(), tok(),
                  pl.BlockSpec((PEER_HEADS, 1, tm), lambda i, e: (0, 0, i), pipeline_mode=once)],
        out_specs=pl.BlockSpec((d, tm), lambda i, e: (0, i)),
        out_shape=jax.ShapeDtypeStruct((d, n), F32),
        compiler_params=_cparams(("arbitrary", "arbitrary")),
        name="peer_dense_experts",
    )(h2t, u_bf, vt_bf, s1m, c1, s2m, e2, thr)


def _final_kernel(x_ref, gt_ref, p_ref, g_ref, o_ref, *, normalize):
    xx = x_ref[0] + gt_ref[0] * p_ref[...].T
    if normalize:
        ms = jnp.mean(xx * xx, axis=-1, keepdims=True)
        xx = xx * lax.rsqrt(ms + EPS) * g_ref[...]
    o_ref[0] = xx


def _final(x3, gt3, peer_t, g_final, tm, normalize):
    g, r, d = x3.shape
    nb = r // tm
    return pl.pallas_call(
        functools.partial(_final_kernel, normalize=normalize),
        grid=(g, nb),
        in_specs=[pl.BlockSpec((1, tm, d), lambda a, i: (a, i, 0)),
                  _row_mod_spec(gt3, tm),
                  pl.BlockSpec((d, tm), lambda a, i: (0, a * nb + i)),
                  pl.BlockSpec((1, d), lambda a, i: (0, 0))],
        out_specs=pl.BlockSpec((1, tm, d), lambda a, i: (a, i, 0)),
        out_shape=jax.ShapeDtypeStruct((g, r, d), F32),
        compiler_params=_cparams(("arbitrary", "arbitrary")),
        name="final_norm",
    )(x3, gt3, peer_t, g_final.reshape(1, d))


def _sample_index_kernel(pt_ref, qi_ref, wi_ref, kself_ref, expand_ref, *rest, n_pages, topk):
    pages = rest[:INDEX_PAGES_PER_STEP]
    mask_ref, self_ref, sc_ref = rest[INDEX_PAGES_PER_STEP:]
    step = pl.program_id(1)
    n_steps = n_pages // INDEX_PAGES_PER_STEP
    score_scale = IDX_DIM ** -0.5 * IDX_HEADS ** -0.5
    qi = qi_ref[0]
    wi = wi_ref[0]
    kcat = jnp.concatenate([pages[k][0] for k in range(INDEX_PAGES_PER_STEP)], axis=1).astype(BF16)
    s = jnp.dot(qi, kcat, preferred_element_type=F32)
    row = jnp.sum(jnp.maximum(s, 0.0) * wi, axis=0, keepdims=True) * score_scale
    for k in range(INDEX_PAGES_PER_STEP):
        sc_ref[pl.ds(step * INDEX_PAGES_PER_STEP + k, 1), :] = row[:, k * PAGE_SIZE:(k + 1) * PAGE_SIZE]

    @pl.when(step == n_steps - 1)
    def _():
        ks = kself_ref[0].astype(BF16).astype(F32)
        s_self = jnp.sum(qi.astype(F32) * ks, axis=1, keepdims=True)
        self_score = jnp.sum(jnp.maximum(s_self, 0.0) * wi, axis=0, keepdims=True) * score_scale
        keys = _sortable(sc_ref[...])
        kself = _sortable(self_score)
        pos = (lax.broadcasted_iota(I32, keys.shape, 0) * PAGE_SIZE
               + lax.broadcasted_iota(I32, keys.shape, 1))
        self_pos = n_pages * PAGE_SIZE

        def total(x, xs):
            return jnp.sum(jnp.sum(x, axis=1, keepdims=True), axis=0, keepdims=True) + xs

        def bit_body(t, thr):
            cand = thr ^ lax.shift_left(jnp.int32(1), 31 - t)
            cnt = total(jnp.where(keys >= cand, 1, 0), jnp.where(kself >= cand, 1, 0))
            return jnp.where(cnt >= topk, cand, thr)

        thr = lax.fori_loop(0, 32, bit_body, jnp.full((1, 1), INT_MIN, I32))
        need = topk - total(jnp.where(keys > thr, 1, 0), jnp.where(kself > thr, 1, 0))

        def idx_body(t, c):
            cand = c | lax.shift_left(jnp.int32(1), 15 - t)
            f = total(jnp.where(keys == thr, jnp.where(pos < cand, 1, 0), 0),
                      jnp.where(kself == thr, jnp.where(self_pos < cand, 1, 0), 0))
            return jnp.where(f < need, cand, c)

        cstar = lax.fori_loop(0, 16, idx_body, jnp.zeros((1, 1), I32))
        picked = jnp.where(keys > thr, 1.0, jnp.where(keys == thr, jnp.where(pos <= cstar, 1.0, 0.0), 0.0))
        rows = jnp.dot(picked.astype(BF16), expand_ref[...], preferred_element_type=F32)
        mask_ref[0] = jnp.where(rows > 0.5, 0.0, NEG)
        ssel = jnp.where(kself > thr, 0.0, jnp.where(kself == thr, jnp.where(self_pos <= cstar, 0.0, NEG), NEG))
        self_ref[0] = jnp.zeros((SUBLANES, LANES), F32) + ssel


def _sample_index(page_table, qi3, wi3, kself3, cki_t, page_base):
    db, n_pages = page_table.shape
    topk = min(TOPK_MAX, (n_pages * PAGE_SIZE + 1) // 4)
    n_steps = n_pages // INDEX_PAGES_PER_STEP
    page_rows = PAGE_SIZE * N_KV_HEADS
    expand = (jnp.arange(page_rows, dtype=I32)[None, :] // N_KV_HEADS
              == jnp.arange(PAGE_SIZE, dtype=I32)[:, None]).astype(BF16)

    def page_spec(k):
        return pl.BlockSpec((1, IDX_DIM, PAGE_SIZE),
                            lambda b, s, pt: (page_base + pt[b, s * INDEX_PAGES_PER_STEP + k], 0, 0))

    grid_spec = pltpu.PrefetchScalarGridSpec(
        num_scalar_prefetch=1,
        grid=(db, n_steps),
        in_specs=[pl.BlockSpec((1, IDX_HEADS, IDX_DIM), lambda b, s, pt: (b, 0, 0)),
                  pl.BlockSpec((1, IDX_HEADS, 1), lambda b, s, pt: (b, 0, 0)),
                  pl.BlockSpec((1, 1, IDX_DIM), lambda b, s, pt: (b, 0, 0)),
                  pl.BlockSpec((PAGE_SIZE, page_rows), lambda b, s, pt: (0, 0))]
                 + [page_spec(k) for k in range(INDEX_PAGES_PER_STEP)],
        out_specs=[pl.BlockSpec((1, n_pages, page_rows), lambda b, s, pt: (b, 0, 0)),
                   pl.BlockSpec((1, SUBLANES, LANES), lambda b, s, pt: (b, 0, 0))],
        scratch_shapes=[pltpu.VMEM((n_pages, PAGE_SIZE), F32)],
    )
    return pl.pallas_call(
        functools.partial(_sample_index_kernel, n_pages=n_pages, topk=topk),
        grid_spec=grid_spec,
        out_shape=[jax.ShapeDtypeStruct((db, n_pages, page_rows), F32),
                   jax.ShapeDtypeStruct((db, SUBLANES, LANES), F32)],
        compiler_params=_cparams(("arbitrary", "arbitrary")),
        name="sample_indexer",
    )(page_table, qi3, wi3, kself3, expand, *([cki_t] * INDEX_PAGES_PER_STEP))


def _sample_attn_kernel(pt_ref, q_ref, kself_ref, vself_ref, mask_ref, self_ref, bias_ref, bself_ref,
                        *rest, n_pages):
    pps = ATTN_PAGES_PER_STEP
    kpages = rest[:pps]
    vpages = rest[pps:2 * pps]
    o_ref, acc_ref, mx_ref, l_ref = rest[2 * pps:]
    step = pl.program_id(1)
    n_steps = n_pages // pps
    page_rows = PAGE_SIZE * N_KV_HEADS
    sm_scale = HEAD_DIM ** -0.5
    q = q_ref[0]
    head_group = lax.broadcasted_iota(I32, (N_HEADS, HEAD_DIM), 0) // KV_GROUP

    def own_group_rows(x_ref):
        out = jnp.zeros((N_HEADS, HEAD_DIM), F32)
        for g in range(N_KV_HEADS):
            out = jnp.where(head_group == g, x_ref[0, g:g + 1, :].astype(BF16).astype(F32), out)
        return out

    @pl.when(step == 0)
    def _():
        logit = jnp.sum(q.astype(F32) * own_group_rows(kself_ref), axis=1, keepdims=True)
        mx_ref[...] = logit * sm_scale + bself_ref[...] + self_ref[0, 0:1, 0:1]
        l_ref[...] = jnp.ones(l_ref.shape, F32)
        acc_ref[...] = own_group_rows(vself_ref)

    kcat = jnp.concatenate([kpages[k][0].astype(BF16) for k in range(pps)], axis=0)
    vcat = jnp.concatenate([vpages[k][0].astype(BF16) for k in range(pps)], axis=0)
    s = _dot_nt(q, kcat) * sm_scale + bias_ref[step]
    s = jnp.concatenate([s[:, k * page_rows:(k + 1) * page_rows] + mask_ref[0, pl.ds(step * pps + k, 1), :]
                         for k in range(pps)], axis=1)
    m_old = mx_ref[...]
    m_new = jnp.maximum(m_old, jnp.max(s, axis=1, keepdims=True))
    alpha = jnp.exp(m_old - m_new)
    p = jnp.exp(s - m_new)
    l_ref[...] = alpha * l_ref[...] + jnp.sum(p, axis=1, keepdims=True)
    acc_ref[...] = alpha * acc_ref[...] + jnp.dot(p.astype(BF16), vcat, preferred_element_type=F32)
    mx_ref[...] = m_new

    @pl.when(step == n_steps - 1)
    def _():
        o_ref[0] = (acc_ref[...] / l_ref[...]).astype(o_ref.dtype)


def _sample_attn(page_table, q3, kself3, vself3, mask, selfsel, bias_steps, bias_self, ck, cv, page_base):
    db, n_pages = page_table.shape
    pps = ATTN_PAGES_PER_STEP
    n_steps = n_pages // pps
    page_rows = PAGE_SIZE * N_KV_HEADS
    step_rows = pps * page_rows

    def page_spec(k):
        return pl.BlockSpec((1, page_rows, HEAD_DIM),
                            lambda b, s, pt: (page_base + pt[b, s * pps + k], 0, 0))

    per_b = lambda shape: pl.BlockSpec((1,) + shape, lambda b, s, pt: (b, 0, 0))
    grid_spec = pltpu.PrefetchScalarGridSpec(
        num_scalar_prefetch=1,
        grid=(db, n_steps),
        in_specs=[per_b((N_HEADS, HEAD_DIM)), per_b((N_KV_HEADS, HEAD_DIM)), per_b((N_KV_HEADS, HEAD_DIM)),
                  per_b((n_pages, page_rows)), per_b((SUBLANES, LANES)),
                  pl.BlockSpec((n_steps, N_HEADS, step_rows), lambda b, s, pt: (0, 0, 0)),
                  pl.BlockSpec((N_HEADS, 1), lambda b, s, pt: (0, 0))]
                 + [page_spec(k) for k in range(pps)] * 2,
        out_specs=per_b((N_HEADS, HEAD_DIM)),
        scratch_shapes=[pltpu.VMEM((N_HEADS, HEAD_DIM), F32),
                        pltpu.VMEM((N_HEADS, 1), F32),
                        pltpu.VMEM((N_HEADS, 1), F32)],
    )
    return pl.pallas_call(
        functools.partial(_sample_attn_kernel, n_pages=n_pages),
        grid_spec=grid_spec,
        out_shape=jax.ShapeDtypeStruct((db, N_HEADS, HEAD_DIM), BF16),
        compiler_params=_cparams(("arbitrary", "arbitrary")),
        name="sample_attention",
    )(page_table, q3, kself3, vself3, mask, selfsel, bias_steps, bias_self,
      *([ck] * pps), *([cv] * pps))


def _rel_bucket(dist):
    n = jnp.maximum(dist, 0)
    max_exact = REL_BUCKETS // 2
    nf = jnp.maximum(n, 1).astype(F32)
    large = max_exact + (jnp.log(nf / max_exact) / math.log(REL_MAX_DIST / max_exact)
                         * (REL_BUCKETS - max_exact)).astype(I32)
    large = jnp.minimum(large, REL_BUCKETS - 1)
    return jnp.where(n < max_exact, n, large)


def _bias_of_dist(rel_bias, dist):
    onehot = (_rel_bucket(dist)[..., None] == jnp.arange(REL_BUCKETS, dtype=I32)).astype(F32)
    return jnp.einsum("...b,bh->...h", onehot, rel_bias.astype(F32), precision=lax.Precision.HIGHEST)


def _prompt_bias_tiles(rel_bias):
    kc = jnp.arange(LANES, dtype=I32)[:, None]
    qr = jnp.arange(LANES, dtype=I32)[None, :]
    far = rel_bias[REL_BUCKETS - 1]
    tiles = []
    for off in (0, LANES):
        dist = off + qr - kc
        t = (_bias_of_dist(rel_bias, dist) - far) * HEAD_DIM ** 0.5
        tiles.append(jnp.where((dist >= 0)[..., None], t, 0.0).transpose(2, 0, 1))
    return jnp.stack(tiles).astype(F32)


def _sample_bias_steps(rel_bias, n_pages):
    past = n_pages * PAGE_SIZE
    bias = _bias_of_dist(rel_bias, past - jnp.arange(past, dtype=I32))
    own = (jnp.arange(N_HEADS, dtype=I32)[:, None] // KV_GROUP) == jnp.arange(N_KV_HEADS, dtype=I32)[None, :]
    rows = jnp.where(own[None], bias[:, :, None], NEG)
    rows = rows.transpose(1, 0, 2).reshape(N_HEADS, n_pages // ATTN_PAGES_PER_STEP, -1)
    return rows.transpose(1, 0, 2).astype(F32)


def _split_in_proj(w_in_l):
    sizes = (ATT_WIDTH, KV_WIDTH, KV_WIDTH, IDX_HEADS * IDX_DIM, IDX_DIM, IDX_HEADS)
    offs = [0]
    for s in sizes:
        offs.append(offs[-1] + s)
    conv_ch = (w_in_l.shape[1] - offs[-1]) // 2
    d = w_in_l.shape[0]
    wq, wk, wv, wqi, wki, wwi = (w_in_l[:, offs[i]:offs[i + 1]] for i in range(6))
    wua = w_in_l[:, offs[-1]:offs[-1] + conv_ch]
    wub = w_in_l[:, offs[-1] + conv_ch:]
    z = lambda n: jnp.zeros((d, n), w_in_l.dtype)
    w_a = jnp.concatenate([wk, wv, wki, z(LANES - IDX_DIM), z(LANES - IDX_DIM), wki,
                           wwi, z(LANES - IDX_HEADS), z(LANES)], axis=1)
    w_b = jnp.concatenate([wq, wqi], axis=1)
    chunk = 256
    parts = []
    for c in range(conv_ch // chunk):
        parts += [wua[:, c * chunk:(c + 1) * chunk], wub[:, c * chunk:(c + 1) * chunk]]
    w_c = jnp.concatenate(parts, axis=1)
    return w_a.astype(BF16), w_b.astype(BF16), w_c.astype(BF16)


COL_K, COL_V, COL_KA, COL_WI = 0, KV_WIDTH, 2 * KV_WIDTH, 2 * KV_WIDTH + 2 * LANES


def _mixer_projections(h, w_a, w_b, w_c, tm):
    z_a = _matmul(h, w_a, F32, tm, 512)
    qh = _matmul(h, w_b, BF16, tm, 512, head_major=True)
    u = _matmul(h, w_c, F32, tm, 512, glu=True)
    return z_a, qh, u


def _peer_block(h2t, wq_t, sk_bf, u_bf, vt_bf, tm, te):
    s_t = _peer_scores(h2t, wq_t, sk_bf, tm)
    s1m, s2m, c1, e2, thr = _peer_route(s_t, min(tm, 256))
    return _peer_dense(h2t, u_bf, vt_bf, s1m, c1, s2m, e2, thr, tm, te)


def kernel(x_prompt, x_sample, cache_k, cache_v, cache_kidx, state_conv, page_table, c_prompt, c_sample,
           rel_bias, w_ada, b_ada, g_mix, w_in, conv_w, conv_b, cn_g, cn_b, w_o, g_ch, peer_wq,
           peer_subkeys, peer_u, peer_v, g_final):
    batch, seq, d = x_prompt.shape
    db = x_sample.shape[0]
    depth = w_ada.shape[0]
    n_pages = page_table.shape[1]
    past = n_pages * PAGE_SIZE
    conv_ch = conv_w.shape[-1]
    n_prompt = batch * seq
    tq = tk = 256

    xp = x_prompt
    xs = jnp.pad(x_sample.reshape(1, db, d), ((0, 0), (0, SAMPLE_ROWS - db), (0, 0)))
    c_rows = batch + db
    c_pad = (-c_rows) % 16
    c_all = jnp.pad(jnp.concatenate([c_prompt, c_sample], axis=0), ((0, c_pad), (0, 0)))
    bias_t = _prompt_bias_tiles(rel_bias)
    bias_steps = _sample_bias_steps(rel_bias, n_pages)
    n_pool = cache_k.shape[1]
    page_rows = PAGE_SIZE * N_KV_HEADS
    ck_rows = cache_k.reshape(depth * n_pool, page_rows, HEAD_DIM)
    cv_rows = cache_v.reshape(depth * n_pool, page_rows, HEAD_DIM)
    cki_t = jnp.swapaxes(cache_kidx, 2, 3).reshape(depth * n_pool, IDX_DIM, PAGE_SIZE)
    bias_self = rel_bias[_rel_bucket(jnp.zeros((), I32))].reshape(N_HEADS, 1)

    outs = {k: [] for k in ("kp", "vp", "kip", "cp", "ks", "vs", "kis", "cs")}
    for l in range(depth):
        mods = _adaln(c_all, w_ada[l], b_ada[l])
        p_mod = [m.reshape(batch, 1, d) for m in jnp.split(mods[:batch], 6, axis=-1)]
        s_mod = [jnp.pad(m.reshape(1, db, d), ((0, 0), (0, SAMPLE_ROWS - db), (0, 0)))
                 for m in jnp.split(mods[batch:c_rows], 6, axis=-1)]
        w_a, w_b, w_c = _split_in_proj(w_in[l])
        wo_a = w_o[l][:ATT_WIDTH].astype(BF16)
        wo_c = w_o[l][ATT_WIDTH:].astype(BF16)

        hp = _modulate(xp, g_mix[l], p_mod[1], p_mod[0], 512).reshape(n_prompt, d)
        z_a, qh, u = _mixer_projections(hp, w_a, w_b, w_c, 1024)
        kvb = z_a[:, :COL_WI].astype(BF16)
        wi_t = z_a[:, COL_WI:COL_WI + IDX_HEADS].T
        vt_tiles = (kvb[:, COL_V:COL_V + KV_WIDTH].reshape(batch, seq // tk, tk, KV_WIDTH)
                    .transpose(0, 1, 3, 2))
        mask = _indexer(qh, kvb, wi_t, batch, seq, tq, tk, qi_blk=N_HEADS // (IDX_HEADS // 2),
                        ka_blk=COL_KA // LANES)
        att = _attention(qh, kvb, vt_tiles, mask, bias_t, batch, seq, tq, tk)
        u3 = u.reshape(batch, seq, conv_ch)
        conv = _conv(u3, u3, conv_w[l], conv_b[l], cn_g[l], cn_b[l], 256, 512, zero_first=True)
        xp = _outproj(att.reshape(batch, seq, ATT_WIDTH), conv, wo_a, wo_c, xp, p_mod[2], 1024, 512)
        outs["kp"].append(z_a[:, COL_K:COL_K + KV_WIDTH].reshape(batch, seq, N_KV_HEADS, HEAD_DIM))
        outs["vp"].append(z_a[:, COL_V:COL_V + KV_WIDTH].reshape(batch, seq, N_KV_HEADS, HEAD_DIM))
        outs["kip"].append(z_a[:, COL_KA:COL_KA + IDX_DIM].reshape(batch, seq, IDX_DIM))
        outs["cp"].append(u3[:, seq - (CONV_W - 1):])

        hs = _modulate(xs, g_mix[l], s_mod[1], s_mod[0], SAMPLE_ROWS).reshape(SAMPLE_ROWS, d)
        zs_a, qhs, us = _mixer_projections(hs, w_a, w_b, w_c, SAMPLE_ROWS)
        k_new = zs_a[:db, COL_K:COL_K + KV_WIDTH]
        v_new = zs_a[:db, COL_V:COL_V + KV_WIDTH]
        ki_new = zs_a[:db, COL_KA:COL_KA + IDX_DIM]
        wi_new = zs_a[:db, COL_WI:COL_WI + IDX_HEADS]
        q_s = qhs[:N_HEADS, :db].transpose(1, 0, 2)
        qi_s = (qhs[N_HEADS:, :db].transpose(1, 0, 2)
                .reshape(db, IDX_HEADS // 2, 2, IDX_DIM).reshape(db, IDX_HEADS, IDX_DIM))
        smask, sself = _sample_index(page_table, qi_s, wi_new.reshape(db, IDX_HEADS, 1),
                                     ki_new.reshape(db, 1, IDX_DIM), cki_t, l * n_pool)
        att_s = _sample_attn(page_table, q_s, k_new.reshape(db, N_KV_HEADS, HEAD_DIM),
                             v_new.reshape(db, N_KV_HEADS, HEAD_DIM), smask, sself, bias_steps, bias_self,
                             ck_rows, cv_rows, l * n_pool)
        att_s = jnp.pad(att_s.reshape(1, db, ATT_WIDTH), ((0, 0), (0, SAMPLE_ROWS - db), (0, 0)))
        u_new = us[:db]
        state = state_conv[l].astype(F32)
        halo = jnp.pad(state, ((0, 0), (CONV_HALO - (CONV_W - 1), 0), (0, 0)))
        cur = jnp.pad(u_new.reshape(db, 1, conv_ch), ((0, 0), (0, SUBLANES - 1), (0, 0)))
        conv_s = _conv(halo, cur, conv_w[l], conv_b[l], cn_g[l], cn_b[l], SUBLANES, 512, zero_first=False)
        conv_s = jnp.pad(conv_s[:, 0].reshape(1, db, conv_ch), ((0, 0), (0, SAMPLE_ROWS - db), (0, 0)))
        xs = _outproj(att_s, conv_s, wo_a, wo_c, xs, s_mod[2], SAMPLE_ROWS, 512)
        outs["ks"].append(k_new.reshape(db, 1, N_KV_HEADS, HEAD_DIM))
        outs["vs"].append(v_new.reshape(db, 1, N_KV_HEADS, HEAD_DIM))
        outs["kis"].append(ki_new.reshape(db, 1, IDX_DIM))
        outs["cs"].append(jnp.concatenate([state[:, 1:], u_new.reshape(db, 1, conv_ch)], axis=1))

        wq_t = peer_wq[l].astype(BF16).T
        sk_bf = peer_subkeys[l].astype(BF16)
        u_bf = peer_u[l].astype(BF16)
        vt_bf = peer_v[l].astype(BF16).T
        hp2 = _modulate(xp, g_ch[l], p_mod[4], p_mod[3], 512, transposed=True)
        peer_p = _peer_block(hp2, wq_t, sk_bf, u_bf, vt_bf, 512, 512)
        hs2 = _modulate(xs, g_ch[l], s_mod[4], s_mod[3], SAMPLE_ROWS, transposed=True)
        peer_s = _peer_block(hs2, wq_t, sk_bf, u_bf, vt_bf, SAMPLE_ROWS, 512)
        last = l == depth - 1
        xp = _final(xp, p_mod[5], peer_p, g_final, 256, normalize=last)
        xs = _final(xs, s_mod[5], peer_s, g_final, SAMPLE_ROWS, normalize=last)

    st = lambda k: jnp.stack(outs[k])
    y_sample = xs[0, :db].reshape(db, 1, d)
    return (xp, y_sample, st("kp"), st("vp"), st("kip"), st("cp"),
            st("ks"), st("vs"), st("kis"), st("cs"))
```

```python
import functools
import math

import jax
import jax.numpy as jnp
from jax import lax
from jax.experimental import pallas as pl
from jax.experimental.pallas import tpu as pltpu

F32 = jnp.float32
BF16 = jnp.bfloat16
I32 = jnp.int32

HEAD_DIM = 128
N_HEADS = 16
N_KV_HEADS = 4
KV_GROUP = N_HEADS // N_KV_HEADS
ATT_WIDTH = N_HEADS * HEAD_DIM
KV_WIDTH = N_KV_HEADS * HEAD_DIM
IDX_HEADS = 16
IDX_DIM = 64
TOPK_MAX = 256
REL_BUCKETS = 32
REL_MAX_DIST = 128
CONV_W = 31
PEER_HEADS = 8
PEER_NKEYS = 128
PEER_TOPK = 16
EPS = 1e-6
PAGE_SIZE = 128

LANES = 128
SUBLANES = 8
VMEM_LIMIT = 56 * 1024 * 1024
VMEM_LIMIT_PEER = 63 * 1024 * 1024

NEG = -1e30
INT_MIN = -(2 ** 31)
CONV_HALO = 32
SAMPLE_ROWS = 128
ATTN_PAGES_PER_STEP = 16


def _cparams(sem, vmem_limit=VMEM_LIMIT):
    return pltpu.CompilerParams(dimension_semantics=sem, vmem_limit_bytes=vmem_limit)


def _dot_nt(a, b):
    return lax.dot_general(a, b, (((1,), (1,)), ((), ())), preferred_element_type=F32)


def _sortable(x):
    bits = pltpu.bitcast(x, I32)
    return bits ^ ((bits >> 31) & jnp.int32(0x7FFFFFFF))


def _adaln_kernel(c_ref, w_ref, b_ref, o_ref):
    c = c_ref[...]
    a = (c * jax.nn.sigmoid(c)).astype(BF16)
    o_ref[...] = jnp.dot(a, w_ref[...].astype(BF16), preferred_element_type=F32) + b_ref[...]


def _adaln(c, w_ada, b_ada, tn=512):
    r, d = c.shape
    n = w_ada.shape[1]
    return pl.pallas_call(
        _adaln_kernel,
        grid=(n // tn,),
        in_specs=[pl.BlockSpec((r, d), lambda j: (0, 0)),
                  pl.BlockSpec((d, tn), lambda j: (0, j)),
                  pl.BlockSpec((1, tn), lambda j: (0, j))],
        out_specs=pl.BlockSpec((r, tn), lambda j: (0, j)),
        out_shape=jax.ShapeDtypeStruct((r, n), F32),
        compiler_params=_cparams(("arbitrary",)),
        name="adaln",
    )(c, w_ada, b_ada.reshape(1, n))


def _modulate_kernel(x_ref, g_ref, sc_ref, sh_ref, o_ref, *, transposed):
    x = x_ref[0]
    ms = jnp.mean(x * x, axis=-1, keepdims=True)
    y = x * lax.rsqrt(ms + EPS) * g_ref[...]
    y = y * (1.0 + sc_ref[0]) + sh_ref[0]
    if transposed:
        o_ref[...] = y.T.astype(o_ref.dtype)
    else:
        o_ref[0] = y.astype(o_ref.dtype)


def _row_mod_spec(mod, tr):
    d = mod.shape[-1]
    if mod.shape[1] == 1:
        return pl.BlockSpec((1, 1, d), lambda g, r, *_: (g, 0, 0))
    return pl.BlockSpec((1, tr, d), lambda g, r, *_: (g, r, 0))


def _modulate(x3, gain, sc3, sh3, tr, transposed=False):
    g, r, d = x3.shape
    nb = r // tr
    if transposed:
        out_spec = pl.BlockSpec((d, tr), lambda a, b: (0, a * nb + b))
        out_shape = jax.ShapeDtypeStruct((d, g * r), BF16)
    else:
        out_spec = pl.BlockSpec((1, tr, d), lambda a, b: (a, b, 0))
        out_shape = jax.ShapeDtypeStruct((g, r, d), BF16)
    return pl.pallas_call(
        functools.partial(_modulate_kernel, transposed=transposed),
        grid=(g, nb),
        in_specs=[pl.BlockSpec((1, tr, d), lambda a, b: (a, b, 0)),
                  pl.BlockSpec((1, d), lambda a, b: (0, 0)),
                  _row_mod_spec(sc3, tr),
                  _row_mod_spec(sh3, tr)],
        out_specs=out_spec,
        out_shape=out_shape,
        compiler_params=_cparams(("arbitrary", "arbitrary")),
        name="modulate",
    )(x3, gain.reshape(1, d), sc3, sh3)


def _mm_kernel(h_ref, w_ref, o_ref, *, glu, head_major):
    acc = jnp.dot(h_ref[...], w_ref[...], preferred_element_type=F32)
    if glu:
        half = acc.shape[1] // 2
        acc = acc[:, :half] * jax.nn.sigmoid(acc[:, half:])
    if head_major:
        for c in range(acc.shape[1] // LANES):
            o_ref[c] = acc[:, c * LANES:(c + 1) * LANES].astype(o_ref.dtype)
    else:
        o_ref[...] = acc.astype(o_ref.dtype)


def _matmul(h, w, out_dtype, tm, tn, glu=False, head_major=False):
    m, k = h.shape
    n = w.shape[1]
    n_out = n // 2 if glu else n
    tn_out = tn // 2 if glu else tn
    if head_major:
        out_shape = jax.ShapeDtypeStruct((n_out // LANES, m, LANES), out_dtype)
        out_spec = pl.BlockSpec((tn_out // LANES, tm, LANES), lambda i, j: (j, i, 0))
    else:
        out_shape = jax.ShapeDtypeStruct((m, n_out), out_dtype)
        out_spec = pl.BlockSpec((tm, tn_out), lambda i, j: (i, j))
    return pl.pallas_call(
        functools.partial(_mm_kernel, glu=glu, head_major=head_major),
        grid=(m // tm, n // tn),
        in_specs=[pl.BlockSpec((tm, k), lambda i, j: (i, 0)),
                  pl.BlockSpec((k, tn), lambda i, j: (0, j))],
        out_specs=out_spec,
        out_shape=out_shape,
        compiler_params=_cparams(("arbitrary", "arbitrary")),
        name="proj_matmul",
    )(h, w)


def _indexer_kernel(qi_ref, ka_ref, kb_ref, wi_ref, o_ref, keys_ref, cst_ref, *, tq, tk, nk, topk):
    i = pl.program_id(1)
    q0 = i * tq
    nvis = (q0 + tq + tk - 1) // tk
    qpos = q0 + lax.broadcasted_iota(I32, (1, tq), 1)
    w = wi_ref[...]
    score_scale = IDX_DIM ** -0.5 * IDX_HEADS ** -0.5

    def kpos_of(j):
        return j * tk + lax.broadcasted_iota(I32, (tk, 1), 0)

    def score_body(j, carry):
        k0 = pl.multiple_of(j * tk, tk)
        ka = ka_ref[pl.ds(k0, tk), :]
        kb = kb_ref[pl.ds(k0, tk), :]
        acc = jnp.zeros((tk, tq), F32)
        for p in range(IDX_HEADS // 2):
            qp = qi_ref[p]
            sa = _dot_nt(ka, qp)
            sb = _dot_nt(kb, qp)
            acc = acc + jnp.maximum(sa, 0.0) * w[2 * p:2 * p + 1]
            acc = acc + jnp.maximum(sb, 0.0) * w[2 * p + 1:2 * p + 2]
        acc = acc * score_scale
        acc = jnp.where(kpos_of(j) <= qpos, acc, -jnp.inf)
        keys_ref[j] = _sortable(acc)
        return carry

    lax.fori_loop(0, nvis, score_body, 0)

    def count(pred):
        def body(j, c):
            return c + jnp.sum(pred(keys_ref[j], j), axis=0, keepdims=True)
        return lax.fori_loop(0, nvis, body, jnp.zeros((1, tq), I32))

    def bit_body(t, thr):
        cand = thr ^ lax.shift_left(jnp.int32(1), 31 - t)
        cnt = count(lambda k, j: jnp.where(k >= cand, 1, 0))
        return jnp.where(cnt >= topk, cand, thr)

    thr = lax.fori_loop(0, 32, bit_body, jnp.full((1, tq), INT_MIN, I32))

    need = topk - count(lambda k, j: jnp.where(k > thr, 1, 0))
    n_eq = count(lambda k, j: jnp.where(k == thr, 1, 0))
    cst_ref[...] = jnp.full((1, tq), nk * tk, I32)

    @pl.when(jnp.max(jnp.where(n_eq > need, 1, 0)) > 0)
    def _():
        def idx_body(t, c):
            cand = c | lax.shift_left(jnp.int32(1), 15 - t)
            f = count(lambda k, j: jnp.where(k == thr, jnp.where(kpos_of(j) < cand, 1, 0), 0))
            return jnp.where(f < need, cand, c)
        cst_ref[...] = lax.fori_loop(0, 16, idx_body, jnp.zeros((1, tq), I32))

    cstar = cst_ref[...]

    def write_body(j, carry):
        k = keys_ref[j]
        kpos = kpos_of(j)
        sel = jnp.where(k > thr, 0.0, jnp.where(k == thr, jnp.where(kpos <= cstar, 0.0, NEG), NEG))
        o_ref[0, j] = jnp.where(kpos <= qpos, sel, NEG).astype(o_ref.dtype)
        return carry

    lax.fori_loop(0, nvis, write_body, 0)

    def fill_body(j, carry):
        o_ref[0, j] = jnp.full((tk, tq), NEG, o_ref.dtype)
        return carry

    lax.fori_loop(nvis, nk, fill_body, 0)


def _indexer(qh, kvb, wi_t, batch, seq, tq, tk, qi_blk, ka_blk):
    nq, nk = seq // tq, seq // tk
    topk = min(TOPK_MAX, seq // 4)
    return pl.pallas_call(
        functools.partial(_indexer_kernel, tq=tq, tk=tk, nk=nk, topk=topk),
        grid=(batch, nq),
        in_specs=[pl.BlockSpec((IDX_HEADS // 2, tq, LANES), lambda b, i: (qi_blk, b * nq + i, 0)),
                  pl.BlockSpec((seq, LANES), lambda b, i: (b, ka_blk)),
                  pl.BlockSpec((seq, LANES), lambda b, i: (b, ka_blk + 1)),
                  pl.BlockSpec((IDX_HEADS, tq), lambda b, i: (0, b * nq + i))],
        out_specs=pl.BlockSpec((1, nk, tk, tq), lambda b, i: (b * nq + i, 0, 0, 0)),
        out_shape=jax.ShapeDtypeStruct((batch * nq, nk, tk, tq), BF16),
        scratch_shapes=[pltpu.VMEM((nk, tk, tq), I32), pltpu.VMEM((1, tq), I32)],
        compiler_params=_cparams(("arbitrary", "arbitrary")),
        name="indexer_topk_mask",
    )(qh, kvb, kvb, wi_t)


def _attn_kernel(q_ref, k_ref, vt_ref, m_ref, bt_ref, o_ref, acc_ref, mx_ref, l_ref, *, tq, tk):
    i = pl.program_id(1)
    exp2_scale = HEAD_DIM ** -0.5 * math.log2(math.e)
    mx_ref[...] = jnp.full(mx_ref.shape, NEG, F32)
    l_ref[...] = jnp.zeros(l_ref.shape, F32)
    acc_ref[...] = jnp.zeros(acc_ref.shape, F32)
    zero_blk = jnp.zeros((LANES, LANES), F32)

    def bias_tile(g, near):
        cols = []
        for r in range(KV_GROUP):
            h = g * KV_GROUP + r
            b0 = bt_ref[0, h]
            b1 = bt_ref[1, h]
            if near == 0:
                top = jnp.concatenate([b0, b1], axis=1)
                bot = jnp.concatenate([zero_blk, b0], axis=1)
            else:
                top = jnp.concatenate([zero_blk, zero_blk], axis=1)
                bot = jnp.concatenate([b1, zero_blk], axis=1)
            cols.append(jnp.concatenate([top, bot], axis=0))
        return jnp.concatenate(cols, axis=1)

    def update(j, near):
        k0 = pl.multiple_of(j * tk, tk)
        mt = m_ref[0, j].astype(F32)
        mt4 = jnp.concatenate([mt] * KV_GROUP, axis=1)
        for g in range(N_KV_HEADS):
            kt = k_ref[pl.ds(k0, tk), g * HEAD_DIM:(g + 1) * HEAD_DIM]
            qs = q_ref[g * KV_GROUP:(g + 1) * KV_GROUP].reshape(KV_GROUP * tq, HEAD_DIM)
            s = _dot_nt(kt, qs) + mt4
            if near is not None:
                s = s + bias_tile(g, near)
            m_old = mx_ref[g]
            m_new = jnp.maximum(m_old, jnp.max(s, axis=0, keepdims=True))
            alpha = jnp.exp2((m_old - m_new) * exp2_scale)
            p = jnp.exp2((s - m_new) * exp2_scale)
            l_ref[g] = alpha * l_ref[g] + jnp.sum(p, axis=0, keepdims=True)
            vt = vt_ref[0, j, g * HEAD_DIM:(g + 1) * HEAD_DIM, :]
            pv = jnp.dot(vt, p.astype(BF16), preferred_element_type=F32)
            acc_ref[g] = alpha * acc_ref[g] + pv
            mx_ref[g] = m_new

    def far_body(j, carry):
        update(j, None)
        return carry

    lax.fori_loop(0, jnp.maximum(i - 1, 0), far_body, 0)

    @pl.when(i >= 1)
    def _():
        update(i - 1, 1)

    update(i, 0)

    for g in range(N_KV_HEADS):
        o = acc_ref[g] / l_ref[g]
        for r in range(KV_GROUP):
            h = g * KV_GROUP + r
            o_ref[:, h * HEAD_DIM:(h + 1) * HEAD_DIM] = o[:, r * tq:(r + 1) * tq].T.astype(o_ref.dtype)


def _attention(qh, kvb, vt_tiles, mask, bias_t, batch, seq, tq, tk):
    nq, nk = seq // tq, seq // tk
    return pl.pallas_call(
        functools.partial(_attn_kernel, tq=tq, tk=tk),
        grid=(batch, nq),
        in_specs=[pl.BlockSpec((N_HEADS, tq, HEAD_DIM), lambda b, i: (0, b * nq + i, 0)),
                  pl.BlockSpec((seq, KV_WIDTH), lambda b, i: (b, 0)),
                  pl.BlockSpec((1, nk, KV_WIDTH, tk), lambda b, i: (b, 0, 0, 0)),
                  pl.BlockSpec((1, nk, tk, tq), lambda b, i: (b * nq + i, 0, 0, 0)),
                  pl.BlockSpec((2, N_HEADS, LANES, LANES), lambda b, i: (0, 0, 0, 0))],
        out_specs=pl.BlockSpec((tq, ATT_WIDTH), lambda b, i: (b * nq + i, 0)),
        out_shape=jax.ShapeDtypeStruct((batch * seq, ATT_WIDTH), BF16),
        scratch_shapes=[pltpu.VMEM((N_KV_HEADS, HEAD_DIM, KV_GROUP * tq), F32),
                        pltpu.VMEM((N_KV_HEADS, 1, KV_GROUP * tq), F32),
                        pltpu.VMEM((N_KV_HEADS, 1, KV_GROUP * tq), F32)],
        compiler_params=_cparams(("arbitrary", "arbitrary")),
        name="masked_attention",
    )(qh, kvb, vt_tiles, mask, bias_t)


def _conv_kernel(halo_ref, cur_ref, w_ref, b_ref, g_ref, bb_ref, o_ref, ext_ref, ph_ref, *, tt, tc, rc, zero_first):
    t = pl.program_id(1)
    halo = halo_ref[0]
    if zero_first:
        halo = jnp.where(t == 0, 0.0, halo)
    ext_ref[0:CONV_HALO, :] = halo
    ext_ref[CONV_HALO:CONV_HALO + tt, :] = cur_ref[0]
    first = CONV_HALO - (CONV_W - 1)
    ph_rows = ph_ref.shape[1]
    for p in range(1, SUBLANES):
        ph_ref[p - 1] = ext_ref[p:p + ph_rows, :]

    def tap_rows(start, cs):
        p, base = start % SUBLANES, start - start % SUBLANES
        if p == 0:
            return ext_ref[base:base + rc, cs]
        return ph_ref[p - 1, base:base + rc, cs]

    for c in range(tc // LANES):
        cs = slice(c * LANES, (c + 1) * LANES)
        for r in range(tt // rc):
            acc = jnp.zeros((rc, LANES), F32) + b_ref[:, cs]
            for j in range(CONV_W):
                acc = acc + w_ref[j:j + 1, cs] * tap_rows(r * rc + first + j, cs)
            mu = jnp.mean(acc, axis=-1, keepdims=True)
            dv = acc - mu
            var = jnp.mean(dv * dv, axis=-1, keepdims=True)
            yn = dv * lax.rsqrt(var + EPS) * g_ref[:, cs] + bb_ref[:, cs]
            o_ref[0, r * rc:(r + 1) * rc, cs] = (yn * jax.nn.sigmoid(yn)).astype(o_ref.dtype)


def _conv(halo_src, cur, conv_w, conv_b, cn_g, cn_b, tt, tc, zero_first):
    b, t, c = cur.shape
    hb = tt // CONV_HALO
    if zero_first:
        halo_spec = pl.BlockSpec((1, CONV_HALO, tc), lambda a, i, j: (a, jnp.maximum(i * hb - 1, 0), j))
    else:
        halo_spec = pl.BlockSpec((1, CONV_HALO, tc), lambda a, i, j: (a, 0, j))
    vec = lambda: pl.BlockSpec((1, tc), lambda a, i, j: (0, j))
    return pl.pallas_call(
        functools.partial(_conv_kernel, tt=tt, tc=tc, rc=min(tt, 64), zero_first=zero_first),
        grid=(b, t // tt, c // tc),
        in_specs=[halo_spec,
                  pl.BlockSpec((1, tt, tc), lambda a, i, j: (a, i, j)),
                  pl.BlockSpec((CONV_W, tc), lambda a, i, j: (0, j)),
                  vec(), vec(), vec()],
        out_specs=pl.BlockSpec((1, tt, tc), lambda a, i, j: (a, i, j)),
        out_shape=jax.ShapeDtypeStruct((b, t, c), BF16),
        scratch_shapes=[pltpu.VMEM((CONV_HALO + tt, tc), F32),
                        pltpu.VMEM((SUBLANES - 1, CONV_HALO + tt - SUBLANES, tc), F32)],
        compiler_params=_cparams(("arbitrary", "arbitrary", "arbitrary")),
        name="conformer_conv",
    )(halo_src, cur, conv_w, conv_b.reshape(1, c), cn_g.reshape(1, c), cn_b.reshape(1, c))


def _outproj_kernel(a_ref, c_ref, wa_ref, wc_ref, x_ref, gt_ref, o_ref):
    acc = jnp.dot(a_ref[0], wa_ref[...], preferred_element_type=F32)
    acc = acc + jnp.dot(c_ref[0], wc_ref[...], preferred_element_type=F32)
    o_ref[0] = x_ref[0] + gt_ref[0] * acc


def _outproj(att3, conv3, wo_a, wo_c, x3, gt3, tm, tn):
    g, r, d = x3.shape
    ka, kc = att3.shape[-1], conv3.shape[-1]
    gt_spec = (pl.BlockSpec((1, 1, tn), lambda a, i, j: (a, 0, j)) if gt3.shape[1] == 1
               else pl.BlockSpec((1, tm, tn), lambda a, i, j: (a, i, j)))
    return pl.pallas_call(
        _outproj_kernel,
        grid=(g, r // tm, d // tn),
        in_specs=[pl.BlockSpec((1, tm, ka), lambda a, i, j: (a, i, 0)),
                  pl.BlockSpec((1, tm, kc), lambda a, i, j: (a, i, 0)),
                  pl.BlockSpec((ka, tn), lambda a, i, j: (0, j)),
                  pl.BlockSpec((kc, tn), lambda a, i, j: (0, j)),
                  pl.BlockSpec((1, tm, tn), lambda a, i, j: (a, i, j)),
                  gt_spec],
        out_specs=pl.BlockSpec((1, tm, tn), lambda a, i, j: (a, i, j)),
        out_shape=jax.ShapeDtypeStruct((g, r, d), F32),
        compiler_params=_cparams(("arbitrary", "arbitrary", "arbitrary")),
        name="outproj_residual",
    )(att3, conv3, wo_a, wo_c, x3, gt3)


def _peer_scores_kernel(ht_ref, wqt_ref, sk_ref, o_ref):
    q_t = jnp.dot(wqt_ref[...], ht_ref[...], preferred_element_type=F32).astype(BF16)
    half = q_t.shape[0] // 2
    o_ref[0:PEER_NKEYS, :] = jnp.dot(sk_ref[0, 0], q_t[:half], preferred_element_type=F32)
    o_ref[PEER_NKEYS:2 * PEER_NKEYS, :] = jnp.dot(sk_ref[0, 1], q_t[half:], preferred_element_type=F32)


def _peer_scores(h2t, wq_t, sk, tm):
    d, n = h2t.shape
    dk = wq_t.shape[0] // PEER_HEADS
    return pl.pallas_call(
        _peer_scores_kernel,
        grid=(n // tm, PEER_HEADS),
        in_specs=[pl.BlockSpec((d, tm), lambda i, h: (0, i)),
                  pl.BlockSpec((dk, d), lambda i, h: (h, 0)),
                  pl.BlockSpec((1, 2, PEER_NKEYS, dk // 2), lambda i, h: (h, 0, 0, 0))],
        out_specs=pl.BlockSpec((2 * PEER_NKEYS, tm), lambda i, h: (h, i)),
        out_shape=jax.ShapeDtypeStruct((PEER_HEADS * 2 * PEER_NKEYS, n), F32),
        compiler_params=_cparams(("arbitrary", "arbitrary")),
        name="peer_subkey_scores",
    )(h2t, wq_t, sk)


def _top16(x):
    rows = x.shape[0]
    rid = lax.broadcasted_iota(I32, x.shape, 0).astype(F32)
    vals = []
    for k in range(PEER_TOPK):
        m = jnp.max(x, axis=0, keepdims=True)
        vals.append(m)
        if k + 1 < PEER_TOPK:
            first = jnp.min(jnp.where(x == m, rid, float(rows)), axis=0, keepdims=True)
            x = jnp.where(rid == first, -jnp.inf, x)
    return vals


def _stack_rows(rows):
    shape = (len(rows), rows[0].shape[1])
    rid = lax.broadcasted_iota(I32, shape, 0)
    out = jnp.zeros(shape, rows[0].dtype)
    for k, row in enumerate(rows):
        out = jnp.where(rid == k, row, out)
    return out


def _peer_route_kernel(s_ref, s1m_ref, s2m_ref, c1_ref, e2_ref, thr_ref):
    s1 = s_ref[0:PEER_NKEYS, :]
    s2 = s_ref[PEER_NKEYS:2 * PEER_NKEYS, :]
    t1 = _top16(s1)
    t2 = _top16(s2)
    t2_all = _stack_rows(t2)
    blocks = [t1[0] + t2_all]
    for a in range(1, 8):
        blocks.append(t1[a] + t2_all[0:8])
    blocks.append(_stack_rows(t1[8:16]) + t2[0])
    top = _top16(jnp.concatenate(blocks, axis=0))
    z = jnp.ones_like(top[0])
    for k in range(1, PEER_TOPK):
        z = z + jnp.exp(top[k] - top[0])
    s1m_ref[...] = jnp.where(s1 >= t1[PEER_TOPK - 1], s1, -jnp.inf)
    s2m_ref[...] = jnp.where(s2 >= t2[PEER_TOPK - 1], s2, -jnp.inf)
    c1_ref[...] = jnp.exp(s1 - t1[0]) / z
    e2_ref[...] = jnp.exp(s2 - t2[0])
    thr_ref[0] = top[PEER_TOPK - 1]


def _peer_route(s_t, tl):
    rows, n = s_t.shape
    big = lambda: pl.BlockSpec((PEER_NKEYS, tl), lambda h, t: (h, t))
    big_shape = jax.ShapeDtypeStruct((PEER_HEADS * PEER_NKEYS, n), F32)
    return pl.pallas_call(
        _peer_route_kernel,
        grid=(PEER_HEADS, n // tl),
        in_specs=[pl.BlockSpec((2 * PEER_NKEYS, tl), lambda h, t: (h, t))],
        out_specs=[big(), big(), big(), big(), pl.BlockSpec((1, 1, tl), lambda h, t: (h, 0, t))],
        out_shape=[big_shape, big_shape, big_shape, big_shape,
                   jax.ShapeDtypeStruct((PEER_HEADS, 1, n), F32)],
        compiler_params=_cparams(("arbitrary", "arbitrary")),
        name="peer_route",
    )(s_t)


GATE_ROWS = 64


MXU_COLS = 256


def _peer_dense_kernel(ht_ref, u_ref, vt_ref, s1_ref, c1_ref, s2_ref, e2_ref, thr_ref, o_ref, *, te):
    e = pl.program_id(1)
    tm = ht_ref.shape[1]
    rows_per_tile = te // PEER_NKEYS

    @pl.when(e == 0)
    def _():
        o_ref[...] = jnp.zeros(o_ref.shape, F32)

    chunk = min(MXU_COLS, tm)
    a_chunks = [jnp.dot(u_ref[...], ht_ref[:, c * chunk:(c + 1) * chunk], preferred_element_type=F32)
                for c in range(tm // chunk)]

    s1rows = [[s1_ref[pl.ds(h * PEER_NKEYS + e * rows_per_tile + r, 1), :] for r in range(rows_per_tile)]
              for h in range(PEER_HEADS)]
    c1rows = [[c1_ref[pl.ds(h * PEER_NKEYS + e * rows_per_tile + r, 1), :] for r in range(rows_per_tile)]
              for h in range(PEER_HEADS)]
    for c in range(tm // chunk):
        w_cols = []
        for lc in range(chunk // LANES):
            ls = slice(c * chunk + lc * LANES, c * chunk + (lc + 1) * LANES)
            als = slice(lc * LANES, (lc + 1) * LANES)
            blocks = [[None] * (PEER_NKEYS // GATE_ROWS) for _ in range(rows_per_tile)]
            for part in range(PEER_NKEYS // GATE_ROWS):
                rs = [slice(r * PEER_NKEYS + part * GATE_ROWS, r * PEER_NKEYS + (part + 1) * GATE_ROWS)
                      for r in range(rows_per_tile)]
                acc = [jnp.zeros((GATE_ROWS, LANES), F32) for _ in range(rows_per_tile)]
                for h in range(PEER_HEADS):
                    row0 = h * PEER_NKEYS + part * GATE_ROWS
                    s2 = s2_ref[row0:row0 + GATE_ROWS, ls]
                    e2 = e2_ref[row0:row0 + GATE_ROWS, ls]
                    thr = thr_ref[h, :, ls]
                    for r in range(rows_per_tile):
                        cand = s2 + s1rows[h][r][:, ls]
                        acc[r] = acc[r] + jnp.where(cand >= thr, e2, 0.0) * c1rows[h][r][:, ls]
                for r in range(rows_per_tile):
                    blocks[r][part] = (acc[r] * jax.nn.gelu(a_chunks[c][rs[r], als])).astype(BF16)
            w_cols.append(jnp.concatenate([b for row in blocks for b in row], axis=0))
        cs = slice(c * chunk, (c + 1) * chunk)
        o_ref[:, cs] += jnp.dot(vt_ref[...], jnp.concatenate(w_cols, axis=1), preferred_element_type=F32)


def _peer_dense(h2t, u_bf, vt_bf, s1m, c1, s2m, e2, thr, tm, te):
    d, n = h2t.shape
    rows = PEER_HEADS * PEER_NKEYS
    once = pl.Buffered(1)
    tok = lambda: pl.BlockSpec((rows, tm), lambda i, e: (0, i), pipeline_mode=once)
    return pl.pallas_call(
        functools.partial(_peer_dense_kernel, te=te),
        grid=(n // tm, u_bf.shape[0] // te),
        in_specs=[pl.BlockSpec((d, tm), lambda i, e: (0, i), pipeline_mode=once),
                  pl.BlockSpec((te, d), lambda i, e: (e, 0)),
                  pl.BlockSpec((d, te), lambda i, e: (0, e)),
                  tok(), tok(), tok(), tok(),
                  pl.BlockSpec((PEER_HEADS, 1, tm), lambda i, e: (0, 0, i), pipeline_mode=once)],
        out_specs=pl.BlockSpec((d, tm), lambda i, e: (0, i), pipeline_mode=once),
        out_shape=jax.ShapeDtypeStruct((d, n), F32),
        compiler_params=_cparams(("arbitrary", "arbitrary"), VMEM_LIMIT_PEER),
        name="peer_dense_experts",
    )(h2t, u_bf, vt_bf, s1m, c1, s2m, e2, thr)


def _final_kernel(x_ref, gt_ref, p_ref, g_ref, o_ref, *, normalize):
    xx = x_ref[0] + gt_ref[0] * p_ref[...].T
    if normalize:
        ms = jnp.mean(xx * xx, axis=-1, keepdims=True)
        xx = xx * lax.rsqrt(ms + EPS) * g_ref[...]
    o_ref[0] = xx


def _final(x3, gt3, peer_t, g_final, tm, normalize):
    g, r, d = x3.shape
    nb = r // tm
    return pl.pallas_call(
        functools.partial(_final_kernel, normalize=normalize),
        grid=(g, nb),
        in_specs=[pl.BlockSpec((1, tm, d), lambda a, i: (a, i, 0)),
                  _row_mod_spec(gt3, tm),
                  pl.BlockSpec((d, tm), lambda a, i: (0, a * nb + i)),
                  pl.BlockSpec((1, d), lambda a, i: (0, 0))],
        out_specs=pl.BlockSpec((1, tm, d), lambda a, i: (a, i, 0)),
        out_shape=jax.ShapeDtypeStruct((g, r, d), F32),
        compiler_params=_cparams(("arbitrary", "arbitrary")),
        name="final_norm",
    )(x3, gt3, peer_t, g_final.reshape(1, d))


def _sample_index_kernel(pt_ref, qi_ref, wi_ref, kself_ref, expand_ref, cki_ref, mask_ref, self_ref,
                         kbuf_ref, sem_ref, sc_ref, *, n_pages, topk, page_base):
    b = pl.program_id(0)
    slot = b % 2
    score_scale = IDX_DIM ** -0.5 * IDX_HEADS ** -0.5

    def page_copy(seq, k, sl):
        page = page_base + pt_ref[seq, k]
        return pltpu.make_async_copy(cki_ref.at[page], kbuf_ref.at[sl, :, pl.ds(k * PAGE_SIZE, PAGE_SIZE)],
                                     sem_ref.at[sl])

    @pl.when(b == 0)
    def _():
        for k in range(n_pages):
            page_copy(0, k, 0).start()

    @pl.when(b + 1 < pl.num_programs(0))
    def _():
        for k in range(n_pages):
            page_copy(b + 1, k, 1 - slot).start()

    for k in range(n_pages):
        page_copy(b, k, slot).wait()

    qi = qi_ref[0]
    wi = wi_ref[0]
    s = jnp.dot(qi, kbuf_ref[slot].astype(BF16), preferred_element_type=F32)
    row = jnp.sum(jnp.maximum(s, 0.0) * wi, axis=0, keepdims=True) * score_scale
    for k in range(n_pages):
        sc_ref[k:k + 1, :] = row[:, k * PAGE_SIZE:(k + 1) * PAGE_SIZE]

    ks = kself_ref[0].astype(BF16).astype(F32)
    s_self = jnp.sum(qi.astype(F32) * ks, axis=1, keepdims=True)
    self_score = jnp.sum(jnp.maximum(s_self, 0.0) * wi, axis=0, keepdims=True) * score_scale
    keys = _sortable(sc_ref[...])
    kself = _sortable(self_score)
    pos = (lax.broadcasted_iota(I32, keys.shape, 0) * PAGE_SIZE
           + lax.broadcasted_iota(I32, keys.shape, 1))
    self_pos = n_pages * PAGE_SIZE

    def total(x, xs):
        return jnp.sum(jnp.sum(x, axis=1, keepdims=True), axis=0, keepdims=True) + xs

    def bit_body(t, thr):
        cand = thr ^ lax.shift_left(jnp.int32(1), 31 - t)
        cnt = total(jnp.where(keys >= cand, 1, 0), jnp.where(kself >= cand, 1, 0))
        return jnp.where(cnt >= topk, cand, thr)

    thr = lax.fori_loop(0, 32, bit_body, jnp.full((1, 1), INT_MIN, I32))
    need = topk - total(jnp.where(keys > thr, 1, 0), jnp.where(kself > thr, 1, 0))

    def idx_body(t, c):
        cand = c | lax.shift_left(jnp.int32(1), 15 - t)
        f = total(jnp.where(keys == thr, jnp.where(pos < cand, 1, 0), 0),
                  jnp.where(kself == thr, jnp.where(self_pos < cand, 1, 0), 0))
        return jnp.where(f < need, cand, c)

    cstar = lax.fori_loop(0, 16, idx_body, jnp.zeros((1, 1), I32))
    picked = jnp.where(keys > thr, 1.0, jnp.where(keys == thr, jnp.where(pos <= cstar, 1.0, 0.0), 0.0))
    rows = jnp.dot(picked.astype(BF16), expand_ref[...], preferred_element_type=F32)
    mask_ref[0] = jnp.where(rows > 0.5, 0.0, NEG)
    ssel = jnp.where(kself > thr, 0.0, jnp.where(kself == thr, jnp.where(self_pos <= cstar, 0.0, NEG), NEG))
    self_ref[0] = jnp.zeros((SUBLANES, LANES), F32) + ssel


def _sample_index(page_table, qi3, wi3, kself3, cki_t, page_base):
    db, n_pages = page_table.shape
    topk = min(TOPK_MAX, (n_pages * PAGE_SIZE + 1) // 4)
    page_rows = PAGE_SIZE * N_KV_HEADS
    expand = (jnp.arange(page_rows, dtype=I32)[None, :] // N_KV_HEADS
              == jnp.arange(PAGE_SIZE, dtype=I32)[:, None]).astype(BF16)
    grid_spec = pltpu.PrefetchScalarGridSpec(
        num_scalar_prefetch=1,
        grid=(db,),
        in_specs=[pl.BlockSpec((1, IDX_HEADS, IDX_DIM), lambda b, pt: (b, 0, 0)),
                  pl.BlockSpec((1, IDX_HEADS, 1), lambda b, pt: (b, 0, 0)),
                  pl.BlockSpec((1, 1, IDX_DIM), lambda b, pt: (b, 0, 0)),
                  pl.BlockSpec((PAGE_SIZE, page_rows), lambda b, pt: (0, 0)),
                  pl.BlockSpec(memory_space=pl.ANY)],
        out_specs=[pl.BlockSpec((1, n_pages, page_rows), lambda b, pt: (b, 0, 0)),
                   pl.BlockSpec((1, SUBLANES, LANES), lambda b, pt: (b, 0, 0))],
        scratch_shapes=[pltpu.VMEM((2, IDX_DIM, n_pages * PAGE_SIZE), F32),
                        pltpu.SemaphoreType.DMA((2,)),
                        pltpu.VMEM((n_pages, PAGE_SIZE), F32)],
    )
    return pl.pallas_call(
        functools.partial(_sample_index_kernel, n_pages=n_pages, topk=topk, page_base=page_base),
        grid_spec=grid_spec,
        out_shape=[jax.ShapeDtypeStruct((db, n_pages, page_rows), F32),
                   jax.ShapeDtypeStruct((db, SUBLANES, LANES), F32)],
        compiler_params=_cparams(("arbitrary",)),
        name="sample_indexer",
    )(page_table, qi3, wi3, kself3, expand, cki_t)


def _sample_attn_kernel(pt_ref, q_ref, kself_ref, vself_ref, mask_ref, self_ref, bias_ref, bself_ref,
                        *rest, n_pages):
    pps = ATTN_PAGES_PER_STEP
    kpages = rest[:pps]
    vpages = rest[pps:2 * pps]
    o_ref, acc_ref, mx_ref, l_ref = rest[2 * pps:]
    step = pl.program_id(1)
    n_steps = n_pages // pps
    page_rows = PAGE_SIZE * N_KV_HEADS
    sm_scale = HEAD_DIM ** -0.5
    q = q_ref[0]
    head_group = lax.broadcasted_iota(I32, (N_HEADS, HEAD_DIM), 0) // KV_GROUP

    def own_group_rows(x_ref):
        out = jnp.zeros((N_HEADS, HEAD_DIM), F32)
        for g in range(N_KV_HEADS):
            out = jnp.where(head_group == g, x_ref[0, g:g + 1, :].astype(BF16).astype(F32), out)
        return out

    @pl.when(step == 0)
    def _():
        logit = jnp.sum(q.astype(F32) * own_group_rows(kself_ref), axis=1, keepdims=True)
        mx_ref[...] = logit * sm_scale + bself_ref[...] + self_ref[0, 0:1, 0:1]
        l_ref[...] = jnp.ones(l_ref.shape, F32)
        acc_ref[...] = own_group_rows(vself_ref)

    kcat = jnp.concatenate([kpages[k][0].astype(BF16) for k in range(pps)], axis=0)
    vcat = jnp.concatenate([vpages[k][0].astype(BF16) for k in range(pps)], axis=0)
    s = _dot_nt(q, kcat) * sm_scale + bias_ref[step]
    s = jnp.concatenate([s[:, k * page_rows:(k + 1) * page_rows] + mask_ref[0, pl.ds(step * pps + k, 1), :]
                         for k in range(pps)], axis=1)
    m_old = mx_ref[...]
    m_new = jnp.maximum(m_old, jnp.max(s, axis=1, keepdims=True))
    alpha = jnp.exp(m_old - m_new)
    p = jnp.exp(s - m_new)
    l_ref[...] = alpha * l_ref[...] + jnp.sum(p, axis=1, keepdims=True)
    acc_ref[...] = alpha * acc_ref[...] + jnp.dot(p.astype(BF16), vcat, preferred_element_type=F32)
    mx_ref[...] = m_new

    @pl.when(step == n_steps - 1)
    def _():
        o_ref[0] = (acc_ref[...] / l_ref[...]).astype(o_ref.dtype)


def _sample_attn(page_table, q3, kself3, vself3, mask, selfsel, bias_steps, bias_self, ck, cv, page_base):
    db, n_pages = page_table.shape
    pps = ATTN_PAGES_PER_STEP
    n_steps = n_pages // pps
    page_rows = PAGE_SIZE * N_KV_HEADS
    step_rows = pps * page_rows

    def page_spec(k):
        return pl.BlockSpec((1, page_rows, HEAD_DIM),
                            lambda b, s, pt: (page_base + pt[b, s * pps + k], 0, 0))

    per_b = lambda shape: pl.BlockSpec((1,) + shape, lambda b, s, pt: (b, 0, 0))
    grid_spec = pltpu.PrefetchScalarGridSpec(
        num_scalar_prefetch=1,
        grid=(db, n_steps),
        in_specs=[per_b((N_HEADS, HEAD_DIM)), per_b((N_KV_HEADS, HEAD_DIM)), per_b((N_KV_HEADS, HEAD_DIM)),
                  per_b((n_pages, page_rows)), per_b((SUBLANES, LANES)),
                  pl.BlockSpec((n_steps, N_HEADS, step_rows), lambda b, s, pt: (0, 0, 0)),
                  pl.BlockSpec((N_HEADS, 1), lambda b, s, pt: (0, 0))]
                 + [page_spec(k) for k in range(pps)] * 2,
        out_specs=per_b((N_HEADS, HEAD_DIM)),
        scratch_shapes=[pltpu.VMEM((N_HEADS, HEAD_DIM), F32),
                        pltpu.VMEM((N_HEADS, 1), F32),
                        pltpu.VMEM((N_HEADS, 1), F32)],
    )
    return pl.pallas_call(
        functools.partial(_sample_attn_kernel, n_pages=n_pages),
        grid_spec=grid_spec,
        out_shape=jax.ShapeDtypeStruct((db, N_HEADS, HEAD_DIM), BF16),
        compiler_params=_cparams(("arbitrary", "arbitrary")),
        name="sample_attention",
    )(page_table, q3, kself3, vself3, mask, selfsel, bias_steps, bias_self,
      *([ck] * pps), *([cv] * pps))


def _rel_bucket(dist):
    n = jnp.maximum(dist, 0)
    max_exact = REL_BUCKETS // 2
    nf = jnp.maximum(n, 1).astype(F32)
    large = max_exact + (jnp.log(nf / max_exact) / math.log(REL_MAX_DIST / max_exact)
                         * (REL_BUCKETS - max_exact)).astype(I32)
    large = jnp.minimum(large, REL_BUCKETS - 1)
    return jnp.where(n < max_exact, n, large)


def _bias_of_dist(rel_bias, dist):
    onehot = (_rel_bucket(dist)[..., None] == jnp.arange(REL_BUCKETS, dtype=I32)).astype(F32)
    return jnp.einsum("...b,bh->...h", onehot, rel_bias.astype(F32), precision=lax.Precision.HIGHEST)


def _prompt_bias_tiles(rel_bias):
    kc = jnp.arange(LANES, dtype=I32)[:, None]
    qr = jnp.arange(LANES, dtype=I32)[None, :]
    far = rel_bias[REL_BUCKETS - 1]
    tiles = []
    for off in (0, LANES):
        dist = off + qr - kc
        t = (_bias_of_dist(rel_bias, dist) - far) * HEAD_DIM ** 0.5
        tiles.append(jnp.where((dist >= 0)[..., None], t, 0.0).transpose(2, 0, 1))
    return jnp.stack(tiles).astype(F32)


def _sample_bias_steps(rel_bias, n_pages):
    past = n_pages * PAGE_SIZE
    bias = _bias_of_dist(rel_bias, past - jnp.arange(past, dtype=I32))
    own = (jnp.arange(N_HEADS, dtype=I32)[:, None] // KV_GROUP) == jnp.arange(N_KV_HEADS, dtype=I32)[None, :]
    rows = jnp.where(own[None], bias[:, :, None], NEG)
    rows = rows.transpose(1, 0, 2).reshape(N_HEADS, n_pages // ATTN_PAGES_PER_STEP, -1)
    return rows.transpose(1, 0, 2).astype(F32)


def _split_in_proj(w_in_l):
    sizes = (ATT_WIDTH, KV_WIDTH, KV_WIDTH, IDX_HEADS * IDX_DIM, IDX_DIM, IDX_HEADS)
    offs = [0]
    for s in sizes:
        offs.append(offs[-1] + s)
    conv_ch = (w_in_l.shape[1] - offs[-1]) // 2
    d = w_in_l.shape[0]
    wq, wk, wv, wqi, wki, wwi = (w_in_l[:, offs[i]:offs[i + 1]] for i in range(6))
    wua = w_in_l[:, offs[-1]:offs[-1] + conv_ch]
    wub = w_in_l[:, offs[-1] + conv_ch:]
    z = lambda n: jnp.zeros((d, n), w_in_l.dtype)
    w_a = jnp.concatenate([wk, wv, wki, z(LANES - IDX_DIM), z(LANES - IDX_DIM), wki,
                           wwi, z(LANES - IDX_HEADS), z(LANES)], axis=1)
    w_b = jnp.concatenate([wq, wqi], axis=1)
    chunk = 256
    parts = []
    for c in range(conv_ch // chunk):
        parts += [wua[:, c * chunk:(c + 1) * chunk], wub[:, c * chunk:(c + 1) * chunk]]
    w_c = jnp.concatenate(parts, axis=1)
    return w_a.astype(BF16), w_b.astype(BF16), w_c.astype(BF16)


COL_K, COL_V, COL_KA, COL_WI = 0, KV_WIDTH, 2 * KV_WIDTH, 2 * KV_WIDTH + 2 * LANES


def _mixer_projections(h, w_a, w_b, w_c, tm):
    z_a = _matmul(h, w_a, F32, tm, 512)
    qh = _matmul(h, w_b, BF16, tm, 512, head_major=True)
    u = _matmul(h, w_c, F32, tm, 512, glu=True)
    return z_a, qh, u


ATTN_TILE = 256
PROJ_TM, PROJ_TN = 1024, 512
NORM_ROWS = 512
FINAL_ROWS = 256
CONV_TT, CONV_TC = 256, 512
PEER_TM, PEER_TE = 512, 1024
PEER_TE_DECODE = 512
ROUTE_LANES = 256


def _peer_block(h2t, wq_t, sk_bf, u_bf, vt_bf, tm, te):
    s_t = _peer_scores(h2t, wq_t, sk_bf, tm)
    s1m, s2m, c1, e2, thr = _peer_route(s_t, min(tm, ROUTE_LANES))
    return _peer_dense(h2t, u_bf, vt_bf, s1m, c1, s2m, e2, thr, tm, te)


def kernel(x_prompt, x_sample, cache_k, cache_v, cache_kidx, state_conv, page_table, c_prompt, c_sample,
           rel_bias, w_ada, b_ada, g_mix, w_in, conv_w, conv_b, cn_g, cn_b, w_o, g_ch, peer_wq,
           peer_subkeys, peer_u, peer_v, g_final):
    batch, seq, d = x_prompt.shape
    db = x_sample.shape[0]
    depth = w_ada.shape[0]
    n_pages = page_table.shape[1]
    conv_ch = conv_w.shape[-1]
    n_prompt = batch * seq
    tq = tk = ATTN_TILE

    xp = x_prompt
    xs = jnp.pad(x_sample.reshape(1, db, d), ((0, 0), (0, SAMPLE_ROWS - db), (0, 0)))
    c_rows = batch + db
    c_pad = (-c_rows) % 16
    c_all = jnp.pad(jnp.concatenate([c_prompt, c_sample], axis=0), ((0, c_pad), (0, 0)))
    bias_t = _prompt_bias_tiles(rel_bias)
    bias_steps = _sample_bias_steps(rel_bias, n_pages)
    n_pool = cache_k.shape[1]
    page_rows = PAGE_SIZE * N_KV_HEADS
    ck_rows = cache_k.reshape(depth * n_pool, page_rows, HEAD_DIM)
    cv_rows = cache_v.reshape(depth * n_pool, page_rows, HEAD_DIM)
    cki_t = jnp.swapaxes(cache_kidx, 2, 3).reshape(depth * n_pool, IDX_DIM, PAGE_SIZE)
    bias_self = rel_bias[_rel_bucket(jnp.zeros((), I32))].reshape(N_HEADS, 1)

    outs = {k: [] for k in ("kp", "vp", "kip", "cp", "ks", "vs", "kis", "cs")}
    for l in range(depth):
        mods = _adaln(c_all, w_ada[l], b_ada[l])
        p_mod = [m.reshape(batch, 1, d) for m in jnp.split(mods[:batch], 6, axis=-1)]
        s_mod = [jnp.pad(m.reshape(1, db, d), ((0, 0), (0, SAMPLE_ROWS - db), (0, 0)))
                 for m in jnp.split(mods[batch:c_rows], 6, axis=-1)]
        w_a, w_b, w_c = _split_in_proj(w_in[l])
        wo_a = w_o[l][:ATT_WIDTH].astype(BF16)
        wo_c = w_o[l][ATT_WIDTH:].astype(BF16)

        hp = _modulate(xp, g_mix[l], p_mod[1], p_mod[0], NORM_ROWS).reshape(n_prompt, d)
        z_a, qh, u = _mixer_projections(hp, w_a, w_b, w_c, PROJ_TM)
        kvb = z_a[:, :COL_WI].astype(BF16)
        wi_t = z_a[:, COL_WI:COL_WI + IDX_HEADS].T
        vt_tiles = (kvb[:, COL_V:COL_V + KV_WIDTH].reshape(batch, seq // tk, tk, KV_WIDTH)
                    .transpose(0, 1, 3, 2))
        mask = _indexer(qh, kvb, wi_t, batch, seq, tq, tk, qi_blk=N_HEADS // (IDX_HEADS // 2),
                        ka_blk=COL_KA // LANES)
        att = _attention(qh, kvb, vt_tiles, mask, bias_t, batch, seq, tq, tk)
        u3 = u.reshape(batch, seq, conv_ch)
        conv = _conv(u3, u3, conv_w[l], conv_b[l], cn_g[l], cn_b[l], CONV_TT, CONV_TC, zero_first=True)
        xp = _outproj(att.reshape(batch, seq, ATT_WIDTH), conv, wo_a, wo_c, xp, p_mod[2], PROJ_TM, PROJ_TN)
        outs["kp"].append(z_a[:, COL_K:COL_K + KV_WIDTH].reshape(batch, seq, N_KV_HEADS, HEAD_DIM))
        outs["vp"].append(z_a[:, COL_V:COL_V + KV_WIDTH].reshape(batch, seq, N_KV_HEADS, HEAD_DIM))
        outs["kip"].append(z_a[:, COL_KA:COL_KA + IDX_DIM].reshape(batch, seq, IDX_DIM))
        outs["cp"].append(u3[:, seq - (CONV_W - 1):])

        hs = _modulate(xs, g_mix[l], s_mod[1], s_mod[0], SAMPLE_ROWS).reshape(SAMPLE_ROWS, d)
        zs_a, qhs, us = _mixer_projections(hs, w_a, w_b, w_c, SAMPLE_ROWS)
        k_new = zs_a[:db, COL_K:COL_K + KV_WIDTH]
        v_new = zs_a[:db, COL_V:COL_V + KV_WIDTH]
        ki_new = zs_a[:db, COL_KA:COL_KA + IDX_DIM]
        wi_new = zs_a[:db, COL_WI:COL_WI + IDX_HEADS]
        q_s = qhs[:N_HEADS, :db].transpose(1, 0, 2)
        qi_s = (qhs[N_HEADS:, :db].transpose(1, 0, 2)
                .reshape(db, IDX_HEADS // 2, 2, IDX_DIM).reshape(db, IDX_HEADS, IDX_DIM))
        smask, sself = _sample_index(page_table, qi_s, wi_new.reshape(db, IDX_HEADS, 1),
                                     ki_new.reshape(db, 1, IDX_DIM), cki_t, l * n_pool)
        att_s = _sample_attn(page_table, q_s, k_new.reshape(db, N_KV_HEADS, HEAD_DIM),
                             v_new.reshape(db, N_KV_HEADS, HEAD_DIM), smask, sself, bias_steps, bias_self,
                             ck_rows, cv_rows, l * n_pool)
        att_s = jnp.pad(att_s.reshape(1, db, ATT_WIDTH), ((0, 0), (0, SAMPLE_ROWS - db), (0, 0)))
        u_new = us[:db]
        state = state_conv[l].astype(F32)
        halo = jnp.pad(state, ((0, 0), (CONV_HALO - (CONV_W - 1), 0), (0, 0)))
        cur = jnp.pad(u_new.reshape(db, 1, conv_ch), ((0, 0), (0, SUBLANES - 1), (0, 0)))
        conv_s = _conv(halo, cur, conv_w[l], conv_b[l], cn_g[l], cn_b[l], SUBLANES, CONV_TC, zero_first=False)
        conv_s = jnp.pad(conv_s[:, 0].reshape(1, db, conv_ch), ((0, 0), (0, SAMPLE_ROWS - db), (0, 0)))
        xs = _outproj(att_s, conv_s, wo_a, wo_c, xs, s_mod[2], SAMPLE_ROWS, PROJ_TN)
        outs["ks"].append(k_new.reshape(db, 1, N_KV_HEADS, HEAD_DIM))
        outs["vs"].append(v_new.reshape(db, 1, N_KV_HEADS, HEAD_DIM))
        outs["kis"].append(ki_new.reshape(db, 1, IDX_DIM))
        outs["cs"].append(jnp.concatenate([state[:, 1:], u_new.reshape(db, 1, conv_ch)], axis=1))

        wq_t = peer_wq[l].astype(BF16).T
        sk_bf = peer_subkeys[l].astype(BF16)
        u_bf = peer_u[l].astype(BF16)
        vt_bf = peer_v[l].astype(BF16).T
        hp2 = _modulate(xp, g_ch[l], p_mod[4], p_mod[3], NORM_ROWS, transposed=True)
        peer_p = _peer_block(hp2, wq_t, sk_bf, u_bf, vt_bf, PEER_TM, PEER_TE)
        hs2 = _modulate(xs, g_ch[l], s_mod[4], s_mod[3], SAMPLE_ROWS, transposed=True)
        peer_s = _peer_block(hs2, wq_t, sk_bf, u_bf, vt_bf, SAMPLE_ROWS, PEER_TE_DECODE)
        last = l == depth - 1
        xp = _final(xp, p_mod[5], peer_p, g_final, FINAL_ROWS, normalize=last)
        xs = _final(xs, s_mod[5], peer_s, g_final, SAMPLE_ROWS, normalize=last)

    st = lambda k: jnp.stack(outs[k])
    y_sample = xs[0, :db].reshape(db, 1, d)
    return (xp, y_sample, st("kp"), st("vp"), st("kip"), st("cp"),
            st("ks"), st("vs"), st("kis"), st("cs"))
```

```python
import functools
import math

import jax
import jax.numpy as jnp
from jax import lax
from jax.experimental import pallas as pl
from jax.experimental.pallas import tpu as pltpu

F32 = jnp.float32
BF16 = jnp.bfloat16
I32 = jnp.int32

HEAD_DIM = 128
N_HEADS = 16
N_KV_HEADS = 4
KV_GROUP = N_HEADS // N_KV_HEADS
ATT_WIDTH = N_HEADS * HEAD_DIM
KV_WIDTH = N_KV_HEADS * HEAD_DIM
IDX_HEADS = 16
IDX_DIM = 64
TOPK_MAX = 256
REL_BUCKETS = 32
REL_MAX_DIST = 128
CONV_W = 31
PEER_HEADS = 8
PEER_NKEYS = 128
PEER_TOPK = 16
EPS = 1e-6
PAGE_SIZE = 128

LANES = 128
SUBLANES = 8
VMEM_LIMIT = 56 * 1024 * 1024
VMEM_LIMIT_PEER = 63 * 1024 * 1024

NEG = -1e30
INT_MIN = -(2 ** 31)
CONV_HALO = 32
SAMPLE_ROWS = 128
ATTN_PAGES_PER_STEP = 16


def _cparams(sem, vmem_limit=VMEM_LIMIT):
    return pltpu.CompilerParams(dimension_semantics=sem, vmem_limit_bytes=vmem_limit)


def _dot_nt(a, b):
    return lax.dot_general(a, b, (((1,), (1,)), ((), ())), preferred_element_type=F32)


def _sortable(x):
    bits = pltpu.bitcast(x, I32)
    return bits ^ ((bits >> 31) & jnp.int32(0x7FFFFFFF))


def _adaln_kernel(c_ref, w_ref, b_ref, o_ref):
    c = c_ref[...]
    a = (c * jax.nn.sigmoid(c)).astype(BF16)
    o_ref[...] = jnp.dot(a, w_ref[...].astype(BF16), preferred_element_type=F32) + b_ref[...]


def _adaln(c, w_ada, b_ada, tn=512):
    r, d = c.shape
    n = w_ada.shape[1]
    return pl.pallas_call(
        _adaln_kernel,
        grid=(n // tn,),
        in_specs=[pl.BlockSpec((r, d), lambda j: (0, 0)),
                  pl.BlockSpec((d, tn), lambda j: (0, j)),
                  pl.BlockSpec((1, tn), lambda j: (0, j))],
        out_specs=pl.BlockSpec((r, tn), lambda j: (0, j)),
        out_shape=jax.ShapeDtypeStruct((r, n), F32),
        compiler_params=_cparams(("arbitrary",)),
        name="adaln",
    )(c, w_ada, b_ada.reshape(1, n))


def _modulate_kernel(x_ref, g_ref, sc_ref, sh_ref, o_ref, *, transposed):
    x = x_ref[0]
    ms = jnp.mean(x * x, axis=-1, keepdims=True)
    y = x * lax.rsqrt(ms + EPS) * g_ref[...]
    y = y * (1.0 + sc_ref[0]) + sh_ref[0]
    if transposed:
        o_ref[...] = y.T.astype(o_ref.dtype)
    else:
        o_ref[0] = y.astype(o_ref.dtype)


def _row_mod_spec(mod, tr):
    d = mod.shape[-1]
    if mod.shape[1] == 1:
        return pl.BlockSpec((1, 1, d), lambda g, r, *_: (g, 0, 0))
    return pl.BlockSpec((1, tr, d), lambda g, r, *_: (g, r, 0))


def _modulate(x3, gain, sc3, sh3, tr, transposed=False):
    g, r, d = x3.shape
    nb = r // tr
    if transposed:
        out_spec = pl.BlockSpec((d, tr), lambda a, b: (0, a * nb + b))
        out_shape = jax.ShapeDtypeStruct((d, g * r), BF16)
    else:
        out_spec = pl.BlockSpec((1, tr, d), lambda a, b: (a, b, 0))
        out_shape = jax.ShapeDtypeStruct((g, r, d), BF16)
    return pl.pallas_call(
        functools.partial(_modulate_kernel, transposed=transposed),
        grid=(g, nb),
        in_specs=[pl.BlockSpec((1, tr, d), lambda a, b: (a, b, 0)),
                  pl.BlockSpec((1, d), lambda a, b: (0, 0)),
                  _row_mod_spec(sc3, tr),
                  _row_mod_spec(sh3, tr)],
        out_specs=out_spec,
        out_shape=out_shape,
        compiler_params=_cparams(("arbitrary", "arbitrary")),
        name="modulate",
    )(x3, gain.reshape(1, d), sc3, sh3)


def _mm_kernel(h_ref, w_ref, o_ref, *, glu, head_major):
    acc = jnp.dot(h_ref[...], w_ref[...], preferred_element_type=F32)
    if glu:
        half = acc.shape[1] // 2
        acc = acc[:, :half] * jax.nn.sigmoid(acc[:, half:])
    if head_major:
        for c in range(acc.shape[1] // LANES):
            o_ref[c] = acc[:, c * LANES:(c + 1) * LANES].astype(o_ref.dtype)
    else:
        o_ref[...] = acc.astype(o_ref.dtype)


def _matmul(h, w, out_dtype, tm, tn, glu=False, head_major=False):
    m, k = h.shape
    n = w.shape[1]
    n_out = n // 2 if glu else n
    tn_out = tn // 2 if glu else tn
    if head_major:
        out_shape = jax.ShapeDtypeStruct((n_out // LANES, m, LANES), out_dtype)
        out_spec = pl.BlockSpec((tn_out // LANES, tm, LANES), lambda i, j: (j, i, 0))
    else:
        out_shape = jax.ShapeDtypeStruct((m, n_out), out_dtype)
        out_spec = pl.BlockSpec((tm, tn_out), lambda i, j: (i, j))
    return pl.pallas_call(
        functools.partial(_mm_kernel, glu=glu, head_major=head_major),
        grid=(m // tm, n // tn),
        in_specs=[pl.BlockSpec((tm, k), lambda i, j: (i, 0)),
                  pl.BlockSpec((k, tn), lambda i, j: (0, j))],
        out_specs=out_spec,
        out_shape=out_shape,
        compiler_params=_cparams(("arbitrary", "arbitrary")),
        name="proj_matmul",
    )(h, w)


def _indexer_kernel(qi_ref, ka_ref, kb_ref, wi_ref, o_ref, keys_ref, cst_ref, *, tq, tk, nk, topk):
    i = pl.program_id(1)
    q0 = i * tq
    nvis = (q0 + tq + tk - 1) // tk
    qpos = q0 + lax.broadcasted_iota(I32, (1, tq), 1)
    w = wi_ref[...]
    score_scale = IDX_DIM ** -0.5 * IDX_HEADS ** -0.5

    def kpos_of(j):
        return j * tk + lax.broadcasted_iota(I32, (tk, 1), 0)

    def score_body(j, carry):
        k0 = pl.multiple_of(j * tk, tk)
        ka = ka_ref[pl.ds(k0, tk), :]
        kb = kb_ref[pl.ds(k0, tk), :]
        acc = jnp.zeros((tk, tq), F32)
        for p in range(IDX_HEADS // 2):
            qp = qi_ref[p]
            sa = _dot_nt(ka, qp)
            sb = _dot_nt(kb, qp)
            acc = acc + jnp.maximum(sa, 0.0) * w[2 * p:2 * p + 1]
            acc = acc + jnp.maximum(sb, 0.0) * w[2 * p + 1:2 * p + 2]
        acc = acc * score_scale
        acc = jnp.where(kpos_of(j) <= qpos, acc, -jnp.inf)
        keys_ref[j] = _sortable(acc)
        return carry

    lax.fori_loop(0, nvis, score_body, 0)

    def count(pred):
        def body(j, c):
            return c + jnp.sum(pred(keys_ref[j], j).reshape(tk // SUBLANES, SUBLANES, tq), axis=0)
        part = lax.fori_loop(0, nvis, body, jnp.zeros((SUBLANES, tq), I32))
        return jnp.sum(part, axis=0, keepdims=True)

    def bit_body(t, thr):
        cand = thr ^ lax.shift_left(jnp.int32(1), 31 - t)
        cnt = count(lambda k, j: jnp.where(k >= cand, 1, 0))
        return jnp.where(cnt >= topk, cand, thr)

    thr = lax.fori_loop(0, 32, bit_body, jnp.full((1, tq), INT_MIN, I32))

    need = topk - count(lambda k, j: jnp.where(k > thr, 1, 0))
    n_eq = count(lambda k, j: jnp.where(k == thr, 1, 0))
    cst_ref[...] = jnp.full((1, tq), nk * tk, I32)

    @pl.when(jnp.max(jnp.where(n_eq > need, 1, 0)) > 0)
    def _():
        def idx_body(t, c):
            cand = c | lax.shift_left(jnp.int32(1), 15 - t)
            f = count(lambda k, j: jnp.where(k == thr, jnp.where(kpos_of(j) < cand, 1, 0), 0))
            return jnp.where(f < need, cand, c)
        cst_ref[...] = lax.fori_loop(0, 16, idx_body, jnp.zeros((1, tq), I32))

    cstar = cst_ref[...]

    def write_body(j, carry):
        k = keys_ref[j]
        kpos = kpos_of(j)
        sel = jnp.where(k > thr, 0.0, jnp.where(k == thr, jnp.where(kpos <= cstar, 0.0, NEG), NEG))
        o_ref[0, j] = jnp.where(kpos <= qpos, sel, NEG).astype(o_ref.dtype)
        return carry

    lax.fori_loop(0, nvis, write_body, 0)

    def fill_body(j, carry):
        o_ref[0, j] = jnp.full((tk, tq), NEG, o_ref.dtype)
        return carry

    lax.fori_loop(nvis, nk, fill_body, 0)


def _indexer(qh, kvb, wi_t, batch, seq, tq, tk, qi_blk, ka_blk):
    nq, nk = seq // tq, seq // tk
    topk = min(TOPK_MAX, seq // 4)
    return pl.pallas_call(
        functools.partial(_indexer_kernel, tq=tq, tk=tk, nk=nk, topk=topk),
        grid=(batch, nq),
        in_specs=[pl.BlockSpec((IDX_HEADS // 2, tq, LANES), lambda b, i: (qi_blk, b * nq + i, 0)),
                  pl.BlockSpec((seq, LANES), lambda b, i: (b, ka_blk)),
                  pl.BlockSpec((seq, LANES), lambda b, i: (b, ka_blk + 1)),
                  pl.BlockSpec((IDX_HEADS, tq), lambda b, i: (0, b * nq + i))],
        out_specs=pl.BlockSpec((1, nk, tk, tq), lambda b, i: (b * nq + i, 0, 0, 0)),
        out_shape=jax.ShapeDtypeStruct((batch * nq, nk, tk, tq), BF16),
        scratch_shapes=[pltpu.VMEM((nk, tk, tq), I32), pltpu.VMEM((1, tq), I32)],
        compiler_params=_cparams(("arbitrary", "arbitrary")),
        name="indexer_topk_mask",
    )(qh, kvb, kvb, wi_t)


def _attn_kernel(q_ref, k_ref, vt_ref, m_ref, bt_ref, o_ref, acc_ref, mx_ref, l_ref, *, tq, tk):
    i = pl.program_id(1)
    exp2_scale = HEAD_DIM ** -0.5 * math.log2(math.e)
    mx_ref[...] = jnp.full(mx_ref.shape, NEG, F32)
    l_ref[...] = jnp.zeros(l_ref.shape, F32)
    acc_ref[...] = jnp.zeros(acc_ref.shape, F32)
    zero_blk = jnp.zeros((LANES, LANES), F32)

    def bias_tile(g, near):
        cols = []
        for r in range(KV_GROUP):
            h = g * KV_GROUP + r
            b0 = bt_ref[0, h]
            b1 = bt_ref[1, h]
            if near == 0:
                top = jnp.concatenate([b0, b1], axis=1)
                bot = jnp.concatenate([zero_blk, b0], axis=1)
            else:
                top = jnp.concatenate([zero_blk, zero_blk], axis=1)
                bot = jnp.concatenate([b1, zero_blk], axis=1)
            cols.append(jnp.concatenate([top, bot], axis=0))
        return jnp.concatenate(cols, axis=1)

    def update(j, near):
        k0 = pl.multiple_of(j * tk, tk)
        mt = m_ref[0, j].astype(F32)
        mt4 = jnp.concatenate([mt] * KV_GROUP, axis=1)
        for g in range(N_KV_HEADS):
            kt = k_ref[pl.ds(k0, tk), g * HEAD_DIM:(g + 1) * HEAD_DIM]
            qs = q_ref[g * KV_GROUP:(g + 1) * KV_GROUP].reshape(KV_GROUP * tq, HEAD_DIM)
            s = _dot_nt(kt, qs) + mt4
            if near is not None:
                s = s + bias_tile(g, near)
            m_old = mx_ref[g]
            m_new = jnp.maximum(m_old, jnp.max(s, axis=0, keepdims=True))
            alpha = jnp.exp2((m_old - m_new) * exp2_scale)
            p = jnp.exp2((s - m_new) * exp2_scale)
            l_ref[g] = alpha * l_ref[g] + jnp.sum(p, axis=0, keepdims=True)
            vt = vt_ref[0, j, g * HEAD_DIM:(g + 1) * HEAD_DIM, :]
            pv = jnp.dot(vt, p.astype(BF16), preferred_element_type=F32)
            acc_ref[g] = alpha * acc_ref[g] + pv
            mx_ref[g] = m_new

    def far_body(j, carry):
        update(j, None)
        return carry

    lax.fori_loop(0, jnp.maximum(i - 1, 0), far_body, 0)

    @pl.when(i >= 1)
    def _():
        update(i - 1, 1)

    update(i, 0)

    for g in range(N_KV_HEADS):
        o = acc_ref[g] / l_ref[g]
        for r in range(KV_GROUP):
            h = g * KV_GROUP + r
            o_ref[:, h * HEAD_DIM:(h + 1) * HEAD_DIM] = o[:, r * tq:(r + 1) * tq].T.astype(o_ref.dtype)


def _attention(qh, kvb, vt_tiles, mask, bias_t, batch, seq, tq, tk):
    nq, nk = seq // tq, seq // tk
    return pl.pallas_call(
        functools.partial(_attn_kernel, tq=tq, tk=tk),
        grid=(batch, nq),
        in_specs=[pl.BlockSpec((N_HEADS, tq, HEAD_DIM), lambda b, i: (0, b * nq + i, 0)),
                  pl.BlockSpec((seq, KV_WIDTH), lambda b, i: (b, 0)),
                  pl.BlockSpec((1, nk, KV_WIDTH, tk), lambda b, i: (b, 0, 0, 0)),
                  pl.BlockSpec((1, nk, tk, tq), lambda b, i: (b * nq + i, 0, 0, 0)),
                  pl.BlockSpec((2, N_HEADS, LANES, LANES), lambda b, i: (0, 0, 0, 0))],
        out_specs=pl.BlockSpec((tq, ATT_WIDTH), lambda b, i: (b * nq + i, 0)),
        out_shape=jax.ShapeDtypeStruct((batch * seq, ATT_WIDTH), BF16),
        scratch_shapes=[pltpu.VMEM((N_KV_HEADS, HEAD_DIM, KV_GROUP * tq), F32),
                        pltpu.VMEM((N_KV_HEADS, 1, KV_GROUP * tq), F32),
                        pltpu.VMEM((N_KV_HEADS, 1, KV_GROUP * tq), F32)],
        compiler_params=_cparams(("arbitrary", "arbitrary")),
        name="masked_attention",
    )(qh, kvb, vt_tiles, mask, bias_t)


def _conv_kernel(halo_ref, cur_ref, w_ref, b_ref, g_ref, bb_ref, o_ref, ext_ref, ph_ref, *, tt, tc, rc, zero_first):
    t = pl.program_id(1)
    halo = halo_ref[0]
    if zero_first:
        halo = jnp.where(t == 0, 0.0, halo)
    ext_ref[0:CONV_HALO, :] = halo
    ext_ref[CONV_HALO:CONV_HALO + tt, :] = cur_ref[0]
    first = CONV_HALO - (CONV_W - 1)
    ph_rows = ph_ref.shape[1]
    for p in range(1, SUBLANES):
        ph_ref[p - 1] = ext_ref[p:p + ph_rows, :]

    def tap_rows(start, cs):
        p, base = start % SUBLANES, start - start % SUBLANES
        if p == 0:
            return ext_ref[base:base + rc, cs]
        return ph_ref[p - 1, base:base + rc, cs]

    for c in range(tc // LANES):
        cs = slice(c * LANES, (c + 1) * LANES)
        for r in range(tt // rc):
            acc = jnp.zeros((rc, LANES), F32) + b_ref[:, cs]
            for j in range(CONV_W):
                acc = acc + w_ref[j:j + 1, cs] * tap_rows(r * rc + first + j, cs)
            mu = jnp.mean(acc, axis=-1, keepdims=True)
            dv = acc - mu
            var = jnp.mean(dv * dv, axis=-1, keepdims=True)
            yn = dv * lax.rsqrt(var + EPS) * g_ref[:, cs] + bb_ref[:, cs]
            o_ref[0, r * rc:(r + 1) * rc, cs] = (yn * jax.nn.sigmoid(yn)).astype(o_ref.dtype)


def _conv(halo_src, cur, conv_w, conv_b, cn_g, cn_b, tt, tc, zero_first):
    b, t, c = cur.shape
    hb = tt // CONV_HALO
    if zero_first:
        halo_spec = pl.BlockSpec((1, CONV_HALO, tc), lambda a, i, j: (a, jnp.maximum(i * hb - 1, 0), j))
    else:
        halo_spec = pl.BlockSpec((1, CONV_HALO, tc), lambda a, i, j: (a, 0, j))
    vec = lambda: pl.BlockSpec((1, tc), lambda a, i, j: (0, j))
    return pl.pallas_call(
        functools.partial(_conv_kernel, tt=tt, tc=tc, rc=min(tt, 64), zero_first=zero_first),
        grid=(b, t // tt, c // tc),
        in_specs=[halo_spec,
                  pl.BlockSpec((1, tt, tc), lambda a, i, j: (a, i, j)),
                  pl.BlockSpec((CONV_W, tc), lambda a, i, j: (0, j)),
                  vec(), vec(), vec()],
        out_specs=pl.BlockSpec((1, tt, tc), lambda a, i, j: (a, i, j)),
        out_shape=jax.ShapeDtypeStruct((b, t, c), BF16),
        scratch_shapes=[pltpu.VMEM((CONV_HALO + tt, tc), F32),
                        pltpu.VMEM((SUBLANES - 1, CONV_HALO + tt - SUBLANES, tc), F32)],
        compiler_params=_cparams(("arbitrary", "arbitrary", "arbitrary")),
        name="conformer_conv",
    )(halo_src, cur, conv_w, conv_b.reshape(1, c), cn_g.reshape(1, c), cn_b.reshape(1, c))


def _outproj_kernel(a_ref, c_ref, wa_ref, wc_ref, x_ref, gt_ref, o_ref):
    acc = jnp.dot(a_ref[0], wa_ref[...], preferred_element_type=F32)
    acc = acc + jnp.dot(c_ref[0], wc_ref[...], preferred_element_type=F32)
    o_ref[0] = x_ref[0] + gt_ref[0] * acc


def _outproj(att3, conv3, wo_a, wo_c, x3, gt3, tm, tn):
    g, r, d = x3.shape
    ka, kc = att3.shape[-1], conv3.shape[-1]
    gt_spec = (pl.BlockSpec((1, 1, tn), lambda a, i, j: (a, 0, j)) if gt3.shape[1] == 1
               else pl.BlockSpec((1, tm, tn), lambda a, i, j: (a, i, j)))
    return pl.pallas_call(
        _outproj_kernel,
        grid=(g, r // tm, d // tn),
        in_specs=[pl.BlockSpec((1, tm, ka), lambda a, i, j: (a, i, 0)),
                  pl.BlockSpec((1, tm, kc), lambda a, i, j: (a, i, 0)),
                  pl.BlockSpec((ka, tn), lambda a, i, j: (0, j)),
                  pl.BlockSpec((kc, tn), lambda a, i, j: (0, j)),
                  pl.BlockSpec((1, tm, tn), lambda a, i, j: (a, i, j)),
                  gt_spec],
        out_specs=pl.BlockSpec((1, tm, tn), lambda a, i, j: (a, i, j)),
        out_shape=jax.ShapeDtypeStruct((g, r, d), F32),
        compiler_params=_cparams(("arbitrary", "arbitrary", "arbitrary")),
        name="outproj_residual",
    )(att3, conv3, wo_a, wo_c, x3, gt3)


def _peer_scores_kernel(ht_ref, wqt_ref, sk_ref, o_ref):
    q_t = jnp.dot(wqt_ref[...], ht_ref[...], preferred_element_type=F32).astype(BF16)
    half = q_t.shape[0] // 2
    o_ref[0:PEER_NKEYS, :] = jnp.dot(sk_ref[0, 0], q_t[:half], preferred_element_type=F32)
    o_ref[PEER_NKEYS:2 * PEER_NKEYS, :] = jnp.dot(sk_ref[0, 1], q_t[half:], preferred_element_type=F32)


def _peer_scores(h2t, wq_t, sk, tm):
    d, n = h2t.shape
    dk = wq_t.shape[0] // PEER_HEADS
    return pl.pallas_call(
        _peer_scores_kernel,
        grid=(n // tm, PEER_HEADS),
        in_specs=[pl.BlockSpec((d, tm), lambda i, h: (0, i)),
                  pl.BlockSpec((dk, d), lambda i, h: (h, 0)),
                  pl.BlockSpec((1, 2, PEER_NKEYS, dk // 2), lambda i, h: (h, 0, 0, 0))],
        out_specs=pl.BlockSpec((2 * PEER_NKEYS, tm), lambda i, h: (h, i)),
        out_shape=jax.ShapeDtypeStruct((PEER_HEADS * 2 * PEER_NKEYS, n), F32),
        compiler_params=_cparams(("arbitrary", "arbitrary")),
        name="peer_subkey_scores",
    )(h2t, wq_t, sk)


def _top16(x):
    rows = x.shape[0]
    rid = lax.broadcasted_iota(I32, x.shape, 0).astype(F32)
    vals = []
    for k in range(PEER_TOPK):
        m = jnp.max(x, axis=0, keepdims=True)
        vals.append(m)
        if k + 1 < PEER_TOPK:
            first = jnp.min(jnp.where(x == m, rid, float(rows)), axis=0, keepdims=True)
            x = jnp.where(rid == first, -jnp.inf, x)
    return vals


def _stack_rows(rows):
    shape = (len(rows), rows[0].shape[1])
    rid = lax.broadcasted_iota(I32, shape, 0)
    out = jnp.zeros(shape, rows[0].dtype)
    for k, row in enumerate(rows):
        out = jnp.where(rid == k, row, out)
    return out


def _peer_route_kernel(s_ref, s1m_ref, s2m_ref, c1_ref, e2_ref, thr_ref):
    s1 = s_ref[0:PEER_NKEYS, :]
    s2 = s_ref[PEER_NKEYS:2 * PEER_NKEYS, :]
    t1 = _top16(s1)
    t2 = _top16(s2)
    t2_all = _stack_rows(t2)
    blocks = [t1[0] + t2_all]
    for a in range(1, 8):
        blocks.append(t1[a] + t2_all[0:8])
    blocks.append(_stack_rows(t1[8:16]) + t2[0])
    top = _top16(jnp.concatenate(blocks, axis=0))
    z = jnp.ones_like(top[0])
    for k in range(1, PEER_TOPK):
        z = z + jnp.exp(top[k] - top[0])
    s1m_ref[...] = jnp.where(s1 >= t1[PEER_TOPK - 1], s1, -jnp.inf)
    s2m_ref[...] = jnp.where(s2 >= t2[PEER_TOPK - 1], s2, -jnp.inf)
    c1_ref[...] = jnp.exp(s1 - t1[0]) / z
    e2_ref[...] = jnp.exp(s2 - t2[0])
    thr_ref[0] = top[PEER_TOPK - 1]


def _peer_route(s_t, tl):
    rows, n = s_t.shape
    big = lambda: pl.BlockSpec((PEER_NKEYS, tl), lambda h, t: (h, t))
    big_shape = jax.ShapeDtypeStruct((PEER_HEADS * PEER_NKEYS, n), F32)
    return pl.pallas_call(
        _peer_route_kernel,
        grid=(PEER_HEADS, n // tl),
        in_specs=[pl.BlockSpec((2 * PEER_NKEYS, tl), lambda h, t: (h, t))],
        out_specs=[big(), big(), big(), big(), pl.BlockSpec((1, 1, tl), lambda h, t: (h, 0, t))],
        out_shape=[big_shape, big_shape, big_shape, big_shape,
                   jax.ShapeDtypeStruct((PEER_HEADS, 1, n), F32)],
        compiler_params=_cparams(("arbitrary", "arbitrary")),
        name="peer_route",
    )(s_t)


GATE_ROWS = 64


MXU_COLS = 256


def _peer_dense_kernel(ht_ref, u_ref, vt_ref, s1_ref, c1_ref, s2_ref, e2_ref, thr_ref, o_ref, *, te):
    e = pl.program_id(1)
    tm = ht_ref.shape[1]
    rows_per_tile = te // PEER_NKEYS

    @pl.when(e == 0)
    def _():
        o_ref[...] = jnp.zeros(o_ref.shape, F32)

    chunk = min(MXU_COLS, tm)
    a_chunks = [jnp.dot(u_ref[...], ht_ref[:, c * chunk:(c + 1) * chunk], preferred_element_type=F32)
                for c in range(tm // chunk)]

    s1rows = [[s1_ref[pl.ds(h * PEER_NKEYS + e * rows_per_tile + r, 1), :] for r in range(rows_per_tile)]
              for h in range(PEER_HEADS)]
    c1rows = [[c1_ref[pl.ds(h * PEER_NKEYS + e * rows_per_tile + r, 1), :] for r in range(rows_per_tile)]
              for h in range(PEER_HEADS)]
    for c in range(tm // chunk):
        w_cols = []
        for lc in range(chunk // LANES):
            ls = slice(c * chunk + lc * LANES, c * chunk + (lc + 1) * LANES)
            als = slice(lc * LANES, (lc + 1) * LANES)
            blocks = [[None] * (PEER_NKEYS // GATE_ROWS) for _ in range(rows_per_tile)]
            for part in range(PEER_NKEYS // GATE_ROWS):
                rs = [slice(r * PEER_NKEYS + part * GATE_ROWS, r * PEER_NKEYS + (part + 1) * GATE_ROWS)
                      for r in range(rows_per_tile)]
                acc = [jnp.zeros((GATE_ROWS, LANES), F32) for _ in range(rows_per_tile)]
                for h in range(PEER_HEADS):
                    row0 = h * PEER_NKEYS + part * GATE_ROWS
                    s2 = s2_ref[row0:row0 + GATE_ROWS, ls]
                    e2 = e2_ref[row0:row0 + GATE_ROWS, ls]
                    thr = thr_ref[h, :, ls]
                    for r in range(rows_per_tile):
                        cand = s2 + s1rows[h][r][:, ls]
                        acc[r] = acc[r] + jnp.where(cand >= thr, e2, 0.0) * c1rows[h][r][:, ls]
                for r in range(rows_per_tile):
                    blocks[r][part] = (acc[r] * jax.nn.gelu(a_chunks[c][rs[r], als])).astype(BF16)
            w_cols.append(jnp.concatenate([b for row in blocks for b in row], axis=0))
        cs = slice(c * chunk, (c + 1) * chunk)
        o_ref[:, cs] += jnp.dot(vt_ref[...], jnp.concatenate(w_cols, axis=1), preferred_element_type=F32)


def _peer_dense(h2t, u_bf, vt_bf, s1m, c1, s2m, e2, thr, tm, te):
    d, n = h2t.shape
    rows = PEER_HEADS * PEER_NKEYS
    once = pl.Buffered(1)
    tok = lambda: pl.BlockSpec((rows, tm), lambda i, e: (0, i), pipeline_mode=once)
    return pl.pallas_call(
        functools.partial(_peer_dense_kernel, te=te),
        grid=(n // tm, u_bf.shape[0] // te),
        in_specs=[pl.BlockSpec((d, tm), lambda i, e: (0, i), pipeline_mode=once),
                  pl.BlockSpec((te, d), lambda i, e: (e, 0)),
                  pl.BlockSpec((d, te), lambda i, e: (0, e)),
                  tok(), tok(), tok(), tok(),
                  pl.BlockSpec((PEER_HEADS, 1, tm), lambda i, e: (0, 0, i), pipeline_mode=once)],
        out_specs=pl.BlockSpec((d, tm), lambda i, e: (0, i), pipeline_mode=once),
        out_shape=jax.ShapeDtypeStruct((d, n), F32),
        compiler_params=_cparams(("arbitrary", "arbitrary"), VMEM_LIMIT_PEER),
        name="peer_dense_experts",
    )(h2t, u_bf, vt_bf, s1m, c1, s2m, e2, thr)


def _final_kernel(x_ref, gt_ref, p_ref, g_ref, o_ref, *, normalize):
    xx = x_ref[0] + gt_ref[0] * p_ref[...].T
    if normalize:
        ms = jnp.mean(xx * xx, axis=-1, keepdims=True)
        xx = xx * lax.rsqrt(ms + EPS) * g_ref[...]
    o_ref[0] = xx


def _final(x3, gt3, peer_t, g_final, tm, normalize):
    g, r, d = x3.shape
    nb = r // tm
    return pl.pallas_call(
        functools.partial(_final_kernel, normalize=normalize),
        grid=(g, nb),
        in_specs=[pl.BlockSpec((1, tm, d), lambda a, i: (a, i, 0)),
                  _row_mod_spec(gt3, tm),
                  pl.BlockSpec((d, tm), lambda a, i: (0, a * nb + i)),
                  pl.BlockSpec((1, d), lambda a, i: (0, 0))],
        out_specs=pl.BlockSpec((1, tm, d), lambda a, i: (a, i, 0)),
        out_shape=jax.ShapeDtypeStruct((g, r, d), F32),
        compiler_params=_cparams(("arbitrary", "arbitrary")),
        name="final_norm",
    )(x3, gt3, peer_t, g_final.reshape(1, d))


def _sample_index_kernel(pt_ref, qi_ref, wi_ref, kself_ref, expand_ref, cki_ref, mask_ref, self_ref,
                         kbuf_ref, sem_ref, sc_ref, scself_ref, *, n_pages, topk, page_base):
    b = pl.program_id(0)
    slot = b % 2
    score_scale = IDX_DIM ** -0.5 * IDX_HEADS ** -0.5

    def page_copy(seq, k, sl):
        page = page_base + pt_ref[seq, k]
        return pltpu.make_async_copy(cki_ref.at[page], kbuf_ref.at[sl, :, pl.ds(k * PAGE_SIZE, PAGE_SIZE)],
                                     sem_ref.at[sl])

    @pl.when(b == 0)
    def _():
        for k in range(n_pages):
            page_copy(0, k, 0).start()

    @pl.when(b + 1 < pl.num_programs(0))
    def _():
        for k in range(n_pages):
            page_copy(b + 1, k, 1 - slot).start()

    for k in range(n_pages):
        page_copy(b, k, slot).wait()

    qi = qi_ref[0]
    wi = wi_ref[0]
    s = jnp.dot(qi, kbuf_ref[slot].astype(BF16), preferred_element_type=F32)
    sc_ref[pl.ds(b, 1), :] = jnp.sum(jnp.maximum(s, 0.0) * wi, axis=0, keepdims=True) * score_scale
    ks = kself_ref[0].astype(BF16).astype(F32)
    s_self = jnp.sum(qi.astype(F32) * ks, axis=1, keepdims=True)
    self_score = jnp.sum(jnp.maximum(s_self, 0.0) * wi, axis=0, keepdims=True) * score_scale
    scself_ref[pl.ds(b, 1), :] = jnp.zeros((1, LANES), F32) + self_score

    @pl.when(b == pl.num_programs(0) - 1)
    def _():
        keys = _sortable(sc_ref[...])
        kself = _sortable(scself_ref[:, 0:1])
        pos = lax.broadcasted_iota(I32, keys.shape, 1)
        self_pos = n_pages * PAGE_SIZE

        def total(x, xs):
            return jnp.sum(x, axis=1, keepdims=True) + xs

        def bit_body(t, thr):
            cand = thr ^ lax.shift_left(jnp.int32(1), 31 - t)
            cnt = total(jnp.where(keys >= cand, 1, 0), jnp.where(kself >= cand, 1, 0))
            return jnp.where(cnt >= topk, cand, thr)

        thr = lax.fori_loop(0, 32, bit_body, jnp.full(kself.shape, INT_MIN, I32))
        need = topk - total(jnp.where(keys > thr, 1, 0), jnp.where(kself > thr, 1, 0))

        def idx_body(t, c):
            cand = c | lax.shift_left(jnp.int32(1), 15 - t)
            f = total(jnp.where(keys == thr, jnp.where(pos < cand, 1, 0), 0),
                      jnp.where(kself == thr, jnp.where(self_pos < cand, 1, 0), 0))
            return jnp.where(f < need, cand, c)

        cstar = lax.fori_loop(0, 16, idx_body, jnp.zeros(kself.shape, I32))
        picked = jnp.where(keys > thr, 1.0, jnp.where(keys == thr, jnp.where(pos <= cstar, 1.0, 0.0), 0.0))
        for k in range(n_pages):
            flags = picked[:, k * PAGE_SIZE:(k + 1) * PAGE_SIZE].astype(BF16)
            rows = jnp.dot(flags, expand_ref[...], preferred_element_type=F32)
            mask_ref[k] = jnp.where(rows > 0.5, 0.0, NEG)
        ssel = jnp.where(kself > thr, 0.0, jnp.where(kself == thr, jnp.where(self_pos <= cstar, 0.0, NEG), NEG))
        self_ref[...] = jnp.zeros(self_ref.shape, F32) + ssel


def _sample_index(page_table, qi3, wi3, kself3, cki_t, page_base):
    db, n_pages = page_table.shape
    topk = min(TOPK_MAX, (n_pages * PAGE_SIZE + 1) // 4)
    page_rows = PAGE_SIZE * N_KV_HEADS
    expand = (jnp.arange(page_rows, dtype=I32)[None, :] // N_KV_HEADS
              == jnp.arange(PAGE_SIZE, dtype=I32)[:, None]).astype(BF16)
    grid_spec = pltpu.PrefetchScalarGridSpec(
        num_scalar_prefetch=1,
        grid=(db,),
        in_specs=[pl.BlockSpec((1, IDX_HEADS, IDX_DIM), lambda b, pt: (b, 0, 0)),
                  pl.BlockSpec((1, IDX_HEADS, 1), lambda b, pt: (b, 0, 0)),
                  pl.BlockSpec((1, 1, IDX_DIM), lambda b, pt: (b, 0, 0)),
                  pl.BlockSpec((PAGE_SIZE, page_rows), lambda b, pt: (0, 0)),
                  pl.BlockSpec(memory_space=pl.ANY)],
        out_specs=[pl.BlockSpec((n_pages, db, page_rows), lambda b, pt: (0, 0, 0)),
                   pl.BlockSpec((db, LANES), lambda b, pt: (0, 0))],
        scratch_shapes=[pltpu.VMEM((2, IDX_DIM, n_pages * PAGE_SIZE), F32),
                        pltpu.SemaphoreType.DMA((2,)),
                        pltpu.VMEM((db, n_pages * PAGE_SIZE), F32),
                        pltpu.VMEM((db, LANES), F32)],
    )
    mask_t, selfsel = pl.pallas_call(
        functools.partial(_sample_index_kernel, n_pages=n_pages, topk=topk, page_base=page_base),
        grid_spec=grid_spec,
        out_shape=[jax.ShapeDtypeStruct((n_pages, db, page_rows), F32),
                   jax.ShapeDtypeStruct((db, LANES), F32)],
        compiler_params=_cparams(("arbitrary",)),
        name="sample_indexer",
    )(page_table, qi3, wi3, kself3, expand, cki_t)
    return mask_t.transpose(1, 0, 2), jnp.broadcast_to(selfsel[:, None, :], (db, SUBLANES, LANES))


def _sample_attn_kernel(pt_ref, q_ref, kself_ref, vself_ref, mask_ref, self_ref, bias_ref, bself_ref,
                        *rest, n_pages):
    pps = ATTN_PAGES_PER_STEP
    kpages = rest[:pps]
    vpages = rest[pps:2 * pps]
    o_ref, acc_ref, mx_ref, l_ref = rest[2 * pps:]
    step = pl.program_id(1)
    n_steps = n_pages // pps
    page_rows = PAGE_SIZE * N_KV_HEADS
    sm_scale = HEAD_DIM ** -0.5
    q = q_ref[0]
    head_group = lax.broadcasted_iota(I32, (N_HEADS, HEAD_DIM), 0) // KV_GROUP

    def own_group_rows(x_ref):
        out = jnp.zeros((N_HEADS, HEAD_DIM), F32)
        for g in range(N_KV_HEADS):
            out = jnp.where(head_group == g, x_ref[0, g:g + 1, :].astype(BF16).astype(F32), out)
        return out

    @pl.when(step == 0)
    def _():
        logit = jnp.sum(q.astype(F32) * own_group_rows(kself_ref), axis=1, keepdims=True)
        mx_ref[...] = logit * sm_scale + bself_ref[...] + self_ref[0, 0:1, 0:1]
        l_ref[...] = jnp.ones(l_ref.shape, F32)
        acc_ref[...] = own_group_rows(vself_ref)

    kcat = jnp.concatenate([kpages[k][0].astype(BF16) for k in range(pps)], axis=0)
    vcat = jnp.concatenate([vpages[k][0].astype(BF16) for k in range(pps)], axis=0)
    s = _dot_nt(q, kcat) * sm_scale + bias_ref[step]
    s = jnp.concatenate([s[:, k * page_rows:(k + 1) * page_rows] + mask_ref[0, pl.ds(step * pps + k, 1), :]
                         for k in range(pps)], axis=1)
    m_old = mx_ref[...]
    m_new = jnp.maximum(m_old, jnp.max(s, axis=1, keepdims=True))
    alpha = jnp.exp(m_old - m_new)
    p = jnp.exp(s - m_new)
    l_ref[...] = alpha * l_ref[...] + jnp.sum(p, axis=1, keepdims=True)
    acc_ref[...] = alpha * acc_ref[...] + jnp.dot(p.astype(BF16), vcat, preferred_element_type=F32)
    mx_ref[...] = m_new

    @pl.when(step == n_steps - 1)
    def _():
        o_ref[0] = (acc_ref[...] / l_ref[...]).astype(o_ref.dtype)


def _sample_attn(page_table, q3, kself3, vself3, mask, selfsel, bias_steps, bias_self, ck, cv, page_base):
    db, n_pages = page_table.shape
    pps = ATTN_PAGES_PER_STEP
    n_steps = n_pages // pps
    page_rows = PAGE_SIZE * N_KV_HEADS
    step_rows = pps * page_rows

    def page_spec(k):
        return pl.BlockSpec((1, page_rows, HEAD_DIM),
                            lambda b, s, pt: (page_base + pt[b, s * pps + k], 0, 0))

    per_b = lambda shape: pl.BlockSpec((1,) + shape, lambda b, s, pt: (b, 0, 0))
    grid_spec = pltpu.PrefetchScalarGridSpec(
        num_scalar_prefetch=1,
        grid=(db, n_steps),
        in_specs=[per_b((N_HEADS, HEAD_DIM)), per_b((N_KV_HEADS, HEAD_DIM)), per_b((N_KV_HEADS, HEAD_DIM)),
                  per_b((n_pages, page_rows)), per_b((SUBLANES, LANES)),
                  pl.BlockSpec((n_steps, N_HEADS, step_rows), lambda b, s, pt: (0, 0, 0)),
                  pl.BlockSpec((N_HEADS, 1), lambda b, s, pt: (0, 0))]
                 + [page_spec(k) for k in range(pps)] * 2,
        out_specs=per_b((N_HEADS, HEAD_DIM)),
        scratch_shapes=[pltpu.VMEM((N_HEADS, HEAD_DIM), F32),
                        pltpu.VMEM((N_HEADS, 1), F32),
                        pltpu.VMEM((N_HEADS, 1), F32)],
    )
    return pl.pallas_call(
        functools.partial(_sample_attn_kernel, n_pages=n_pages),
        grid_spec=grid_spec,
        out_shape=jax.ShapeDtypeStruct((db, N_HEADS, HEAD_DIM), BF16),
        compiler_params=_cparams(("arbitrary", "arbitrary")),
        name="sample_attention",
    )(page_table, q3, kself3, vself3, mask, selfsel, bias_steps, bias_self,
      *([ck] * pps), *([cv] * pps))


def _rel_bucket(dist):
    n = jnp.maximum(dist, 0)
    max_exact = REL_BUCKETS // 2
    nf = jnp.maximum(n, 1).astype(F32)
    large = max_exact + (jnp.log(nf / max_exact) / math.log(REL_MAX_DIST / max_exact)
                         * (REL_BUCKETS - max_exact)).astype(I32)
    large = jnp.minimum(large, REL_BUCKETS - 1)
    return jnp.where(n < max_exact, n, large)


def _bias_of_dist(rel_bias, dist):
    onehot = (_rel_bucket(dist)[..., None] == jnp.arange(REL_BUCKETS, dtype=I32)).astype(F32)
    return jnp.einsum("...b,bh->...h", onehot, rel_bias.astype(F32), precision=lax.Precision.HIGHEST)


def _prompt_bias_tiles(rel_bias):
    kc = jnp.arange(LANES, dtype=I32)[:, None]
    qr = jnp.arange(LANES, dtype=I32)[None, :]
    far = rel_bias[REL_BUCKETS - 1]
    tiles = []
    for off in (0, LANES):
        dist = off + qr - kc
        t = (_bias_of_dist(rel_bias, dist) - far) * HEAD_DIM ** 0.5
        tiles.append(jnp.where((dist >= 0)[..., None], t, 0.0).transpose(2, 0, 1))
    return jnp.stack(tiles).astype(F32)


def _sample_bias_steps(rel_bias, n_pages):
    past = n_pages * PAGE_SIZE
    bias = _bias_of_dist(rel_bias, past - jnp.arange(past, dtype=I32))
    own = (jnp.arange(N_HEADS, dtype=I32)[:, None] // KV_GROUP) == jnp.arange(N_KV_HEADS, dtype=I32)[None, :]
    rows = jnp.where(own[None], bias[:, :, None], NEG)
    rows = rows.transpose(1, 0, 2).reshape(N_HEADS, n_pages // ATTN_PAGES_PER_STEP, -1)
    return rows.transpose(1, 0, 2).astype(F32)


def _split_in_proj(w_in_l):
    sizes = (ATT_WIDTH, KV_WIDTH, KV_WIDTH, IDX_HEADS * IDX_DIM, IDX_DIM, IDX_HEADS)
    offs = [0]
    for s in sizes:
        offs.append(offs[-1] + s)
    conv_ch = (w_in_l.shape[1] - offs[-1]) // 2
    d = w_in_l.shape[0]
    wq, wk, wv, wqi, wki, wwi = (w_in_l[:, offs[i]:offs[i + 1]] for i in range(6))
    wua = w_in_l[:, offs[-1]:offs[-1] + conv_ch]
    wub = w_in_l[:, offs[-1] + conv_ch:]
    z = lambda n: jnp.zeros((d, n), w_in_l.dtype)
    w_a = jnp.concatenate([wk, wv, wki, z(LANES - IDX_DIM), z(LANES - IDX_DIM), wki,
                           wwi, z(LANES - IDX_HEADS), z(LANES)], axis=1)
    w_b = jnp.concatenate([wq, wqi], axis=1)
    chunk = 256
    parts = []
    for c in range(conv_ch // chunk):
        parts += [wua[:, c * chunk:(c + 1) * chunk], wub[:, c * chunk:(c + 1) * chunk]]
    w_c = jnp.concatenate(parts, axis=1)
    return w_a.astype(BF16), w_b.astype(BF16), w_c.astype(BF16)


COL_K, COL_V, COL_KA, COL_WI = 0, KV_WIDTH, 2 * KV_WIDTH, 2 * KV_WIDTH + 2 * LANES


def _mixer_projections(h, w_a, w_b, w_c, tm):
    z_a = _matmul(h, w_a, F32, tm, 512)
    qh = _matmul(h, w_b, BF16, tm, 512, head_major=True)
    u = _matmul(h, w_c, F32, tm, 512, glu=True)
    return z_a, qh, u


ATTN_TILE = 256
PROJ_TM, PROJ_TN = 1024, 512
NORM_ROWS = 512
FINAL_ROWS = 256
CONV_TT, CONV_TC = 256, 512
PEER_TM, PEER_TE = 512, 1024
PEER_TE_DECODE = 512
ROUTE_LANES = 256


def _peer_block(h2t, wq_t, sk_bf, u_bf, vt_bf, tm, te):
    s_t = _peer_scores(h2t, wq_t, sk_bf, tm)
    s1m, s2m, c1, e2, thr = _peer_route(s_t, min(tm, ROUTE_LANES))
    return _peer_dense(h2t, u_bf, vt_bf, s1m, c1, s2m, e2, thr, tm, te)


def kernel(x_prompt, x_sample, cache_k, cache_v, cache_kidx, state_conv, page_table, c_prompt, c_sample,
           rel_bias, w_ada, b_ada, g_mix, w_in, conv_w, conv_b, cn_g, cn_b, w_o, g_ch, peer_wq,
           peer_subkeys, peer_u, peer_v, g_final):
    batch, seq, d = x_prompt.shape
    db = x_sample.shape[0]
    depth = w_ada.shape[0]
    n_pages = page_table.shape[1]
    conv_ch = conv_w.shape[-1]
    n_prompt = batch * seq
    tq = tk = ATTN_TILE

    xp = x_prompt
    xs = jnp.pad(x_sample.reshape(1, db, d), ((0, 0), (0, SAMPLE_ROWS - db), (0, 0)))
    c_rows = batch + db
    c_pad = (-c_rows) % 16
    c_all = jnp.pad(jnp.concatenate([c_prompt, c_sample], axis=0), ((0, c_pad), (0, 0)))
    bias_t = _prompt_bias_tiles(rel_bias)
    bias_steps = _sample_bias_steps(rel_bias, n_pages)
    n_pool = cache_k.shape[1]
    page_rows = PAGE_SIZE * N_KV_HEADS
    ck_rows = cache_k.reshape(depth * n_pool, page_rows, HEAD_DIM)
    cv_rows = cache_v.reshape(depth * n_pool, page_rows, HEAD_DIM)
    cki_t = jnp.swapaxes(cache_kidx, 2, 3).reshape(depth * n_pool, IDX_DIM, PAGE_SIZE)
    bias_self = rel_bias[_rel_bucket(jnp.zeros((), I32))].reshape(N_HEADS, 1)

    outs = {k: [] for k in ("kp", "vp", "kip", "cp", "ks", "vs", "kis", "cs")}
    for l in range(depth):
        mods = _adaln(c_all, w_ada[l], b_ada[l])
        p_mod = [m.reshape(batch, 1, d) for m in jnp.split(mods[:batch], 6, axis=-1)]
        s_mod = [jnp.pad(m.reshape(1, db, d), ((0, 0), (0, SAMPLE_ROWS - db), (0, 0)))
                 for m in jnp.split(mods[batch:c_rows], 6, axis=-1)]
        w_a, w_b, w_c = _split_in_proj(w_in[l])
        wo_a = w_o[l][:ATT_WIDTH].astype(BF16)
        wo_c = w_o[l][ATT_WIDTH:].astype(BF16)

        hp = _modulate(xp, g_mix[l], p_mod[1], p_mod[0], NORM_ROWS).reshape(n_prompt, d)
        z_a, qh, u = _mixer_projections(hp, w_a, w_b, w_c, PROJ_TM)
        kvb = z_a[:, :COL_WI].astype(BF16)
        wi_t = z_a[:, COL_WI:COL_WI + IDX_HEADS].T
        vt_tiles = (kvb[:, COL_V:COL_V + KV_WIDTH].reshape(batch, seq // tk, tk, KV_WIDTH)
                    .transpose(0, 1, 3, 2))
        mask = _indexer(qh, kvb, wi_t, batch, seq, tq, tk, qi_blk=N_HEADS // (IDX_HEADS // 2),
                        ka_blk=COL_KA // LANES)
        att = _attention(qh, kvb, vt_tiles, mask, bias_t, batch, seq, tq, tk)
        u3 = u.reshape(batch, seq, conv_ch)
        conv = _conv(u3, u3, conv_w[l], conv_b[l], cn_g[l], cn_b[l], CONV_TT, CONV_TC, zero_first=True)
        xp = _outproj(att.reshape(batch, seq, ATT_WIDTH), conv, wo_a, wo_c, xp, p_mod[2], PROJ_TM, PROJ_TN)
        outs["kp"].append(z_a[:, COL_K:COL_K + KV_WIDTH].reshape(batch, seq, N_KV_HEADS, HEAD_DIM))
        outs["vp"].append(z_a[:, COL_V:COL_V + KV_WIDTH].reshape(batch, seq, N_KV_HEADS, HEAD_DIM))
        outs["kip"].append(z_a[:, COL_KA:COL_KA + IDX_DIM].reshape(batch, seq, IDX_DIM))
        outs["cp"].append(u3[:, seq - (CONV_W - 1):])

        hs = _modulate(xs, g_mix[l], s_mod[1], s_mod[0], SAMPLE_ROWS).reshape(SAMPLE_ROWS, d)
        zs_a, qhs, us = _mixer_projections(hs, w_a, w_b, w_c, SAMPLE_ROWS)
        k_new = zs_a[:db, COL_K:COL_K + KV_WIDTH]
        v_new = zs_a[:db, COL_V:COL_V + KV_WIDTH]
        ki_new = zs_a[:db, COL_KA:COL_KA + IDX_DIM]
        wi_new = zs_a[:db, COL_WI:COL_WI + IDX_HEADS]
        q_s = qhs[:N_HEADS, :db].transpose(1, 0, 2)
        qi_s = (qhs[N_HEADS:, :db].transpose(1, 0, 2)
                .reshape(db, IDX_HEADS // 2, 2, IDX_DIM).reshape(db, IDX_HEADS, IDX_DIM))
        smask, sself = _sample_index(page_table, qi_s, wi_new.reshape(db, IDX_HEADS, 1),
                                     ki_new.reshape(db, 1, IDX_DIM), cki_t, l * n_pool)
        att_s = _sample_attn(page_table, q_s, k_new.reshape(db, N_KV_HEADS, HEAD_DIM),
                             v_new.reshape(db, N_KV_HEADS, HEAD_DIM), smask, sself, bias_steps, bias_self,
                             ck_rows, cv_rows, l * n_pool)
        att_s = jnp.pad(att_s.reshape(1, db, ATT_WIDTH), ((0, 0), (0, SAMPLE_ROWS - db), (0, 0)))
        u_new = us[:db]
        state = state_conv[l].astype(F32)
        halo = jnp.pad(state, ((0, 0), (CONV_HALO - (CONV_W - 1), 0), (0, 0)))
        cur = jnp.pad(u_new.reshape(db, 1, conv_ch), ((0, 0), (0, SUBLANES - 1), (0, 0)))
        conv_s = _conv(halo, cur, conv_w[l], conv_b[l], cn_g[l], cn_b[l], SUBLANES, CONV_TC, zero_first=False)
        conv_s = jnp.pad(conv_s[:, 0].reshape(1, db, conv_ch), ((0, 0), (0, SAMPLE_ROWS - db), (0, 0)))
        xs = _outproj(att_s, conv_s, wo_a, wo_c, xs, s_mod[2], SAMPLE_ROWS, PROJ_TN)
        outs["ks"].append(k_new.reshape(db, 1, N_KV_HEADS, HEAD_DIM))
        outs["vs"].append(v_new.reshape(db, 1, N_KV_HEADS, HEAD_DIM))
        outs["kis"].append(ki_new.reshape(db, 1, IDX_DIM))
        outs["cs"].append(jnp.concatenate([state[:, 1:], u_new.reshape(db, 1, conv_ch)], axis=1))

        wq_t = peer_wq[l].astype(BF16).T
        sk_bf = peer_subkeys[l].astype(BF16)
        u_bf = peer_u[l].astype(BF16)
        vt_bf = peer_v[l].astype(BF16).T
        hp2 = _modulate(xp, g_ch[l], p_mod[4], p_mod[3], NORM_ROWS, transposed=True)
        peer_p = _peer_block(hp2, wq_t, sk_bf, u_bf, vt_bf, PEER_TM, PEER_TE)
        hs2 = _modulate(xs, g_ch[l], s_mod[4], s_mod[3], SAMPLE_ROWS, transposed=True)
        peer_s = _peer_block(hs2, wq_t, sk_bf, u_bf, vt_bf, SAMPLE_ROWS, PEER_TE_DECODE)
        last = l == depth - 1
        xp = _final(xp, p_mod[5], peer_p, g_final, FINAL_ROWS, normalize=last)
        xs = _final(xs, s_mod[5], peer_s, g_final, SAMPLE_ROWS, normalize=last)

    st = lambda k: jnp.stack(outs[k])
    y_sample = xs[0, :db].reshape(db, 1, d)
    return (xp, y_sample, st("kp"), st("vp"), st("kip"), st("cp"),
            st("ks"), st("vs"), st("kis"), st("cs"))
```

```python
import functools
import math

import jax
import jax.numpy as jnp
from jax import lax
from jax.experimental import pallas as pl
from jax.experimental.pallas import tpu as pltpu

F32 = jnp.float32
BF16 = jnp.bfloat16
I32 = jnp.int32

HEAD_DIM = 128
N_HEADS = 16
N_KV_HEADS = 4
KV_GROUP = N_HEADS // N_KV_HEADS
ATT_WIDTH = N_HEADS * HEAD_DIM
KV_WIDTH = N_KV_HEADS * HEAD_DIM
IDX_HEADS = 16
IDX_DIM = 64
TOPK_MAX = 256
REL_BUCKETS = 32
REL_MAX_DIST = 128
CONV_W = 31
PEER_HEADS = 8
PEER_NKEYS = 128
PEER_TOPK = 16
EPS = 1e-6
PAGE_SIZE = 128

LANES = 128
SUBLANES = 8
VMEM_LIMIT = 56 * 1024 * 1024
VMEM_LIMIT_PEER = 63 * 1024 * 1024

NEG = -1e30
INT_MIN = -(2 ** 31)
CONV_HALO = 32
SAMPLE_ROWS = 128
ATTN_PAGES_PER_STEP = 32


def _cparams(sem, vmem_limit=VMEM_LIMIT):
    return pltpu.CompilerParams(dimension_semantics=sem, vmem_limit_bytes=vmem_limit)


def _dot_nt(a, b):
    return lax.dot_general(a, b, (((1,), (1,)), ((), ())), preferred_element_type=F32)


def _sortable(x):
    bits = pltpu.bitcast(x, I32)
    return bits ^ ((bits >> 31) & jnp.int32(0x7FFFFFFF))


def _adaln_kernel(c_ref, w_ref, b_ref, o_ref):
    c = c_ref[...]
    a = (c * jax.nn.sigmoid(c)).astype(BF16)
    o_ref[...] = jnp.dot(a, w_ref[...].astype(BF16), preferred_element_type=F32) + b_ref[...]


def _adaln(c, w_ada, b_ada, tn=512):
    r, d = c.shape
    n = w_ada.shape[1]
    return pl.pallas_call(
        _adaln_kernel,
        grid=(n // tn,),
        in_specs=[pl.BlockSpec((r, d), lambda j: (0, 0)),
                  pl.BlockSpec((d, tn), lambda j: (0, j)),
                  pl.BlockSpec((1, tn), lambda j: (0, j))],
        out_specs=pl.BlockSpec((r, tn), lambda j: (0, j)),
        out_shape=jax.ShapeDtypeStruct((r, n), F32),
        compiler_params=_cparams(("arbitrary",)),
        name="adaln",
    )(c, w_ada, b_ada.reshape(1, n))


def _modulate_kernel(x_ref, g_ref, sc_ref, sh_ref, o_ref, *, transposed):
    x = x_ref[0]
    ms = jnp.mean(x * x, axis=-1, keepdims=True)
    y = x * lax.rsqrt(ms + EPS) * g_ref[...]
    y = y * (1.0 + sc_ref[0]) + sh_ref[0]
    if transposed:
        o_ref[...] = y.T.astype(o_ref.dtype)
    else:
        o_ref[0] = y.astype(o_ref.dtype)


def _row_mod_spec(mod, tr):
    d = mod.shape[-1]
    if mod.shape[1] == 1:
        return pl.BlockSpec((1, 1, d), lambda g, r, *_: (g, 0, 0))
    return pl.BlockSpec((1, tr, d), lambda g, r, *_: (g, r, 0))


def _modulate(x3, gain, sc3, sh3, tr, transposed=False):
    g, r, d = x3.shape
    nb = r // tr
    if transposed:
        out_spec = pl.BlockSpec((d, tr), lambda a, b: (0, a * nb + b))
        out_shape = jax.ShapeDtypeStruct((d, g * r), BF16)
    else:
        out_spec = pl.BlockSpec((1, tr, d), lambda a, b: (a, b, 0))
        out_shape = jax.ShapeDtypeStruct((g, r, d), BF16)
    return pl.pallas_call(
        functools.partial(_modulate_kernel, transposed=transposed),
        grid=(g, nb),
        in_specs=[pl.BlockSpec((1, tr, d), lambda a, b: (a, b, 0)),
                  pl.BlockSpec((1, d), lambda a, b: (0, 0)),
                  _row_mod_spec(sc3, tr),
                  _row_mod_spec(sh3, tr)],
        out_specs=out_spec,
        out_shape=out_shape,
        compiler_params=_cparams(("arbitrary", "arbitrary")),
        name="modulate",
    )(x3, gain.reshape(1, d), sc3, sh3)


def _mm_kernel(h_ref, w_ref, o_ref, *, glu, head_major):
    acc = jnp.dot(h_ref[...], w_ref[...], preferred_element_type=F32)
    if glu:
        half = acc.shape[1] // 2
        acc = acc[:, :half] * jax.nn.sigmoid(acc[:, half:])
    if head_major:
        for c in range(acc.shape[1] // LANES):
            o_ref[c] = acc[:, c * LANES:(c + 1) * LANES].astype(o_ref.dtype)
    else:
        o_ref[...] = acc.astype(o_ref.dtype)


def _matmul(h, w, out_dtype, tm, tn, glu=False, head_major=False):
    m, k = h.shape
    n = w.shape[1]
    n_out = n // 2 if glu else n
    tn_out = tn // 2 if glu else tn
    if head_major:
        out_shape = jax.ShapeDtypeStruct((n_out // LANES, m, LANES), out_dtype)
        out_spec = pl.BlockSpec((tn_out // LANES, tm, LANES), lambda i, j: (j, i, 0))
    else:
        out_shape = jax.ShapeDtypeStruct((m, n_out), out_dtype)
        out_spec = pl.BlockSpec((tm, tn_out), lambda i, j: (i, j))
    return pl.pallas_call(
        functools.partial(_mm_kernel, glu=glu, head_major=head_major),
        grid=(m // tm, n // tn),
        in_specs=[pl.BlockSpec((tm, k), lambda i, j: (i, 0)),
                  pl.BlockSpec((k, tn), lambda i, j: (0, j))],
        out_specs=out_spec,
        out_shape=out_shape,
        compiler_params=_cparams(("arbitrary", "arbitrary")),
        name="proj_matmul",
    )(h, w)


def _indexer_kernel(qi_ref, ka_ref, kb_ref, wi_ref, o_ref, keys_ref, cst_ref, *, tq, tk, nk, topk):
    i = pl.program_id(1)
    q0 = i * tq
    nvis = (q0 + tq + tk - 1) // tk
    qpos = q0 + lax.broadcasted_iota(I32, (1, tq), 1)
    w = wi_ref[...]
    score_scale = IDX_DIM ** -0.5 * IDX_HEADS ** -0.5

    def kpos_of(j):
        return j * tk + lax.broadcasted_iota(I32, (tk, 1), 0)

    def score_body(j, carry):
        k0 = pl.multiple_of(j * tk, tk)
        ka = ka_ref[pl.ds(k0, tk), :]
        kb = kb_ref[pl.ds(k0, tk), :]
        acc = jnp.zeros((tk, tq), F32)
        for p in range(IDX_HEADS // 2):
            qp = qi_ref[p]
            sa = _dot_nt(ka, qp)
            sb = _dot_nt(kb, qp)
            acc = acc + jnp.maximum(sa, 0.0) * w[2 * p:2 * p + 1]
            acc = acc + jnp.maximum(sb, 0.0) * w[2 * p + 1:2 * p + 2]
        acc = acc * score_scale
        acc = jnp.where(kpos_of(j) <= qpos, acc, -jnp.inf)
        keys_ref[j] = _sortable(acc)
        return carry

    lax.fori_loop(0, nvis, score_body, 0)

    def count(pred):
        def body(j, c):
            return c + jnp.sum(pred(keys_ref[j], j).reshape(tk // SUBLANES, SUBLANES, tq), axis=0)
        part = lax.fori_loop(0, nvis, body, jnp.zeros((SUBLANES, tq), I32))
        return jnp.sum(part, axis=0, keepdims=True)

    def bit_body(t, thr):
        cand = thr ^ lax.shift_left(jnp.int32(1), 31 - t)
        cnt = count(lambda k, j: jnp.where(k >= cand, 1, 0))
        return jnp.where(cnt >= topk, cand, thr)

    thr = lax.fori_loop(0, 32, bit_body, jnp.full((1, tq), INT_MIN, I32))

    need = topk - count(lambda k, j: jnp.where(k > thr, 1, 0))
    n_eq = count(lambda k, j: jnp.where(k == thr, 1, 0))
    cst_ref[...] = jnp.full((1, tq), nk * tk, I32)

    @pl.when(jnp.max(jnp.where(n_eq > need, 1, 0)) > 0)
    def _():
        def idx_body(t, c):
            cand = c | lax.shift_left(jnp.int32(1), 15 - t)
            f = count(lambda k, j: jnp.where(k == thr, jnp.where(kpos_of(j) < cand, 1, 0), 0))
            return jnp.where(f < need, cand, c)
        cst_ref[...] = lax.fori_loop(0, 16, idx_body, jnp.zeros((1, tq), I32))

    cstar = cst_ref[...]

    def write_body(j, carry):
        k = keys_ref[j]
        kpos = kpos_of(j)
        sel = jnp.where(k > thr, 0.0, jnp.where(k == thr, jnp.where(kpos <= cstar, 0.0, NEG), NEG))
        o_ref[0, j] = jnp.where(kpos <= qpos, sel, NEG).astype(o_ref.dtype)
        return carry

    lax.fori_loop(0, nvis, write_body, 0)

    def fill_body(j, carry):
        o_ref[0, j] = jnp.full((tk, tq), NEG, o_ref.dtype)
        return carry

    lax.fori_loop(nvis, nk, fill_body, 0)


def _indexer(qh, kvb, wi_t, batch, seq, tq, tk, qi_blk, ka_blk):
    nq, nk = seq // tq, seq // tk
    topk = min(TOPK_MAX, seq // 4)
    return pl.pallas_call(
        functools.partial(_indexer_kernel, tq=tq, tk=tk, nk=nk, topk=topk),
        grid=(batch, nq),
        in_specs=[pl.BlockSpec((IDX_HEADS // 2, tq, LANES), lambda b, i: (qi_blk, b * nq + i, 0)),
                  pl.BlockSpec((seq, LANES), lambda b, i: (b, ka_blk)),
                  pl.BlockSpec((seq, LANES), lambda b, i: (b, ka_blk + 1)),
                  pl.BlockSpec((IDX_HEADS, tq), lambda b, i: (0, b * nq + i))],
        out_specs=pl.BlockSpec((1, nk, tk, tq), lambda b, i: (b * nq + i, 0, 0, 0)),
        out_shape=jax.ShapeDtypeStruct((batch * nq, nk, tk, tq), BF16),
        scratch_shapes=[pltpu.VMEM((nk, tk, tq), I32), pltpu.VMEM((1, tq), I32)],
        compiler_params=_cparams(("arbitrary", "arbitrary")),
        name="indexer_topk_mask",
    )(qh, kvb, kvb, wi_t)


def _attn_kernel(q_ref, k_ref, vt_ref, m_ref, bt_ref, o_ref, acc_ref, mx_ref, l_ref, *, tq, tk):
    i = pl.program_id(1)
    exp2_scale = HEAD_DIM ** -0.5 * math.log2(math.e)
    mx_ref[...] = jnp.full(mx_ref.shape, NEG, F32)
    l_ref[...] = jnp.zeros(l_ref.shape, F32)
    acc_ref[...] = jnp.zeros(acc_ref.shape, F32)
    zero_blk = jnp.zeros((LANES, LANES), F32)

    def bias_tile(g, near):
        cols = []
        for r in range(KV_GROUP):
            h = g * KV_GROUP + r
            b0 = bt_ref[0, h]
            b1 = bt_ref[1, h]
            if near == 0:
                top = jnp.concatenate([b0, b1], axis=1)
                bot = jnp.concatenate([zero_blk, b0], axis=1)
            else:
                top = jnp.concatenate([zero_blk, zero_blk], axis=1)
                bot = jnp.concatenate([b1, zero_blk], axis=1)
            cols.append(jnp.concatenate([top, bot], axis=0))
        return jnp.concatenate(cols, axis=1)

    def update(j, near):
        k0 = pl.multiple_of(j * tk, tk)
        mt = m_ref[0, j].astype(F32)
        mt4 = jnp.concatenate([mt] * KV_GROUP, axis=1)
        for g in range(N_KV_HEADS):
            kt = k_ref[pl.ds(k0, tk), g * HEAD_DIM:(g + 1) * HEAD_DIM]
            qs = q_ref[g * KV_GROUP:(g + 1) * KV_GROUP].reshape(KV_GROUP * tq, HEAD_DIM)
            s = _dot_nt(kt, qs) + mt4
            if near is not None:
                s = s + bias_tile(g, near)
            m_old = mx_ref[g]
            m_new = jnp.maximum(m_old, jnp.max(s, axis=0, keepdims=True))
            alpha = jnp.exp2((m_old - m_new) * exp2_scale)
            p = jnp.exp2((s - m_new) * exp2_scale)
            l_ref[g] = alpha * l_ref[g] + jnp.sum(p, axis=0, keepdims=True)
            vt = vt_ref[0, j, g * HEAD_DIM:(g + 1) * HEAD_DIM, :]
            pv = jnp.dot(vt, p.astype(BF16), preferred_element_type=F32)
            acc_ref[g] = alpha * acc_ref[g] + pv
            mx_ref[g] = m_new

    def far_body(j, carry):
        update(j, None)
        return carry

    lax.fori_loop(0, jnp.maximum(i - 1, 0), far_body, 0)

    @pl.when(i >= 1)
    def _():
        update(i - 1, 1)

    update(i, 0)

    for g in range(N_KV_HEADS):
        o = acc_ref[g] / l_ref[g]
        for r in range(KV_GROUP):
            h = g * KV_GROUP + r
            o_ref[:, h * HEAD_DIM:(h + 1) * HEAD_DIM] = o[:, r * tq:(r + 1) * tq].T.astype(o_ref.dtype)


def _attention(qh, kvb, vt_tiles, mask, bias_t, batch, seq, tq, tk):
    nq, nk = seq // tq, seq // tk
    return pl.pallas_call(
        functools.partial(_attn_kernel, tq=tq, tk=tk),
        grid=(batch, nq),
        in_specs=[pl.BlockSpec((N_HEADS, tq, HEAD_DIM), lambda b, i: (0, b * nq + i, 0)),
                  pl.BlockSpec((seq, KV_WIDTH), lambda b, i: (b, 0)),
                  pl.BlockSpec((1, nk, KV_WIDTH, tk), lambda b, i: (b, 0, 0, 0)),
                  pl.BlockSpec((1, nk, tk, tq), lambda b, i: (b * nq + i, 0, 0, 0)),
                  pl.BlockSpec((2, N_HEADS, LANES, LANES), lambda b, i: (0, 0, 0, 0))],
        out_specs=pl.BlockSpec((tq, ATT_WIDTH), lambda b, i: (b * nq + i, 0)),
        out_shape=jax.ShapeDtypeStruct((batch * seq, ATT_WIDTH), BF16),
        scratch_shapes=[pltpu.VMEM((N_KV_HEADS, HEAD_DIM, KV_GROUP * tq), F32),
                        pltpu.VMEM((N_KV_HEADS, 1, KV_GROUP * tq), F32),
                        pltpu.VMEM((N_KV_HEADS, 1, KV_GROUP * tq), F32)],
        compiler_params=_cparams(("arbitrary", "arbitrary")),
        name="masked_attention",
    )(qh, kvb, vt_tiles, mask, bias_t)


def _conv_kernel(halo_ref, cur_ref, w_ref, b_ref, g_ref, bb_ref, o_ref, ext_ref, ph_ref, *, tt, tc, rc, zero_first):
    t = pl.program_id(1)
    halo = halo_ref[0]
    if zero_first:
        halo = jnp.where(t == 0, 0.0, halo)
    ext_ref[0:CONV_HALO, :] = halo
    ext_ref[CONV_HALO:CONV_HALO + tt, :] = cur_ref[0]
    first = CONV_HALO - (CONV_W - 1)
    ph_rows = ph_ref.shape[1]
    for p in range(1, SUBLANES):
        ph_ref[p - 1] = ext_ref[p:p + ph_rows, :]

    def tap_rows(start, cs):
        p, base = start % SUBLANES, start - start % SUBLANES
        if p == 0:
            return ext_ref[base:base + rc, cs]
        return ph_ref[p - 1, base:base + rc, cs]

    for c in range(tc // LANES):
        cs = slice(c * LANES, (c + 1) * LANES)
        for r in range(tt // rc):
            acc = jnp.zeros((rc, LANES), F32) + b_ref[:, cs]
            for j in range(CONV_W):
                acc = acc + w_ref[j:j + 1, cs] * tap_rows(r * rc + first + j, cs)
            mu = jnp.mean(acc, axis=-1, keepdims=True)
            dv = acc - mu
            var = jnp.mean(dv * dv, axis=-1, keepdims=True)
            yn = dv * lax.rsqrt(var + EPS) * g_ref[:, cs] + bb_ref[:, cs]
            o_ref[0, r * rc:(r + 1) * rc, cs] = (yn * jax.nn.sigmoid(yn)).astype(o_ref.dtype)


def _conv(halo_src, cur, conv_w, conv_b, cn_g, cn_b, tt, tc, zero_first):
    b, t, c = cur.shape
    hb = tt // CONV_HALO
    if zero_first:
        halo_spec = pl.BlockSpec((1, CONV_HALO, tc), lambda a, i, j: (a, jnp.maximum(i * hb - 1, 0), j))
    else:
        halo_spec = pl.BlockSpec((1, CONV_HALO, tc), lambda a, i, j: (a, 0, j))
    vec = lambda: pl.BlockSpec((1, tc), lambda a, i, j: (0, j))
    return pl.pallas_call(
        functools.partial(_conv_kernel, tt=tt, tc=tc, rc=min(tt, 64), zero_first=zero_first),
        grid=(b, t // tt, c // tc),
        in_specs=[halo_spec,
                  pl.BlockSpec((1, tt, tc), lambda a, i, j: (a, i, j)),
                  pl.BlockSpec((CONV_W, tc), lambda a, i, j: (0, j)),
                  vec(), vec(), vec()],
        out_specs=pl.BlockSpec((1, tt, tc), lambda a, i, j: (a, i, j)),
        out_shape=jax.ShapeDtypeStruct((b, t, c), BF16),
        scratch_shapes=[pltpu.VMEM((CONV_HALO + tt, tc), F32),
                        pltpu.VMEM((SUBLANES - 1, CONV_HALO + tt - SUBLANES, tc), F32)],
        compiler_params=_cparams(("arbitrary", "arbitrary", "arbitrary")),
        name="conformer_conv",
    )(halo_src, cur, conv_w, conv_b.reshape(1, c), cn_g.reshape(1, c), cn_b.reshape(1, c))


def _outproj_kernel(a_ref, c_ref, wa_ref, wc_ref, x_ref, gt_ref, o_ref):
    acc = jnp.dot(a_ref[0], wa_ref[...], preferred_element_type=F32)
    acc = acc + jnp.dot(c_ref[0], wc_ref[...], preferred_element_type=F32)
    o_ref[0] = x_ref[0] + gt_ref[0] * acc


def _outproj(att3, conv3, wo_a, wo_c, x3, gt3, tm, tn):
    g, r, d = x3.shape
    ka, kc = att3.shape[-1], conv3.shape[-1]
    gt_spec = (pl.BlockSpec((1, 1, tn), lambda a, i, j: (a, 0, j)) if gt3.shape[1] == 1
               else pl.BlockSpec((1, tm, tn), lambda a, i, j: (a, i, j)))
    return pl.pallas_call(
        _outproj_kernel,
        grid=(g, r // tm, d // tn),
        in_specs=[pl.BlockSpec((1, tm, ka), lambda a, i, j: (a, i, 0)),
                  pl.BlockSpec((1, tm, kc), lambda a, i, j: (a, i, 0)),
                  pl.BlockSpec((ka, tn), lambda a, i, j: (0, j)),
                  pl.BlockSpec((kc, tn), lambda a, i, j: (0, j)),
                  pl.BlockSpec((1, tm, tn), lambda a, i, j: (a, i, j)),
                  gt_spec],
        out_specs=pl.BlockSpec((1, tm, tn), lambda a, i, j: (a, i, j)),
        out_shape=jax.ShapeDtypeStruct((g, r, d), F32),
        compiler_params=_cparams(("arbitrary", "arbitrary", "arbitrary")),
        name="outproj_residual",
    )(att3, conv3, wo_a, wo_c, x3, gt3)


def _peer_scores_kernel(ht_ref, wqt_ref, sk_ref, o_ref):
    q_t = jnp.dot(wqt_ref[...], ht_ref[...], preferred_element_type=F32).astype(BF16)
    half = q_t.shape[0] // 2
    o_ref[0:PEER_NKEYS, :] = jnp.dot(sk_ref[0, 0], q_t[:half], preferred_element_type=F32)
    o_ref[PEER_NKEYS:2 * PEER_NKEYS, :] = jnp.dot(sk_ref[0, 1], q_t[half:], preferred_element_type=F32)


def _peer_scores(h2t, wq_t, sk, tm):
    d, n = h2t.shape
    dk = wq_t.shape[0] // PEER_HEADS
    return pl.pallas_call(
        _peer_scores_kernel,
        grid=(n // tm, PEER_HEADS),
        in_specs=[pl.BlockSpec((d, tm), lambda i, h: (0, i)),
                  pl.BlockSpec((dk, d), lambda i, h: (h, 0)),
                  pl.BlockSpec((1, 2, PEER_NKEYS, dk // 2), lambda i, h: (h, 0, 0, 0))],
        out_specs=pl.BlockSpec((2 * PEER_NKEYS, tm), lambda i, h: (h, i)),
        out_shape=jax.ShapeDtypeStruct((PEER_HEADS * 2 * PEER_NKEYS, n), F32),
        compiler_params=_cparams(("arbitrary", "arbitrary")),
        name="peer_subkey_scores",
    )(h2t, wq_t, sk)


def _top16(x):
    rows = x.shape[0]
    rid = lax.broadcasted_iota(I32, x.shape, 0).astype(F32)
    vals = []
    for k in range(PEER_TOPK):
        m = jnp.max(x, axis=0, keepdims=True)
        vals.append(m)
        if k + 1 < PEER_TOPK:
            first = jnp.min(jnp.where(x == m, rid, float(rows)), axis=0, keepdims=True)
            x = jnp.where(rid == first, -jnp.inf, x)
    return vals


def _stack_rows(rows):
    shape = (len(rows), rows[0].shape[1])
    rid = lax.broadcasted_iota(I32, shape, 0)
    out = jnp.zeros(shape, rows[0].dtype)
    for k, row in enumerate(rows):
        out = jnp.where(rid == k, row, out)
    return out


def _peer_route_kernel(s_ref, s1m_ref, s2m_ref, c1_ref, e2_ref, thr_ref):
    s1 = s_ref[0:PEER_NKEYS, :]
    s2 = s_ref[PEER_NKEYS:2 * PEER_NKEYS, :]
    t1 = _top16(s1)
    t2 = _top16(s2)
    t2_all = _stack_rows(t2)
    blocks = [t1[0] + t2_all]
    for a in range(1, 8):
        blocks.append(t1[a] + t2_all[0:8])
    blocks.append(_stack_rows(t1[8:16]) + t2[0])
    top = _top16(jnp.concatenate(blocks, axis=0))
    z = jnp.ones_like(top[0])
    for k in range(1, PEER_TOPK):
        z = z + jnp.exp(top[k] - top[0])
    s1m_ref[...] = jnp.where(s1 >= t1[PEER_TOPK - 1], s1, -jnp.inf)
    s2m_ref[...] = jnp.where(s2 >= t2[PEER_TOPK - 1], s2, -jnp.inf)
    c1_ref[...] = jnp.exp(s1 - t1[0]) / z
    e2_ref[...] = jnp.exp(s2 - t2[0])
    thr_ref[0] = top[PEER_TOPK - 1]


def _peer_route(s_t, tl):
    rows, n = s_t.shape
    big = lambda: pl.BlockSpec((PEER_NKEYS, tl), lambda h, t: (h, t))
    big_shape = jax.ShapeDtypeStruct((PEER_HEADS * PEER_NKEYS, n), F32)
    return pl.pallas_call(
        _peer_route_kernel,
        grid=(PEER_HEADS, n // tl),
        in_specs=[pl.BlockSpec((2 * PEER_NKEYS, tl), lambda h, t: (h, t))],
        out_specs=[big(), big(), big(), big(), pl.BlockSpec((1, 1, tl), lambda h, t: (h, 0, t))],
        out_shape=[big_shape, big_shape, big_shape, big_shape,
                   jax.ShapeDtypeStruct((PEER_HEADS, 1, n), F32)],
        compiler_params=_cparams(("arbitrary", "arbitrary")),
        name="peer_route",
    )(s_t)


GATE_ROWS = 64


MXU_COLS = 256


def _peer_dense_kernel(ht_ref, u_ref, v_ref, s1_ref, c1_ref, s2_ref, e2_ref, thr_ref, o_ref, *, te):
    e = pl.program_id(1)
    tm = ht_ref.shape[1]
    rows_per_tile = te // PEER_NKEYS

    @pl.when(e == 0)
    def _():
        o_ref[...] = jnp.zeros(o_ref.shape, F32)

    chunk = min(MXU_COLS, tm)
    a_chunks = [jnp.dot(u_ref[...], ht_ref[:, c * chunk:(c + 1) * chunk], preferred_element_type=F32)
                for c in range(tm // chunk)]

    s1rows = [[s1_ref[pl.ds(h * PEER_NKEYS + e * rows_per_tile + r, 1), :] for r in range(rows_per_tile)]
              for h in range(PEER_HEADS)]
    c1rows = [[c1_ref[pl.ds(h * PEER_NKEYS + e * rows_per_tile + r, 1), :] for r in range(rows_per_tile)]
              for h in range(PEER_HEADS)]
    for c in range(tm // chunk):
        w_cols = []
        for lc in range(chunk // LANES):
            ls = slice(c * chunk + lc * LANES, c * chunk + (lc + 1) * LANES)
            als = slice(lc * LANES, (lc + 1) * LANES)
            blocks = [[None] * (PEER_NKEYS // GATE_ROWS) for _ in range(rows_per_tile)]
            for part in range(PEER_NKEYS // GATE_ROWS):
                rs = [slice(r * PEER_NKEYS + part * GATE_ROWS, r * PEER_NKEYS + (part + 1) * GATE_ROWS)
                      for r in range(rows_per_tile)]
                acc = [jnp.zeros((GATE_ROWS, LANES), F32) for _ in range(rows_per_tile)]
                for h in range(PEER_HEADS):
                    row0 = h * PEER_NKEYS + part * GATE_ROWS
                    s2 = s2_ref[row0:row0 + GATE_ROWS, ls]
                    e2 = e2_ref[row0:row0 + GATE_ROWS, ls]
                    thr = thr_ref[h, :, ls]
                    for r in range(rows_per_tile):
                        cand = s2 + s1rows[h][r][:, ls]
                        acc[r] = acc[r] + jnp.where(cand >= thr, e2, 0.0) * c1rows[h][r][:, ls]
                for r in range(rows_per_tile):
                    blocks[r][part] = (acc[r] * jax.nn.gelu(a_chunks[c][rs[r], als])).astype(BF16)
            w_cols.append(jnp.concatenate([b for row in blocks for b in row], axis=0))
        cs = slice(c * chunk, (c + 1) * chunk)
        o_ref[:, cs] += lax.dot_general(v_ref[...], jnp.concatenate(w_cols, axis=1),
                                        (((0,), (0,)), ((), ())), preferred_element_type=F32)


def _peer_dense(h2t, u_bf, v_bf, s1m, c1, s2m, e2, thr, tm, te):
    d, n = h2t.shape
    rows = PEER_HEADS * PEER_NKEYS
    once = pl.Buffered(1)
    tok = lambda: pl.BlockSpec((rows, tm), lambda i, e: (0, i), pipeline_mode=once)
    return pl.pallas_call(
        functools.partial(_peer_dense_kernel, te=te),
        grid=(n // tm, u_bf.shape[0] // te),
        in_specs=[pl.BlockSpec((d, tm), lambda i, e: (0, i), pipeline_mode=once),
                  pl.BlockSpec((te, d), lambda i, e: (e, 0)),
                  pl.BlockSpec((te, d), lambda i, e: (e, 0)),
                  tok(), tok(), tok(), tok(),
                  pl.BlockSpec((PEER_HEADS, 1, tm), lambda i, e: (0, 0, i), pipeline_mode=once)],
        out_specs=pl.BlockSpec((d, tm), lambda i, e: (0, i), pipeline_mode=once),
        out_shape=jax.ShapeDtypeStruct((d, n), F32),
        compiler_params=_cparams(("arbitrary", "arbitrary"), VMEM_LIMIT_PEER),
        name="peer_dense_experts",
    )(h2t, u_bf, v_bf, s1m, c1, s2m, e2, thr)


def _final_kernel(x_ref, gt_ref, p_ref, g_ref, o_ref, *, normalize):
    xx = x_ref[0] + gt_ref[0] * p_ref[...].T
    if normalize:
        ms = jnp.mean(xx * xx, axis=-1, keepdims=True)
        xx = xx * lax.rsqrt(ms + EPS) * g_ref[...]
    o_ref[0] = xx


def _final(x3, gt3, peer_t, g_final, tm, normalize):
    g, r, d = x3.shape
    nb = r // tm
    return pl.pallas_call(
        functools.partial(_final_kernel, normalize=normalize),
        grid=(g, nb),
        in_specs=[pl.BlockSpec((1, tm, d), lambda a, i: (a, i, 0)),
                  _row_mod_spec(gt3, tm),
                  pl.BlockSpec((d, tm), lambda a, i: (0, a * nb + i)),
                  pl.BlockSpec((1, d), lambda a, i: (0, 0))],
        out_specs=pl.BlockSpec((1, tm, d), lambda a, i: (a, i, 0)),
        out_shape=jax.ShapeDtypeStruct((g, r, d), F32),
        compiler_params=_cparams(("arbitrary", "arbitrary")),
        name="final_norm",
    )(x3, gt3, peer_t, g_final.reshape(1, d))


def _sample_index_kernel(pt_ref, qi_ref, wi_ref, kself_ref, expand_ref, cki_ref, mask_ref, self_ref,
                         kbuf_ref, sem_ref, sc_ref, scself_ref, *, n_pages, topk, page_base):
    b = pl.program_id(0)
    slot = b % 2
    score_scale = IDX_DIM ** -0.5 * IDX_HEADS ** -0.5

    def page_copy(seq, k, sl):
        page = page_base + pt_ref[seq, k]
        return pltpu.make_async_copy(cki_ref.at[page], kbuf_ref.at[sl, :, pl.ds(k * PAGE_SIZE, PAGE_SIZE)],
                                     sem_ref.at[sl])

    @pl.when(b == 0)
    def _():
        for k in range(n_pages):
            page_copy(0, k, 0).start()

    @pl.when(b + 1 < pl.num_programs(0))
    def _():
        for k in range(n_pages):
            page_copy(b + 1, k, 1 - slot).start()

    for k in range(n_pages):
        page_copy(b, k, slot).wait()

    qi = qi_ref[0]
    wi = wi_ref[0]
    s = jnp.dot(qi, kbuf_ref[slot].astype(BF16), preferred_element_type=F32)
    sc_ref[pl.ds(b, 1), :] = jnp.sum(jnp.maximum(s, 0.0) * wi, axis=0, keepdims=True) * score_scale
    ks = kself_ref[0].astype(BF16).astype(F32)
    s_self = jnp.sum(qi.astype(F32) * ks, axis=1, keepdims=True)
    self_score = jnp.sum(jnp.maximum(s_self, 0.0) * wi, axis=0, keepdims=True) * score_scale
    scself_ref[pl.ds(b, 1), :] = jnp.zeros((1, LANES), F32) + self_score

    @pl.when(b == pl.num_programs(0) - 1)
    def _():
        keys = _sortable(sc_ref[...])
        kself = _sortable(scself_ref[:, 0:1])
        pos = lax.broadcasted_iota(I32, keys.shape, 1)
        self_pos = n_pages * PAGE_SIZE

        def total(x, xs):
            return jnp.sum(x, axis=1, keepdims=True) + xs

        def bit_body(t, thr):
            cand = thr ^ lax.shift_left(jnp.int32(1), 31 - t)
            cnt = total(jnp.where(keys >= cand, 1, 0), jnp.where(kself >= cand, 1, 0))
            return jnp.where(cnt >= topk, cand, thr)

        thr = lax.fori_loop(0, 32, bit_body, jnp.full(kself.shape, INT_MIN, I32))
        need = topk - total(jnp.where(keys > thr, 1, 0), jnp.where(kself > thr, 1, 0))

        def idx_body(t, c):
            cand = c | lax.shift_left(jnp.int32(1), 15 - t)
            f = total(jnp.where(keys == thr, jnp.where(pos < cand, 1, 0), 0),
                      jnp.where(kself == thr, jnp.where(self_pos < cand, 1, 0), 0))
            return jnp.where(f < need, cand, c)

        cstar = lax.fori_loop(0, 16, idx_body, jnp.zeros(kself.shape, I32))
        picked = jnp.where(keys > thr, 1.0, jnp.where(keys == thr, jnp.where(pos <= cstar, 1.0, 0.0), 0.0))
        for k in range(n_pages):
            flags = picked[:, k * PAGE_SIZE:(k + 1) * PAGE_SIZE].astype(BF16)
            rows = jnp.dot(flags, expand_ref[...], preferred_element_type=F32)
            mask_ref[k] = jnp.where(rows > 0.5, 0.0, NEG)
        ssel = jnp.where(kself > thr, 0.0, jnp.where(kself == thr, jnp.where(self_pos <= cstar, 0.0, NEG), NEG))
        self_ref[...] = jnp.zeros(self_ref.shape, F32) + ssel


def _sample_index(page_table, qi3, wi3, kself3, cki_t, page_base):
    db, n_pages = page_table.shape
    topk = min(TOPK_MAX, (n_pages * PAGE_SIZE + 1) // 4)
    page_rows = PAGE_SIZE * N_KV_HEADS
    expand = (jnp.arange(page_rows, dtype=I32)[None, :] // N_KV_HEADS
              == jnp.arange(PAGE_SIZE, dtype=I32)[:, None]).astype(BF16)
    grid_spec = pltpu.PrefetchScalarGridSpec(
        num_scalar_prefetch=1,
        grid=(db,),
        in_specs=[pl.BlockSpec((1, IDX_HEADS, IDX_DIM), lambda b, pt: (b, 0, 0)),
                  pl.BlockSpec((1, IDX_HEADS, 1), lambda b, pt: (b, 0, 0)),
                  pl.BlockSpec((1, 1, IDX_DIM), lambda b, pt: (b, 0, 0)),
                  pl.BlockSpec((PAGE_SIZE, page_rows), lambda b, pt: (0, 0)),
                  pl.BlockSpec(memory_space=pl.ANY)],
        out_specs=[pl.BlockSpec((n_pages, db, page_rows), lambda b, pt: (0, 0, 0)),
                   pl.BlockSpec((db, LANES), lambda b, pt: (0, 0))],
        scratch_shapes=[pltpu.VMEM((2, IDX_DIM, n_pages * PAGE_SIZE), F32),
                        pltpu.SemaphoreType.DMA((2,)),
                        pltpu.VMEM((db, n_pages * PAGE_SIZE), F32),
                        pltpu.VMEM((db, LANES), F32)],
    )
    mask_t, selfsel = pl.pallas_call(
        functools.partial(_sample_index_kernel, n_pages=n_pages, topk=topk, page_base=page_base),
        grid_spec=grid_spec,
        out_shape=[jax.ShapeDtypeStruct((n_pages, db, page_rows), F32),
                   jax.ShapeDtypeStruct((db, LANES), F32)],
        compiler_params=_cparams(("arbitrary",)),
        name="sample_indexer",
    )(page_table, qi3, wi3, kself3, expand, cki_t)
    return mask_t.transpose(1, 0, 2), jnp.broadcast_to(selfsel[:, None, :], (db, SUBLANES, LANES))


def _sample_attn_kernel(pt_ref, q_ref, kself_ref, vself_ref, mask_ref, self_ref, bias_ref, bself_ref,
                        *rest, n_pages):
    pps = ATTN_PAGES_PER_STEP
    kpages = rest[:pps]
    vpages = rest[pps:2 * pps]
    o_ref, acc_ref, mx_ref, l_ref = rest[2 * pps:]
    step = pl.program_id(1)
    n_steps = n_pages // pps
    page_rows = PAGE_SIZE * N_KV_HEADS
    sm_scale = HEAD_DIM ** -0.5
    q = q_ref[0]
    head_group = lax.broadcasted_iota(I32, (N_HEADS, HEAD_DIM), 0) // KV_GROUP

    def own_group_rows(x_ref):
        out = jnp.zeros((N_HEADS, HEAD_DIM), F32)
        for g in range(N_KV_HEADS):
            out = jnp.where(head_group == g, x_ref[0, g:g + 1, :].astype(BF16).astype(F32), out)
        return out

    @pl.when(step == 0)
    def _():
        logit = jnp.sum(q.astype(F32) * own_group_rows(kself_ref), axis=1, keepdims=True)
        mx_ref[...] = logit * sm_scale + bself_ref[...] + self_ref[0, 0:1, 0:1]
        l_ref[...] = jnp.ones(l_ref.shape, F32)
        acc_ref[...] = own_group_rows(vself_ref)

    kcat = jnp.concatenate([kpages[k][0].astype(BF16) for k in range(pps)], axis=0)
    vcat = jnp.concatenate([vpages[k][0].astype(BF16) for k in range(pps)], axis=0)
    s = _dot_nt(q, kcat) * sm_scale + bias_ref[step]
    s = jnp.concatenate([s[:, k * page_rows:(k + 1) * page_rows] + mask_ref[0, pl.ds(step * pps + k, 1), :]
                         for k in range(pps)], axis=1)
    m_old = mx_ref[...]
    m_new = jnp.maximum(m_old, jnp.max(s, axis=1, keepdims=True))
    alpha = jnp.exp(m_old - m_new)
    p = jnp.exp(s - m_new)
    l_ref[...] = alpha * l_ref[...] + jnp.sum(p, axis=1, keepdims=True)
    acc_ref[...] = alpha * acc_ref[...] + jnp.dot(p.astype(BF16), vcat, preferred_element_type=F32)
    mx_ref[...] = m_new

    @pl.when(step == n_steps - 1)
    def _():
        o_ref[0] = (acc_ref[...] / l_ref[...]).astype(o_ref.dtype)


def _sample_attn(page_table, q3, kself3, vself3, mask, selfsel, bias_steps, bias_self, ck, cv, page_base):
    db, n_pages = page_table.shape
    pps = ATTN_PAGES_PER_STEP
    n_steps = n_pages // pps
    page_rows = PAGE_SIZE * N_KV_HEADS
    step_rows = pps * page_rows

    def page_spec(k):
        return pl.BlockSpec((1, page_rows, HEAD_DIM),
                            lambda b, s, pt: (page_base + pt[b, s * pps + k], 0, 0))

    per_b = lambda shape: pl.BlockSpec((1,) + shape, lambda b, s, pt: (b, 0, 0))
    grid_spec = pltpu.PrefetchScalarGridSpec(
        num_scalar_prefetch=1,
        grid=(db, n_steps),
        in_specs=[per_b((N_HEADS, HEAD_DIM)), per_b((N_KV_HEADS, HEAD_DIM)), per_b((N_KV_HEADS, HEAD_DIM)),
                  per_b((n_pages, page_rows)), per_b((SUBLANES, LANES)),
                  pl.BlockSpec((n_steps, N_HEADS, step_rows), lambda b, s, pt: (0, 0, 0)),
                  pl.BlockSpec((N_HEADS, 1), lambda b, s, pt: (0, 0))]
                 + [page_spec(k) for k in range(pps)] * 2,
        out_specs=per_b((N_HEADS, HEAD_DIM)),
        scratch_shapes=[pltpu.VMEM((N_HEADS, HEAD_DIM), F32),
                        pltpu.VMEM((N_HEADS, 1), F32),
                        pltpu.VMEM((N_HEADS, 1), F32)],
    )
    return pl.pallas_call(
        functools.partial(_sample_attn_kernel, n_pages=n_pages),
        grid_spec=grid_spec,
        out_shape=jax.ShapeDtypeStruct((db, N_HEADS, HEAD_DIM), BF16),
        compiler_params=_cparams(("arbitrary", "arbitrary")),
        name="sample_attention",
    )(page_table, q3, kself3, vself3, mask, selfsel, bias_steps, bias_self,
      *([ck] * pps), *([cv] * pps))


def _rel_bucket(dist):
    n = jnp.maximum(dist, 0)
    max_exact = REL_BUCKETS // 2
    nf = jnp.maximum(n, 1).astype(F32)
    large = max_exact + (jnp.log(nf / max_exact) / math.log(REL_MAX_DIST / max_exact)
                         * (REL_BUCKETS - max_exact)).astype(I32)
    large = jnp.minimum(large, REL_BUCKETS - 1)
    return jnp.where(n < max_exact, n, large)


def _bias_of_dist(rel_bias, dist):
    onehot = (_rel_bucket(dist)[..., None] == jnp.arange(REL_BUCKETS, dtype=I32)).astype(F32)
    return jnp.einsum("...b,bh->...h", onehot, rel_bias.astype(F32), precision=lax.Precision.HIGHEST)


def _prompt_bias_tiles(rel_bias):
    kc = jnp.arange(LANES, dtype=I32)[:, None]
    qr = jnp.arange(LANES, dtype=I32)[None, :]
    far = rel_bias[REL_BUCKETS - 1]
    tiles = []
    for off in (0, LANES):
        dist = off + qr - kc
        t = (_bias_of_dist(rel_bias, dist) - far) * HEAD_DIM ** 0.5
        tiles.append(jnp.where((dist >= 0)[..., None], t, 0.0).transpose(2, 0, 1))
    return jnp.stack(tiles).astype(F32)


def _sample_bias_steps(rel_bias, n_pages):
    past = n_pages * PAGE_SIZE
    bias = _bias_of_dist(rel_bias, past - jnp.arange(past, dtype=I32))
    own = (jnp.arange(N_HEADS, dtype=I32)[:, None] // KV_GROUP) == jnp.arange(N_KV_HEADS, dtype=I32)[None, :]
    rows = jnp.where(own[None], bias[:, :, None], NEG)
    rows = rows.transpose(1, 0, 2).reshape(N_HEADS, n_pages // ATTN_PAGES_PER_STEP, -1)
    return rows.transpose(1, 0, 2).astype(F32)


def _split_in_proj(w_in_l):
    sizes = (ATT_WIDTH, KV_WIDTH, KV_WIDTH, IDX_HEADS * IDX_DIM, IDX_DIM, IDX_HEADS)
    offs = [0]
    for s in sizes:
        offs.append(offs[-1] + s)
    conv_ch = (w_in_l.shape[1] - offs[-1]) // 2
    d = w_in_l.shape[0]
    wq, wk, wv, wqi, wki, wwi = (w_in_l[:, offs[i]:offs[i + 1]] for i in range(6))
    wua = w_in_l[:, offs[-1]:offs[-1] + conv_ch]
    wub = w_in_l[:, offs[-1] + conv_ch:]
    z = lambda n: jnp.zeros((d, n), w_in_l.dtype)
    w_a = jnp.concatenate([wk, wv, wki, z(LANES - IDX_DIM), z(LANES - IDX_DIM), wki,
                           wwi, z(LANES - IDX_HEADS), z(LANES)], axis=1)
    w_b = jnp.concatenate([wq, wqi], axis=1)
    chunk = 256
    parts = []
    for c in range(conv_ch // chunk):
        parts += [wua[:, c * chunk:(c + 1) * chunk], wub[:, c * chunk:(c + 1) * chunk]]
    w_c = jnp.concatenate(parts, axis=1)
    return w_a.astype(BF16), w_b.astype(BF16), w_c.astype(BF16)


COL_K, COL_V, COL_KA, COL_WI = 0, KV_WIDTH, 2 * KV_WIDTH, 2 * KV_WIDTH + 2 * LANES


def _mixer_projections(h, w_a, w_b, w_c, tm):
    z_a = _matmul(h, w_a, F32, tm, 512)
    qh = _matmul(h, w_b, BF16, tm, 512, head_major=True)
    u = _matmul(h, w_c, F32, tm, 512, glu=True)
    return z_a, qh, u


ATTN_TILE = 256
PROJ_TM, PROJ_TN = 1024, 512
NORM_ROWS = 512
FINAL_ROWS = 256
CONV_TT, CONV_TC = 256, 512
PEER_TM, PEER_TE = 512, 1024
PEER_TE_DECODE = 512
ROUTE_LANES = 256


def _peer_block(h2t, wq_t, sk_bf, u_bf, v_bf, tm, te):
    s_t = _peer_scores(h2t, wq_t, sk_bf, tm)
    s1m, s2m, c1, e2, thr = _peer_route(s_t, min(tm, ROUTE_LANES))
    return _peer_dense(h2t, u_bf, v_bf, s1m, c1, s2m, e2, thr, tm, te)


def kernel(x_prompt, x_sample, cache_k, cache_v, cache_kidx, state_conv, page_table, c_prompt, c_sample,
           rel_bias, w_ada, b_ada, g_mix, w_in, conv_w, conv_b, cn_g, cn_b, w_o, g_ch, peer_wq,
           peer_subkeys, peer_u, peer_v, g_final):
    batch, seq, d = x_prompt.shape
    db = x_sample.shape[0]
    depth = w_ada.shape[0]
    n_pages = page_table.shape[1]
    conv_ch = conv_w.shape[-1]
    n_prompt = batch * seq
    tq = tk = ATTN_TILE

    xp = x_prompt
    xs = jnp.pad(x_sample.reshape(1, db, d), ((0, 0), (0, SAMPLE_ROWS - db), (0, 0)))
    c_rows = batch + db
    c_pad = (-c_rows) % 16
    c_all = jnp.pad(jnp.concatenate([c_prompt, c_sample], axis=0), ((0, c_pad), (0, 0)))
    bias_t = _prompt_bias_tiles(rel_bias)
    bias_steps = _sample_bias_steps(rel_bias, n_pages)
    n_pool = cache_k.shape[1]
    page_rows = PAGE_SIZE * N_KV_HEADS
    ck_rows = cache_k.reshape(depth * n_pool, page_rows, HEAD_DIM)
    cv_rows = cache_v.reshape(depth * n_pool, page_rows, HEAD_DIM)
    cki_t = jnp.swapaxes(cache_kidx, 2, 3).reshape(depth * n_pool, IDX_DIM, PAGE_SIZE)
    bias_self = rel_bias[_rel_bucket(jnp.zeros((), I32))].reshape(N_HEADS, 1)

    outs = {k: [] for k in ("kp", "vp", "kip", "cp", "ks", "vs", "kis", "cs")}
    for l in range(depth):
        mods = _adaln(c_all, w_ada[l], b_ada[l])
        p_mod = [m.reshape(batch, 1, d) for m in jnp.split(mods[:batch], 6, axis=-1)]
        s_mod = [jnp.pad(m.reshape(1, db, d), ((0, 0), (0, SAMPLE_ROWS - db), (0, 0)))
                 for m in jnp.split(mods[batch:c_rows], 6, axis=-1)]
        w_a, w_b, w_c = _split_in_proj(w_in[l])
        wo_a = w_o[l][:ATT_WIDTH].astype(BF16)
        wo_c = w_o[l][ATT_WIDTH:].astype(BF16)

        hp = _modulate(xp, g_mix[l], p_mod[1], p_mod[0], NORM_ROWS).reshape(n_prompt, d)
        z_a, qh, u = _mixer_projections(hp, w_a, w_b, w_c, PROJ_TM)
        kvb = z_a[:, :COL_WI].astype(BF16)
        wi_t = z_a[:, COL_WI:COL_WI + IDX_HEADS].T
        vt_tiles = (kvb[:, COL_V:COL_V + KV_WIDTH].reshape(batch, seq // tk, tk, KV_WIDTH)
                    .transpose(0, 1, 3, 2))
        mask = _indexer(qh, kvb, wi_t, batch, seq, tq, tk, qi_blk=N_HEADS // (IDX_HEADS // 2),
                        ka_blk=COL_KA // LANES)
        att = _attention(qh, kvb, vt_tiles, mask, bias_t, batch, seq, tq, tk)
        u3 = u.reshape(batch, seq, conv_ch)
        conv = _conv(u3, u3, conv_w[l], conv_b[l], cn_g[l], cn_b[l], CONV_TT, CONV_TC, zero_first=True)
        xp = _outproj(att.reshape(batch, seq, ATT_WIDTH), conv, wo_a, wo_c, xp, p_mod[2], PROJ_TM, PROJ_TN)
        outs["kp"].append(z_a[:, COL_K:COL_K + KV_WIDTH].reshape(batch, seq, N_KV_HEADS, HEAD_DIM))
        outs["vp"].append(z_a[:, COL_V:COL_V + KV_WIDTH].reshape(batch, seq, N_KV_HEADS, HEAD_DIM))
        outs["kip"].append(z_a[:, COL_KA:COL_KA + IDX_DIM].reshape(batch, seq, IDX_DIM))
        outs["cp"].append(u3[:, seq - (CONV_W - 1):])

        hs = _modulate(xs, g_mix[l], s_mod[1], s_mod[0], SAMPLE_ROWS).reshape(SAMPLE_ROWS, d)
        zs_a, qhs, us = _mixer_projections(hs, w_a, w_b, w_c, SAMPLE_ROWS)
        k_new = zs_a[:db, COL_K:COL_K + KV_WIDTH]
        v_new = zs_a[:db, COL_V:COL_V + KV_WIDTH]
        ki_new = zs_a[:db, COL_KA:COL_KA + IDX_DIM]
        wi_new = zs_a[:db, COL_WI:COL_WI + IDX_HEADS]
        q_s = qhs[:N_HEADS, :db].transpose(1, 0, 2)
        qi_s = (qhs[N_HEADS:, :db].transpose(1, 0, 2)
                .reshape(db, IDX_HEADS // 2, 2, IDX_DIM).reshape(db, IDX_HEADS, IDX_DIM))
        smask, sself = _sample_index(page_table, qi_s, wi_new.reshape(db, IDX_HEADS, 1),
                                     ki_new.reshape(db, 1, IDX_DIM), cki_t, l * n_pool)
        att_s = _sample_attn(page_table, q_s, k_new.reshape(db, N_KV_HEADS, HEAD_DIM),
                             v_new.reshape(db, N_KV_HEADS, HEAD_DIM), smask, sself, bias_steps, bias_self,
                             ck_rows, cv_rows, l * n_pool)
        att_s = jnp.pad(att_s.reshape(1, db, ATT_WIDTH), ((0, 0), (0, SAMPLE_ROWS - db), (0, 0)))
        u_new = us[:db]
        state = state_conv[l].astype(F32)
        halo = jnp.pad(state, ((0, 0), (CONV_HALO - (CONV_W - 1), 0), (0, 0)))
        cur = jnp.pad(u_new.reshape(db, 1, conv_ch), ((0, 0), (0, SUBLANES - 1), (0, 0)))
        conv_s = _conv(halo, cur, conv_w[l], conv_b[l], cn_g[l], cn_b[l], SUBLANES, CONV_TC, zero_first=False)
        conv_s = jnp.pad(conv_s[:, 0].reshape(1, db, conv_ch), ((0, 0), (0, SAMPLE_ROWS - db), (0, 0)))
        xs = _outproj(att_s, conv_s, wo_a, wo_c, xs, s_mod[2], SAMPLE_ROWS, PROJ_TN)
        outs["ks"].append(k_new.reshape(db, 1, N_KV_HEADS, HEAD_DIM))
        outs["vs"].append(v_new.reshape(db, 1, N_KV_HEADS, HEAD_DIM))
        outs["kis"].append(ki_new.reshape(db, 1, IDX_DIM))
        outs["cs"].append(jnp.concatenate([state[:, 1:], u_new.reshape(db, 1, conv_ch)], axis=1))

        wq_t = peer_wq[l].astype(BF16).T
        sk_bf = peer_subkeys[l].astype(BF16)
        u_bf = peer_u[l].astype(BF16)
        v_bf = peer_v[l].astype(BF16)
        hp2 = _modulate(xp, g_ch[l], p_mod[4], p_mod[3], NORM_ROWS, transposed=True)
        peer_p = _peer_block(hp2, wq_t, sk_bf, u_bf, v_bf, PEER_TM, PEER_TE)
        hs2 = _modulate(xs, g_ch[l], s_mod[4], s_mod[3], SAMPLE_ROWS, transposed=True)
        peer_s = _peer_block(hs2, wq_t, sk_bf, u_bf, v_bf, SAMPLE_ROWS, PEER_TE_DECODE)
        last = l == depth - 1
        xp = _final(xp, p_mod[5], peer_p, g_final, FINAL_ROWS, normalize=last)
        xs = _final(xs, s_mod[5], peer_s, g_final, SAMPLE_ROWS, normalize=last)

    st = lambda k: jnp.stack(outs[k])
    y_sample = xs[0, :db].reshape(db, 1, d)
    return (xp, y_sample, st("kp"), st("vp"), st("kip"), st("cp"),
            st("ks"), st("vs"), st("kis"), st("cs"))
```

```python
import functools
import math

import jax
import jax.numpy as jnp
from jax import lax
from jax.experimental import pallas as pl
from jax.experimental.pallas import tpu as pltpu

F32 = jnp.float32
BF16 = jnp.bfloat16
I32 = jnp.int32

HEAD_DIM = 128
N_HEADS = 16
N_KV_HEADS = 4
KV_GROUP = N_HEADS // N_KV_HEADS
ATT_WIDTH = N_HEADS * HEAD_DIM
KV_WIDTH = N_KV_HEADS * HEAD_DIM
IDX_HEADS = 16
IDX_DIM = 64
TOPK_MAX = 256
REL_BUCKETS = 32
REL_MAX_DIST = 128
CONV_W = 31
PEER_HEADS = 8
PEER_NKEYS = 128
PEER_TOPK = 16
EPS = 1e-6
PAGE_SIZE = 128

LANES = 128
SUBLANES = 8
VMEM_LIMIT = 56 * 1024 * 1024
VMEM_LIMIT_PEER = 63 * 1024 * 1024

NEG = -1e30
INT_MIN = -(2 ** 31)
CONV_HALO = 32
SAMPLE_ROWS = 128
ATTN_PAGES_PER_STEP = 32


def _cparams(sem, vmem_limit=VMEM_LIMIT):
    return pltpu.CompilerParams(dimension_semantics=sem, vmem_limit_bytes=vmem_limit)


def _dot_nt(a, b):
    return lax.dot_general(a, b, (((1,), (1,)), ((), ())), preferred_element_type=F32)


def _sortable(x):
    bits = pltpu.bitcast(x, I32)
    return bits ^ ((bits >> 31) & jnp.int32(0x7FFFFFFF))


def _adaln_kernel(c_ref, w_ref, b_ref, o_ref):
    c = c_ref[...]
    a = (c * jax.nn.sigmoid(c)).astype(BF16)
    o_ref[...] = jnp.dot(a, w_ref[...].astype(BF16), preferred_element_type=F32) + b_ref[...]


def _adaln(c, w_ada, b_ada, tn=512):
    r, d = c.shape
    n = w_ada.shape[1]
    return pl.pallas_call(
        _adaln_kernel,
        grid=(n // tn,),
        in_specs=[pl.BlockSpec((r, d), lambda j: (0, 0)),
                  pl.BlockSpec((d, tn), lambda j: (0, j)),
                  pl.BlockSpec((1, tn), lambda j: (0, j))],
        out_specs=pl.BlockSpec((r, tn), lambda j: (0, j)),
        out_shape=jax.ShapeDtypeStruct((r, n), F32),
        compiler_params=_cparams(("arbitrary",)),
        name="adaln",
    )(c, w_ada, b_ada.reshape(1, n))


def _modulate_kernel(x_ref, g_ref, sc_ref, sh_ref, o_ref, *, transposed):
    x = x_ref[0]
    ms = jnp.mean(x * x, axis=-1, keepdims=True)
    y = x * lax.rsqrt(ms + EPS) * g_ref[...]
    y = y * (1.0 + sc_ref[0]) + sh_ref[0]
    if transposed:
        o_ref[...] = y.T.astype(o_ref.dtype)
    else:
        o_ref[0] = y.astype(o_ref.dtype)


def _row_mod_spec(mod, tr):
    d = mod.shape[-1]
    if mod.shape[1] == 1:
        return pl.BlockSpec((1, 1, d), lambda g, r, *_: (g, 0, 0))
    return pl.BlockSpec((1, tr, d), lambda g, r, *_: (g, r, 0))


def _modulate(x3, gain, sc3, sh3, tr, transposed=False):
    g, r, d = x3.shape
    nb = r // tr
    if transposed:
        out_spec = pl.BlockSpec((d, tr), lambda a, b: (0, a * nb + b))
        out_shape = jax.ShapeDtypeStruct((d, g * r), BF16)
    else:
        out_spec = pl.BlockSpec((1, tr, d), lambda a, b: (a, b, 0))
        out_shape = jax.ShapeDtypeStruct((g, r, d), BF16)
    return pl.pallas_call(
        functools.partial(_modulate_kernel, transposed=transposed),
        grid=(g, nb),
        in_specs=[pl.BlockSpec((1, tr, d), lambda a, b: (a, b, 0)),
                  pl.BlockSpec((1, d), lambda a, b: (0, 0)),
                  _row_mod_spec(sc3, tr),
                  _row_mod_spec(sh3, tr)],
        out_specs=out_spec,
        out_shape=out_shape,
        compiler_params=_cparams(("arbitrary", "arbitrary")),
        name="modulate",
    )(x3, gain.reshape(1, d), sc3, sh3)


def _mm_kernel(h_ref, w_ref, o_ref, *, glu, head_major):
    acc = jnp.dot(h_ref[...], w_ref[...], preferred_element_type=F32)
    if glu:
        half = acc.shape[1] // 2
        acc = acc[:, :half] * jax.nn.sigmoid(acc[:, half:])
    if head_major:
        for c in range(acc.shape[1] // LANES):
            o_ref[c] = acc[:, c * LANES:(c + 1) * LANES].astype(o_ref.dtype)
    else:
        o_ref[...] = acc.astype(o_ref.dtype)


def _matmul(h, w, out_dtype, tm, tn, glu=False, head_major=False):
    m, k = h.shape
    n = w.shape[1]
    n_out = n // 2 if glu else n
    tn_out = tn // 2 if glu else tn
    if head_major:
        out_shape = jax.ShapeDtypeStruct((n_out // LANES, m, LANES), out_dtype)
        out_spec = pl.BlockSpec((tn_out // LANES, tm, LANES), lambda i, j: (j, i, 0))
    else:
        out_shape = jax.ShapeDtypeStruct((m, n_out), out_dtype)
        out_spec = pl.BlockSpec((tm, tn_out), lambda i, j: (i, j))
    return pl.pallas_call(
        functools.partial(_mm_kernel, glu=glu, head_major=head_major),
        grid=(m // tm, n // tn),
        in_specs=[pl.BlockSpec((tm, k), lambda i, j: (i, 0)),
                  pl.BlockSpec((k, tn), lambda i, j: (0, j))],
        out_specs=out_spec,
        out_shape=out_shape,
        compiler_params=_cparams(("arbitrary", "arbitrary")),
        name="proj_matmul",
    )(h, w)


def _indexer_kernel(qi_ref, ka_ref, kb_ref, wi_ref, o_ref, keys_ref, cst_ref, *, tq, tk, nk, topk):
    i = pl.program_id(1)
    q0 = i * tq
    nvis = (q0 + tq + tk - 1) // tk
    qpos = q0 + lax.broadcasted_iota(I32, (1, tq), 1)
    w = wi_ref[...]
    score_scale = IDX_DIM ** -0.5 * IDX_HEADS ** -0.5

    def kpos_of(j):
        return j * tk + lax.broadcasted_iota(I32, (tk, 1), 0)

    def score_body(j, carry):
        k0 = pl.multiple_of(j * tk, tk)
        ka = ka_ref[pl.ds(k0, tk), :]
        kb = kb_ref[pl.ds(k0, tk), :]
        acc = jnp.zeros((tk, tq), F32)
        for p in range(IDX_HEADS // 2):
            qp = qi_ref[p]
            sa = _dot_nt(ka, qp)
            sb = _dot_nt(kb, qp)
            acc = acc + jnp.maximum(sa, 0.0) * w[2 * p:2 * p + 1]
            acc = acc + jnp.maximum(sb, 0.0) * w[2 * p + 1:2 * p + 2]
        acc = acc * score_scale
        acc = jnp.where(kpos_of(j) <= qpos, acc, -jnp.inf)
        keys_ref[j] = _sortable(acc)
        return carry

    lax.fori_loop(0, nvis, score_body, 0)

    def count(pred):
        def body(j, c):
            return c + jnp.sum(pred(keys_ref[j], j).reshape(tk // SUBLANES, SUBLANES, tq), axis=0)
        part = lax.fori_loop(0, nvis, body, jnp.zeros((SUBLANES, tq), I32))
        return jnp.sum(part, axis=0, keepdims=True)

    def bit_body(t, thr):
        cand = thr ^ lax.shift_left(jnp.int32(1), 31 - t)
        cnt = count(lambda k, j: jnp.where(k >= cand, 1, 0))
        return jnp.where(cnt >= topk, cand, thr)

    thr = lax.fori_loop(0, 32, bit_body, jnp.full((1, tq), INT_MIN, I32))

    need = topk - count(lambda k, j: jnp.where(k > thr, 1, 0))
    n_eq = count(lambda k, j: jnp.where(k == thr, 1, 0))
    cst_ref[...] = jnp.full((1, tq), nk * tk, I32)

    @pl.when(jnp.max(jnp.where(n_eq > need, 1, 0)) > 0)
    def _():
        def idx_body(t, c):
            cand = c | lax.shift_left(jnp.int32(1), 15 - t)
            f = count(lambda k, j: jnp.where(k == thr, jnp.where(kpos_of(j) < cand, 1, 0), 0))
            return jnp.where(f < need, cand, c)
        cst_ref[...] = lax.fori_loop(0, 16, idx_body, jnp.zeros((1, tq), I32))

    cstar = cst_ref[...]

    def write_body(j, carry):
        k = keys_ref[j]
        kpos = kpos_of(j)
        sel = jnp.where(k > thr, 0.0, jnp.where(k == thr, jnp.where(kpos <= cstar, 0.0, NEG), NEG))
        o_ref[0, j] = jnp.where(kpos <= qpos, sel, NEG).astype(o_ref.dtype)
        return carry

    lax.fori_loop(0, nvis, write_body, 0)

    def fill_body(j, carry):
        o_ref[0, j] = jnp.full((tk, tq), NEG, o_ref.dtype)
        return carry

    lax.fori_loop(nvis, nk, fill_body, 0)


def _indexer(qh, kvb, wi_t, batch, seq, tq, tk, qi_blk, ka_blk):
    nq, nk = seq // tq, seq // tk
    topk = min(TOPK_MAX, seq // 4)
    return pl.pallas_call(
        functools.partial(_indexer_kernel, tq=tq, tk=tk, nk=nk, topk=topk),
        grid=(batch, nq),
        in_specs=[pl.BlockSpec((IDX_HEADS // 2, tq, LANES), lambda b, i: (qi_blk, b * nq + i, 0)),
                  pl.BlockSpec((seq, LANES), lambda b, i: (b, ka_blk)),
                  pl.BlockSpec((seq, LANES), lambda b, i: (b, ka_blk + 1)),
                  pl.BlockSpec((IDX_HEADS, tq), lambda b, i: (0, b * nq + i))],
        out_specs=pl.BlockSpec((1, nk, tk, tq), lambda b, i: (b * nq + i, 0, 0, 0)),
        out_shape=jax.ShapeDtypeStruct((batch * nq, nk, tk, tq), BF16),
        scratch_shapes=[pltpu.VMEM((nk, tk, tq), I32), pltpu.VMEM((1, tq), I32)],
        compiler_params=_cparams(("arbitrary", "arbitrary")),
        name="indexer_topk_mask",
    )(qh, kvb, kvb, wi_t)


def _attn_kernel(q_ref, k_ref, vt_ref, m_ref, bt_ref, o_ref, acc_ref, mx_ref, l_ref, *, tq, tk):
    i = pl.program_id(1)
    exp2_scale = HEAD_DIM ** -0.5 * math.log2(math.e)
    mx_ref[...] = jnp.full(mx_ref.shape, NEG, F32)
    l_ref[...] = jnp.zeros(l_ref.shape, F32)
    acc_ref[...] = jnp.zeros(acc_ref.shape, F32)
    zero_blk = jnp.zeros((LANES, LANES), F32)

    def bias_tile(g, near):
        cols = []
        for r in range(KV_GROUP):
            h = g * KV_GROUP + r
            b0 = bt_ref[0, h]
            b1 = bt_ref[1, h]
            if near == 0:
                top = jnp.concatenate([b0, b1], axis=1)
                bot = jnp.concatenate([zero_blk, b0], axis=1)
            else:
                top = jnp.concatenate([zero_blk, zero_blk], axis=1)
                bot = jnp.concatenate([b1, zero_blk], axis=1)
            cols.append(jnp.concatenate([top, bot], axis=0))
        return jnp.concatenate(cols, axis=1)

    def update(j, near):
        k0 = pl.multiple_of(j * tk, tk)
        mt = m_ref[0, j].astype(F32)
        mt4 = jnp.concatenate([mt] * KV_GROUP, axis=1)
        for g in range(N_KV_HEADS):
            kt = k_ref[pl.ds(k0, tk), g * HEAD_DIM:(g + 1) * HEAD_DIM]
            qs = q_ref[g * KV_GROUP:(g + 1) * KV_GROUP].reshape(KV_GROUP * tq, HEAD_DIM)
            s = _dot_nt(kt, qs) + mt4
            if near is not None:
                s = s + bias_tile(g, near)
            m_old = mx_ref[g]
            m_new = jnp.maximum(m_old, jnp.max(s, axis=0, keepdims=True))
            alpha = jnp.exp2((m_old - m_new) * exp2_scale)
            p = jnp.exp2((s - m_new) * exp2_scale)
            l_ref[g] = alpha * l_ref[g] + jnp.sum(p, axis=0, keepdims=True)
            vt = vt_ref[0, j, g * HEAD_DIM:(g + 1) * HEAD_DIM, :]
            pv = jnp.dot(vt, p.astype(BF16), preferred_element_type=F32)
            acc_ref[g] = alpha * acc_ref[g] + pv
            mx_ref[g] = m_new

    def far_body(j, carry):
        update(j, None)
        return carry

    lax.fori_loop(0, jnp.maximum(i - 1, 0), far_body, 0)

    @pl.when(i >= 1)
    def _():
        update(i - 1, 1)

    update(i, 0)

    for g in range(N_KV_HEADS):
        o = acc_ref[g] / l_ref[g]
        for r in range(KV_GROUP):
            h = g * KV_GROUP + r
            o_ref[:, h * HEAD_DIM:(h + 1) * HEAD_DIM] = o[:, r * tq:(r + 1) * tq].T.astype(o_ref.dtype)


def _attention(qh, kvb, vt_tiles, mask, bias_t, batch, seq, tq, tk):
    nq, nk = seq // tq, seq // tk
    return pl.pallas_call(
        functools.partial(_attn_kernel, tq=tq, tk=tk),
        grid=(batch, nq),
        in_specs=[pl.BlockSpec((N_HEADS, tq, HEAD_DIM), lambda b, i: (0, b * nq + i, 0)),
                  pl.BlockSpec((seq, KV_WIDTH), lambda b, i: (b, 0)),
                  pl.BlockSpec((1, nk, KV_WIDTH, tk), lambda b, i: (b, 0, 0, 0)),
                  pl.BlockSpec((1, nk, tk, tq), lambda b, i: (b * nq + i, 0, 0, 0)),
                  pl.BlockSpec((2, N_HEADS, LANES, LANES), lambda b, i: (0, 0, 0, 0))],
        out_specs=pl.BlockSpec((tq, ATT_WIDTH), lambda b, i: (b * nq + i, 0)),
        out_shape=jax.ShapeDtypeStruct((batch * seq, ATT_WIDTH), BF16),
        scratch_shapes=[pltpu.VMEM((N_KV_HEADS, HEAD_DIM, KV_GROUP * tq), F32),
                        pltpu.VMEM((N_KV_HEADS, 1, KV_GROUP * tq), F32),
                        pltpu.VMEM((N_KV_HEADS, 1, KV_GROUP * tq), F32)],
        compiler_params=_cparams(("arbitrary", "arbitrary")),
        name="masked_attention",
    )(qh, kvb, vt_tiles, mask, bias_t)


def _conv_kernel(halo_ref, cur_ref, w_ref, b_ref, g_ref, bb_ref, o_ref, ext_ref, ph_ref, *, tt, tc, rc, zero_first):
    t = pl.program_id(1)
    halo = halo_ref[0]
    if zero_first:
        halo = jnp.where(t == 0, 0.0, halo)
    ext_ref[0:CONV_HALO, :] = halo
    ext_ref[CONV_HALO:CONV_HALO + tt, :] = cur_ref[0]
    first = CONV_HALO - (CONV_W - 1)
    ph_rows = ph_ref.shape[1]
    for p in range(1, SUBLANES):
        ph_ref[p - 1] = ext_ref[p:p + ph_rows, :]

    def tap_rows(start, cs):
        p, base = start % SUBLANES, start - start % SUBLANES
        if p == 0:
            return ext_ref[base:base + rc, cs]
        return ph_ref[p - 1, base:base + rc, cs]

    for c in range(tc // LANES):
        cs = slice(c * LANES, (c + 1) * LANES)
        for r in range(tt // rc):
            acc = jnp.zeros((rc, LANES), F32) + b_ref[:, cs]
            for j in range(CONV_W):
                acc = acc + w_ref[j:j + 1, cs] * tap_rows(r * rc + first + j, cs)
            mu = jnp.mean(acc, axis=-1, keepdims=True)
            dv = acc - mu
            var = jnp.mean(dv * dv, axis=-1, keepdims=True)
            yn = dv * lax.rsqrt(var + EPS) * g_ref[:, cs] + bb_ref[:, cs]
            o_ref[0, r * rc:(r + 1) * rc, cs] = (yn * jax.nn.sigmoid(yn)).astype(o_ref.dtype)


def _conv(halo_src, cur, conv_w, conv_b, cn_g, cn_b, tt, tc, zero_first):
    b, t, c = cur.shape
    hb = tt // CONV_HALO
    if zero_first:
        halo_spec = pl.BlockSpec((1, CONV_HALO, tc), lambda a, i, j: (a, jnp.maximum(i * hb - 1, 0), j))
    else:
        halo_spec = pl.BlockSpec((1, CONV_HALO, tc), lambda a, i, j: (a, 0, j))
    vec = lambda: pl.BlockSpec((1, tc), lambda a, i, j: (0, j))
    return pl.pallas_call(
        functools.partial(_conv_kernel, tt=tt, tc=tc, rc=min(tt, 64), zero_first=zero_first),
        grid=(b, t // tt, c // tc),
        in_specs=[halo_spec,
                  pl.BlockSpec((1, tt, tc), lambda a, i, j: (a, i, j)),
                  pl.BlockSpec((CONV_W, tc), lambda a, i, j: (0, j)),
                  vec(), vec(), vec()],
        out_specs=pl.BlockSpec((1, tt, tc), lambda a, i, j: (a, i, j)),
        out_shape=jax.ShapeDtypeStruct((b, t, c), BF16),
        scratch_shapes=[pltpu.VMEM((CONV_HALO + tt, tc), F32),
                        pltpu.VMEM((SUBLANES - 1, CONV_HALO + tt - SUBLANES, tc), F32)],
        compiler_params=_cparams(("arbitrary", "arbitrary", "arbitrary")),
        name="conformer_conv",
    )(halo_src, cur, conv_w, conv_b.reshape(1, c), cn_g.reshape(1, c), cn_b.reshape(1, c))


def _outproj_kernel(a_ref, c_ref, wa_ref, wc_ref, x_ref, gt_ref, o_ref):
    acc = jnp.dot(a_ref[0], wa_ref[...], preferred_element_type=F32)
    acc = acc + jnp.dot(c_ref[0], wc_ref[...], preferred_element_type=F32)
    o_ref[0] = x_ref[0] + gt_ref[0] * acc


def _outproj(att3, conv3, wo_a, wo_c, x3, gt3, tm, tn):
    g, r, d = x3.shape
    ka, kc = att3.shape[-1], conv3.shape[-1]
    gt_spec = (pl.BlockSpec((1, 1, tn), lambda a, i, j: (a, 0, j)) if gt3.shape[1] == 1
               else pl.BlockSpec((1, tm, tn), lambda a, i, j: (a, i, j)))
    return pl.pallas_call(
        _outproj_kernel,
        grid=(g, r // tm, d // tn),
        in_specs=[pl.BlockSpec((1, tm, ka), lambda a, i, j: (a, i, 0)),
                  pl.BlockSpec((1, tm, kc), lambda a, i, j: (a, i, 0)),
                  pl.BlockSpec((ka, tn), lambda a, i, j: (0, j)),
                  pl.BlockSpec((kc, tn), lambda a, i, j: (0, j)),
                  pl.BlockSpec((1, tm, tn), lambda a, i, j: (a, i, j)),
                  gt_spec],
        out_specs=pl.BlockSpec((1, tm, tn), lambda a, i, j: (a, i, j)),
        out_shape=jax.ShapeDtypeStruct((g, r, d), F32),
        compiler_params=_cparams(("arbitrary", "arbitrary", "arbitrary")),
        name="outproj_residual",
    )(att3, conv3, wo_a, wo_c, x3, gt3)


def _peer_scores_kernel(ht_ref, wqt_ref, sk_ref, o_ref):
    q_t = jnp.dot(wqt_ref[...], ht_ref[...], preferred_element_type=F32).astype(BF16)
    half = q_t.shape[0] // 2
    o_ref[0:PEER_NKEYS, :] = jnp.dot(sk_ref[0, 0], q_t[:half], preferred_element_type=F32)
    o_ref[PEER_NKEYS:2 * PEER_NKEYS, :] = jnp.dot(sk_ref[0, 1], q_t[half:], preferred_element_type=F32)


def _peer_scores(h2t, wq_t, sk, tm):
    d, n = h2t.shape
    dk = wq_t.shape[0] // PEER_HEADS
    return pl.pallas_call(
        _peer_scores_kernel,
        grid=(n // tm, PEER_HEADS),
        in_specs=[pl.BlockSpec((d, tm), lambda i, h: (0, i)),
                  pl.BlockSpec((dk, d), lambda i, h: (h, 0)),
                  pl.BlockSpec((1, 2, PEER_NKEYS, dk // 2), lambda i, h: (h, 0, 0, 0))],
        out_specs=pl.BlockSpec((2 * PEER_NKEYS, tm), lambda i, h: (h, i)),
        out_shape=jax.ShapeDtypeStruct((PEER_HEADS * 2 * PEER_NKEYS, n), F32),
        compiler_params=_cparams(("arbitrary", "arbitrary")),
        name="peer_subkey_scores",
    )(h2t, wq_t, sk)


def _top16(x):
    rows = x.shape[0]
    rid = lax.broadcasted_iota(I32, x.shape, 0).astype(F32)
    vals = []
    for k in range(PEER_TOPK):
        m = jnp.max(x, axis=0, keepdims=True)
        vals.append(m)
        if k + 1 < PEER_TOPK:
            first = jnp.min(jnp.where(x == m, rid, float(rows)), axis=0, keepdims=True)
            x = jnp.where(rid == first, -jnp.inf, x)
    return vals


def _stack_rows(rows):
    shape = (len(rows), rows[0].shape[1])
    rid = lax.broadcasted_iota(I32, shape, 0)
    out = jnp.zeros(shape, rows[0].dtype)
    for k, row in enumerate(rows):
        out = jnp.where(rid == k, row, out)
    return out


def _peer_route_kernel(s_ref, s1m_ref, s2m_ref, c1_ref, e2_ref, thr_ref):
    s1 = s_ref[0:PEER_NKEYS, :]
    s2 = s_ref[PEER_NKEYS:2 * PEER_NKEYS, :]
    t1 = _top16(s1)
    t2 = _top16(s2)
    t2_all = _stack_rows(t2)
    blocks = [t1[0] + t2_all]
    for a in range(1, 8):
        blocks.append(t1[a] + t2_all[0:8])
    blocks.append(_stack_rows(t1[8:16]) + t2[0])
    top = _top16(jnp.concatenate(blocks, axis=0))
    z = jnp.ones_like(top[0])
    for k in range(1, PEER_TOPK):
        z = z + jnp.exp(top[k] - top[0])
    s1m_ref[...] = jnp.where(s1 >= t1[PEER_TOPK - 1], s1, -jnp.inf)
    s2m_ref[...] = jnp.where(s2 >= t2[PEER_TOPK - 1], s2, -jnp.inf)
    c1_ref[...] = jnp.exp(s1 - t1[0]) / z
    e2_ref[...] = jnp.exp(s2 - t2[0])
    thr_ref[0] = top[PEER_TOPK - 1]


def _peer_route(s_t, tl):
    rows, n = s_t.shape
    big = lambda: pl.BlockSpec((PEER_NKEYS, tl), lambda h, t: (h, t))
    big_shape = jax.ShapeDtypeStruct((PEER_HEADS * PEER_NKEYS, n), F32)
    return pl.pallas_call(
        _peer_route_kernel,
        grid=(PEER_HEADS, n // tl),
        in_specs=[pl.BlockSpec((2 * PEER_NKEYS, tl), lambda h, t: (h, t))],
        out_specs=[big(), big(), big(), big(), pl.BlockSpec((1, 1, tl), lambda h, t: (h, 0, t))],
        out_shape=[big_shape, big_shape, big_shape, big_shape,
                   jax.ShapeDtypeStruct((PEER_HEADS, 1, n), F32)],
        compiler_params=_cparams(("arbitrary", "arbitrary")),
        name="peer_route",
    )(s_t)


GATE_ROWS = 64


MXU_COLS = 256


def _peer_dense_kernel(ht_ref, u_ref, v_ref, s1_ref, c1_ref, s2_ref, e2_ref, thr_ref, o_ref, *, te):
    e = pl.program_id(1)
    tm = ht_ref.shape[1]
    rows_per_tile = te // PEER_NKEYS

    @pl.when(e == 0)
    def _():
        o_ref[...] = jnp.zeros(o_ref.shape, F32)

    chunk = min(MXU_COLS, tm)
    a_chunks = [jnp.dot(u_ref[...], ht_ref[:, c * chunk:(c + 1) * chunk], preferred_element_type=F32)
                for c in range(tm // chunk)]

    s1rows = [[s1_ref[pl.ds(h * PEER_NKEYS + e * rows_per_tile + r, 1), :] for r in range(rows_per_tile)]
              for h in range(PEER_HEADS)]
    c1rows = [[c1_ref[pl.ds(h * PEER_NKEYS + e * rows_per_tile + r, 1), :] for r in range(rows_per_tile)]
              for h in range(PEER_HEADS)]
    for c in range(tm // chunk):
        w_cols = []
        for lc in range(chunk // LANES):
            ls = slice(c * chunk + lc * LANES, c * chunk + (lc + 1) * LANES)
            als = slice(lc * LANES, (lc + 1) * LANES)
            blocks = [[None] * (PEER_NKEYS // GATE_ROWS) for _ in range(rows_per_tile)]
            for part in range(PEER_NKEYS // GATE_ROWS):
                rs = [slice(r * PEER_NKEYS + part * GATE_ROWS, r * PEER_NKEYS + (part + 1) * GATE_ROWS)
                      for r in range(rows_per_tile)]
                acc = [jnp.zeros((GATE_ROWS, LANES), F32) for _ in range(rows_per_tile)]
                for h in range(PEER_HEADS):
                    row0 = h * PEER_NKEYS + part * GATE_ROWS
                    s2 = s2_ref[row0:row0 + GATE_ROWS, ls]
                    e2 = e2_ref[row0:row0 + GATE_ROWS, ls]
                    thr = thr_ref[h, :, ls]
                    for r in range(rows_per_tile):
                        cand = s2 + s1rows[h][r][:, ls]
                        acc[r] = acc[r] + jnp.where(cand >= thr, e2, 0.0) * c1rows[h][r][:, ls]
                for r in range(rows_per_tile):
                    blocks[r][part] = (acc[r] * jax.nn.gelu(a_chunks[c][rs[r], als])).astype(BF16)
            w_cols.append(jnp.concatenate([b for row in blocks for b in row], axis=0))
        cs = slice(c * chunk, (c + 1) * chunk)
        o_ref[:, cs] += lax.dot_general(v_ref[...], jnp.concatenate(w_cols, axis=1),
                                        (((0,), (0,)), ((), ())), preferred_element_type=F32)


def _peer_dense(h2t, u_bf, v_bf, s1m, c1, s2m, e2, thr, tm, te):
    d, n = h2t.shape
    rows = PEER_HEADS * PEER_NKEYS
    once = pl.Buffered(1)
    tok = lambda: pl.BlockSpec((rows, tm), lambda i, e: (0, i), pipeline_mode=once)
    return pl.pallas_call(
        functools.partial(_peer_dense_kernel, te=te),
        grid=(n // tm, u_bf.shape[0] // te),
        in_specs=[pl.BlockSpec((d, tm), lambda i, e: (0, i), pipeline_mode=once),
                  pl.BlockSpec((te, d), lambda i, e: (e, 0)),
                  pl.BlockSpec((te, d), lambda i, e: (e, 0)),
                  tok(), tok(), tok(), tok(),
                  pl.BlockSpec((PEER_HEADS, 1, tm), lambda i, e: (0, 0, i), pipeline_mode=once)],
        out_specs=pl.BlockSpec((d, tm), lambda i, e: (0, i), pipeline_mode=once),
        out_shape=jax.ShapeDtypeStruct((d, n), F32),
        compiler_params=_cparams(("arbitrary", "arbitrary"), VMEM_LIMIT_PEER),
        name="peer_dense_experts",
    )(h2t, u_bf, v_bf, s1m, c1, s2m, e2, thr)


def _final_kernel(x_ref, gt_ref, p_ref, g_ref, o_ref, *, normalize):
    xx = x_ref[0] + gt_ref[0] * p_ref[...].T
    if normalize:
        ms = jnp.mean(xx * xx, axis=-1, keepdims=True)
        xx = xx * lax.rsqrt(ms + EPS) * g_ref[...]
    o_ref[0] = xx


def _final(x3, gt3, peer_t, g_final, tm, normalize):
    g, r, d = x3.shape
    nb = r // tm
    return pl.pallas_call(
        functools.partial(_final_kernel, normalize=normalize),
        grid=(g, nb),
        in_specs=[pl.BlockSpec((1, tm, d), lambda a, i: (a, i, 0)),
                  _row_mod_spec(gt3, tm),
                  pl.BlockSpec((d, tm), lambda a, i: (0, a * nb + i)),
                  pl.BlockSpec((1, d), lambda a, i: (0, 0))],
        out_specs=pl.BlockSpec((1, tm, d), lambda a, i: (a, i, 0)),
        out_shape=jax.ShapeDtypeStruct((g, r, d), F32),
        compiler_params=_cparams(("arbitrary", "arbitrary")),
        name="final_norm",
    )(x3, gt3, peer_t, g_final.reshape(1, d))


def _sample_index_kernel(pt_ref, qi_ref, wi_ref, kself_ref, expand_ref, cki_ref, mask_ref, self_ref,
                         kbuf_ref, sem_ref, sc_ref, scself_ref, *, n_pages, topk, page_base):
    b = pl.program_id(0)
    slot = b % 2
    score_scale = IDX_DIM ** -0.5 * IDX_HEADS ** -0.5

    def page_copy(seq, k, sl):
        page = page_base + pt_ref[seq, k]
        return pltpu.make_async_copy(cki_ref.at[page], kbuf_ref.at[sl, :, pl.ds(k * PAGE_SIZE, PAGE_SIZE)],
                                     sem_ref.at[sl])

    @pl.when(b == 0)
    def _():
        for k in range(n_pages):
            page_copy(0, k, 0).start()

    @pl.when(b + 1 < pl.num_programs(0))
    def _():
        for k in range(n_pages):
            page_copy(b + 1, k, 1 - slot).start()

    for k in range(n_pages):
        page_copy(b, k, slot).wait()

    qi = qi_ref[0]
    wi = wi_ref[0]
    s = jnp.dot(qi, kbuf_ref[slot].astype(BF16), preferred_element_type=F32)
    sc_ref[pl.ds(b, 1), :] = jnp.sum(jnp.maximum(s, 0.0) * wi, axis=0, keepdims=True) * score_scale
    ks = kself_ref[0].astype(BF16).astype(F32)
    s_self = jnp.sum(qi.astype(F32) * ks, axis=1, keepdims=True)
    self_score = jnp.sum(jnp.maximum(s_self, 0.0) * wi, axis=0, keepdims=True) * score_scale
    scself_ref[pl.ds(b, 1), :] = jnp.zeros((1, LANES), F32) + self_score

    @pl.when(b == pl.num_programs(0) - 1)
    def _():
        keys = _sortable(sc_ref[...])
        kself = _sortable(scself_ref[:, 0:1])
        pos = lax.broadcasted_iota(I32, keys.shape, 1)
        self_pos = n_pages * PAGE_SIZE

        def total(x, xs):
            return jnp.sum(x, axis=1, keepdims=True) + xs

        def bit_body(t, thr):
            cand = thr ^ lax.shift_left(jnp.int32(1), 31 - t)
            cnt = total(jnp.where(keys >= cand, 1, 0), jnp.where(kself >= cand, 1, 0))
            return jnp.where(cnt >= topk, cand, thr)

        thr = lax.fori_loop(0, 32, bit_body, jnp.full(kself.shape, INT_MIN, I32))
        need = topk - total(jnp.where(keys > thr, 1, 0), jnp.where(kself > thr, 1, 0))

        def idx_body(t, c):
            cand = c | lax.shift_left(jnp.int32(1), 15 - t)
            f = total(jnp.where(keys == thr, jnp.where(pos < cand, 1, 0), 0),
                      jnp.where(kself == thr, jnp.where(self_pos < cand, 1, 0), 0))
            return jnp.where(f < need, cand, c)

        cstar = lax.fori_loop(0, 16, idx_body, jnp.zeros(kself.shape, I32))
        picked = jnp.where(keys > thr, 1.0, jnp.where(keys == thr, jnp.where(pos <= cstar, 1.0, 0.0), 0.0))
        for k in range(n_pages):
            flags = picked[:, k * PAGE_SIZE:(k + 1) * PAGE_SIZE].astype(BF16)
            rows = jnp.dot(flags, expand_ref[...], preferred_element_type=F32)
            mask_ref[k] = jnp.where(rows > 0.5, 0.0, NEG)
        ssel = jnp.where(kself > thr, 0.0, jnp.where(kself == thr, jnp.where(self_pos <= cstar, 0.0, NEG), NEG))
        self_ref[...] = jnp.zeros(self_ref.shape, F32) + ssel


def _sample_index(page_table, qi3, wi3, kself3, cki_t, page_base):
    db, n_pages = page_table.shape
    topk = min(TOPK_MAX, (n_pages * PAGE_SIZE + 1) // 4)
    page_rows = PAGE_SIZE * N_KV_HEADS
    expand = (jnp.arange(page_rows, dtype=I32)[None, :] // N_KV_HEADS
              == jnp.arange(PAGE_SIZE, dtype=I32)[:, None]).astype(BF16)
    grid_spec = pltpu.PrefetchScalarGridSpec(
        num_scalar_prefetch=1,
        grid=(db,),
        in_specs=[pl.BlockSpec((1, IDX_HEADS, IDX_DIM), lambda b, pt: (b, 0, 0)),
                  pl.BlockSpec((1, IDX_HEADS, 1), lambda b, pt: (b, 0, 0)),
                  pl.BlockSpec((1, 1, IDX_DIM), lambda b, pt: (b, 0, 0)),
                  pl.BlockSpec((PAGE_SIZE, page_rows), lambda b, pt: (0, 0)),
                  pl.BlockSpec(memory_space=pl.ANY)],
        out_specs=[pl.BlockSpec((n_pages, db, page_rows), lambda b, pt: (0, 0, 0)),
                   pl.BlockSpec((db, LANES), lambda b, pt: (0, 0))],
        scratch_shapes=[pltpu.VMEM((2, IDX_DIM, n_pages * PAGE_SIZE), F32),
                        pltpu.SemaphoreType.DMA((2,)),
                        pltpu.VMEM((db, n_pages * PAGE_SIZE), F32),
                        pltpu.VMEM((db, LANES), F32)],
    )
    mask_t, selfsel = pl.pallas_call(
        functools.partial(_sample_index_kernel, n_pages=n_pages, topk=topk, page_base=page_base),
        grid_spec=grid_spec,
        out_shape=[jax.ShapeDtypeStruct((n_pages, db, page_rows), F32),
                   jax.ShapeDtypeStruct((db, LANES), F32)],
        compiler_params=_cparams(("arbitrary",)),
        name="sample_indexer",
    )(page_table, qi3, wi3, kself3, expand, cki_t)
    return mask_t.transpose(1, 0, 2), jnp.broadcast_to(selfsel[:, None, :], (db, SUBLANES, LANES))


def _sample_attn_kernel(pt_ref, q_ref, kself_ref, vself_ref, mask_ref, self_ref, bias_ref, bself_ref,
                        *rest, n_pages):
    pps = ATTN_PAGES_PER_STEP
    kpages = rest[:pps]
    vpages = rest[pps:2 * pps]
    o_ref, acc_ref, mx_ref, l_ref = rest[2 * pps:]
    step = pl.program_id(1)
    n_steps = n_pages // pps
    page_rows = PAGE_SIZE * N_KV_HEADS
    sm_scale = HEAD_DIM ** -0.5
    q = q_ref[0]
    head_group = lax.broadcasted_iota(I32, (N_HEADS, HEAD_DIM), 0) // KV_GROUP

    def own_group_rows(x_ref):
        out = jnp.zeros((N_HEADS, HEAD_DIM), F32)
        for g in range(N_KV_HEADS):
            out = jnp.where(head_group == g, x_ref[0, g:g + 1, :].astype(BF16).astype(F32), out)
        return out

    @pl.when(step == 0)
    def _():
        logit = jnp.sum(q.astype(F32) * own_group_rows(kself_ref), axis=1, keepdims=True)
        mx_ref[...] = logit * sm_scale + bself_ref[...] + self_ref[0, 0:1, 0:1]
        l_ref[...] = jnp.ones(l_ref.shape, F32)
        acc_ref[...] = own_group_rows(vself_ref)

    kcat = jnp.concatenate([kpages[k][0].astype(BF16) for k in range(pps)], axis=0)
    vcat = jnp.concatenate([vpages[k][0].astype(BF16) for k in range(pps)], axis=0)
    s = _dot_nt(q, kcat) * sm_scale + bias_ref[step]
    s = jnp.concatenate([s[:, k * page_rows:(k + 1) * page_rows] + mask_ref[0, pl.ds(step * pps + k, 1), :]
                         for k in range(pps)], axis=1)
    m_old = mx_ref[...]
    m_new = jnp.maximum(m_old, jnp.max(s, axis=1, keepdims=True))
    alpha = jnp.exp(m_old - m_new)
    p = jnp.exp(s - m_new)
    l_ref[...] = alpha * l_ref[...] + jnp.sum(p, axis=1, keepdims=True)
    acc_ref[...] = alpha * acc_ref[...] + jnp.dot(p.astype(BF16), vcat, preferred_element_type=F32)
    mx_ref[...] = m_new

    @pl.when(step == n_steps - 1)
    def _():
        o_ref[0] = (acc_ref[...] / l_ref[...]).astype(o_ref.dtype)


def _sample_attn(page_table, q3, kself3, vself3, mask, selfsel, bias_steps, bias_self, ck, cv, page_base):
    db, n_pages = page_table.shape
    pps = ATTN_PAGES_PER_STEP
    n_steps = n_pages // pps
    page_rows = PAGE_SIZE * N_KV_HEADS
    step_rows = pps * page_rows

    def page_spec(k):
        return pl.BlockSpec((1, page_rows, HEAD_DIM),
                            lambda b, s, pt: (page_base + pt[b, s * pps + k], 0, 0))

    per_b = lambda shape: pl.BlockSpec((1,) + shape, lambda b, s, pt: (b, 0, 0))
    grid_spec = pltpu.PrefetchScalarGridSpec(
        num_scalar_prefetch=1,
        grid=(db, n_steps),
        in_specs=[per_b((N_HEADS, HEAD_DIM)), per_b((N_KV_HEADS, HEAD_DIM)), per_b((N_KV_HEADS, HEAD_DIM)),
                  per_b((n_pages, page_rows)), per_b((SUBLANES, LANES)),
                  pl.BlockSpec((n_steps, N_HEADS, step_rows), lambda b, s, pt: (0, 0, 0)),
                  pl.BlockSpec((N_HEADS, 1), lambda b, s, pt: (0, 0))]
                 + [page_spec(k) for k in range(pps)] * 2,
        out_specs=per_b((N_HEADS, HEAD_DIM)),
        scratch_shapes=[pltpu.VMEM((N_HEADS, HEAD_DIM), F32),
                        pltpu.VMEM((N_HEADS, 1), F32),
                        pltpu.VMEM((N_HEADS, 1), F32)],
    )
    return pl.pallas_call(
        functools.partial(_sample_attn_kernel, n_pages=n_pages),
        grid_spec=grid_spec,
        out_shape=jax.ShapeDtypeStruct((db, N_HEADS, HEAD_DIM), BF16),
        compiler_params=_cparams(("arbitrary", "arbitrary")),
        name="sample_attention",
    )(page_table, q3, kself3, vself3, mask, selfsel, bias_steps, bias_self,
      *([ck] * pps), *([cv] * pps))


def _rel_bucket(dist):
    n = jnp.maximum(dist, 0)
    max_exact = REL_BUCKETS // 2
    nf = jnp.maximum(n, 1).astype(F32)
    large = max_exact + (jnp.log(nf / max_exact) / math.log(REL_MAX_DIST / max_exact)
                         * (REL_BUCKETS - max_exact)).astype(I32)
    large = jnp.minimum(large, REL_BUCKETS - 1)
    return jnp.where(n < max_exact, n, large)


def _bias_of_dist(rel_bias, dist):
    onehot = (_rel_bucket(dist)[..., None] == jnp.arange(REL_BUCKETS, dtype=I32)).astype(F32)
    return jnp.einsum("...b,bh->...h", onehot, rel_bias.astype(F32), precision=lax.Precision.HIGHEST)


def _prompt_bias_tiles(rel_bias):
    kc = jnp.arange(LANES, dtype=I32)[:, None]
    qr = jnp.arange(LANES, dtype=I32)[None, :]
    far = rel_bias[REL_BUCKETS - 1]
    tiles = []
    for off in (0, LANES):
        dist = off + qr - kc
        t = (_bias_of_dist(rel_bias, dist) - far) * HEAD_DIM ** 0.5
        tiles.append(jnp.where((dist >= 0)[..., None], t, 0.0).transpose(2, 0, 1))
    return jnp.stack(tiles).astype(F32)


def _sample_bias_steps(rel_bias, n_pages):
    past = n_pages * PAGE_SIZE
    bias = _bias_of_dist(rel_bias, past - jnp.arange(past, dtype=I32))
    own = (jnp.arange(N_HEADS, dtype=I32)[:, None] // KV_GROUP) == jnp.arange(N_KV_HEADS, dtype=I32)[None, :]
    rows = jnp.where(own[None], bias[:, :, None], NEG)
    rows = rows.transpose(1, 0, 2).reshape(N_HEADS, n_pages // ATTN_PAGES_PER_STEP, -1)
    return rows.transpose(1, 0, 2).astype(F32)


def _split_in_proj(w_in_l):
    sizes = (ATT_WIDTH, KV_WIDTH, KV_WIDTH, IDX_HEADS * IDX_DIM, IDX_DIM, IDX_HEADS)
    offs = [0]
    for s in sizes:
        offs.append(offs[-1] + s)
    conv_ch = (w_in_l.shape[1] - offs[-1]) // 2
    d = w_in_l.shape[0]
    wq, wk, wv, wqi, wki, wwi = (w_in_l[:, offs[i]:offs[i + 1]] for i in range(6))
    wua = w_in_l[:, offs[-1]:offs[-1] + conv_ch]
    wub = w_in_l[:, offs[-1] + conv_ch:]
    z = lambda n: jnp.zeros((d, n), w_in_l.dtype)
    w_a = jnp.concatenate([wk, wv, wki, z(LANES - IDX_DIM), z(LANES - IDX_DIM), wki,
                           wwi, z(LANES - IDX_HEADS), z(LANES)], axis=1)
    w_b = jnp.concatenate([wq, wqi], axis=1)
    chunk = 256
    parts = []
    for c in range(conv_ch // chunk):
        parts += [wua[:, c * chunk:(c + 1) * chunk], wub[:, c * chunk:(c + 1) * chunk]]
    w_c = jnp.concatenate(parts, axis=1)
    return w_a.astype(BF16), w_b.astype(BF16), w_c.astype(BF16)


COL_K, COL_V, COL_KA, COL_WI = 0, KV_WIDTH, 2 * KV_WIDTH, 2 * KV_WIDTH + 2 * LANES


def _mixer_projections(h, w_a, w_b, w_c, tm):
    z_a = _matmul(h, w_a, F32, tm, 512)
    qh = _matmul(h, w_b, BF16, tm, 512, head_major=True)
    u = _matmul(h, w_c, F32, tm, 512, glu=True)
    return z_a, qh, u


ATTN_TILE = 256
PROJ_TM, PROJ_TN = 1024, 512
NORM_ROWS = 512
FINAL_ROWS = 256
CONV_TT, CONV_TC = 256, 512
PEER_TM, PEER_TE = 512, 1024
PEER_TE_DECODE = 1024
ROUTE_LANES = 512


def _peer_block(h2t, wq_t, sk_bf, u_bf, v_bf, tm, te):
    s_t = _peer_scores(h2t, wq_t, sk_bf, tm)
    s1m, s2m, c1, e2, thr = _peer_route(s_t, min(tm, ROUTE_LANES))
    return _peer_dense(h2t, u_bf, v_bf, s1m, c1, s2m, e2, thr, tm, te)


def kernel(x_prompt, x_sample, cache_k, cache_v, cache_kidx, state_conv, page_table, c_prompt, c_sample,
           rel_bias, w_ada, b_ada, g_mix, w_in, conv_w, conv_b, cn_g, cn_b, w_o, g_ch, peer_wq,
           peer_subkeys, peer_u, peer_v, g_final):
    batch, seq, d = x_prompt.shape
    db = x_sample.shape[0]
    depth = w_ada.shape[0]
    n_pages = page_table.shape[1]
    conv_ch = conv_w.shape[-1]
    n_prompt = batch * seq
    tq = tk = ATTN_TILE

    xp = x_prompt
    xs = jnp.pad(x_sample.reshape(1, db, d), ((0, 0), (0, SAMPLE_ROWS - db), (0, 0)))
    c_rows = batch + db
    c_pad = (-c_rows) % 16
    c_all = jnp.pad(jnp.concatenate([c_prompt, c_sample], axis=0), ((0, c_pad), (0, 0)))
    bias_t = _prompt_bias_tiles(rel_bias)
    bias_steps = _sample_bias_steps(rel_bias, n_pages)
    n_pool = cache_k.shape[1]
    page_rows = PAGE_SIZE * N_KV_HEADS
    ck_rows = cache_k.reshape(depth * n_pool, page_rows, HEAD_DIM)
    cv_rows = cache_v.reshape(depth * n_pool, page_rows, HEAD_DIM)
    cki_t = jnp.swapaxes(cache_kidx, 2, 3).reshape(depth * n_pool, IDX_DIM, PAGE_SIZE)
    bias_self = rel_bias[_rel_bucket(jnp.zeros((), I32))].reshape(N_HEADS, 1)

    outs = {k: [] for k in ("kp", "vp", "kip", "cp", "ks", "vs", "kis", "cs")}
    for l in range(depth):
        mods = _adaln(c_all, w_ada[l], b_ada[l])
        p_mod = [m.reshape(batch, 1, d) for m in jnp.split(mods[:batch], 6, axis=-1)]
        s_mod = [jnp.pad(m.reshape(1, db, d), ((0, 0), (0, SAMPLE_ROWS - db), (0, 0)))
                 for m in jnp.split(mods[batch:c_rows], 6, axis=-1)]
        w_a, w_b, w_c = _split_in_proj(w_in[l])
        wo_a = w_o[l][:ATT_WIDTH].astype(BF16)
        wo_c = w_o[l][ATT_WIDTH:].astype(BF16)

        hp = _modulate(xp, g_mix[l], p_mod[1], p_mod[0], NORM_ROWS).reshape(n_prompt, d)
        z_a, qh, u = _mixer_projections(hp, w_a, w_b, w_c, PROJ_TM)
        kvb = z_a[:, :COL_WI].astype(BF16)
        wi_t = z_a[:, COL_WI:COL_WI + IDX_HEADS].T
        vt_tiles = (kvb[:, COL_V:COL_V + KV_WIDTH].reshape(batch, seq // tk, tk, KV_WIDTH)
                    .transpose(0, 1, 3, 2))
        mask = _indexer(qh, kvb, wi_t, batch, seq, tq, tk, qi_blk=N_HEADS // (IDX_HEADS // 2),
                        ka_blk=COL_KA // LANES)
        att = _attention(qh, kvb, vt_tiles, mask, bias_t, batch, seq, tq, tk)
        u3 = u.reshape(batch, seq, conv_ch)
        conv = _conv(u3, u3, conv_w[l], conv_b[l], cn_g[l], cn_b[l], CONV_TT, CONV_TC, zero_first=True)
        xp = _outproj(att.reshape(batch, seq, ATT_WIDTH), conv, wo_a, wo_c, xp, p_mod[2], PROJ_TM, PROJ_TN)
        outs["kp"].append(z_a[:, COL_K:COL_K + KV_WIDTH].reshape(batch, seq, N_KV_HEADS, HEAD_DIM))
        outs["vp"].append(z_a[:, COL_V:COL_V + KV_WIDTH].reshape(batch, seq, N_KV_HEADS, HEAD_DIM))
        outs["kip"].append(z_a[:, COL_KA:COL_KA + IDX_DIM].reshape(batch, seq, IDX_DIM))
        outs["cp"].append(u3[:, seq - (CONV_W - 1):])

        hs = _modulate(xs, g_mix[l], s_mod[1], s_mod[0], SAMPLE_ROWS).reshape(SAMPLE_ROWS, d)
        zs_a, qhs, us = _mixer_projections(hs, w_a, w_b, w_c, SAMPLE_ROWS)
        k_new = zs_a[:db, COL_K:COL_K + KV_WIDTH]
        v_new = zs_a[:db, COL_V:COL_V + KV_WIDTH]
        ki_new = zs_a[:db, COL_KA:COL_KA + IDX_DIM]
        wi_new = zs_a[:db, COL_WI:COL_WI + IDX_HEADS]
        q_s = qhs[:N_HEADS, :db].transpose(1, 0, 2)
        qi_s = (qhs[N_HEADS:, :db].transpose(1, 0, 2)
                .reshape(db, IDX_HEADS // 2, 2, IDX_DIM).reshape(db, IDX_HEADS, IDX_DIM))
        smask, sself = _sample_index(page_table, qi_s, wi_new.reshape(db, IDX_HEADS, 1),
                                     ki_new.reshape(db, 1, IDX_DIM), cki_t, l * n_pool)
        att_s = _sample_attn(page_table, q_s, k_new.reshape(db, N_KV_HEADS, HEAD_DIM),
                             v_new.reshape(db, N_KV_HEADS, HEAD_DIM), smask, sself, bias_steps, bias_self,
                             ck_rows, cv_rows, l * n_pool)
        att_s = jnp.pad(att_s.reshape(1, db, ATT_WIDTH), ((0, 0), (0, SAMPLE_ROWS - db), (0, 0)))
        u_new = us[:db]
        state = state_conv[l].astype(F32)
        halo = jnp.pad(state, ((0, 0), (CONV_HALO - (CONV_W - 1), 0), (0, 0)))
        cur = jnp.pad(u_new.reshape(db, 1, conv_ch), ((0, 0), (0, SUBLANES - 1), (0, 0)))
        conv_s = _conv(halo, cur, conv_w[l], conv_b[l], cn_g[l], cn_b[l], SUBLANES, CONV_TC, zero_first=False)
        conv_s = jnp.pad(conv_s[:, 0].reshape(1, db, conv_ch), ((0, 0), (0, SAMPLE_ROWS - db), (0, 0)))
        xs = _outproj(att_s, conv_s, wo_a, wo_c, xs, s_mod[2], SAMPLE_ROWS, PROJ_TN)
        outs["ks"].append(k_new.reshape(db, 1, N_KV_HEADS, HEAD_DIM))
        outs["vs"].append(v_new.reshape(db, 1, N_KV_HEADS, HEAD_DIM))
        outs["kis"].append(ki_new.reshape(db, 1, IDX_DIM))
        outs["cs"].append(jnp.concatenate([state[:, 1:], u_new.reshape(db, 1, conv_ch)], axis=1))

        wq_t = peer_wq[l].astype(BF16).T
        sk_bf = peer_subkeys[l].astype(BF16)
        u_bf = peer_u[l].astype(BF16)
        v_bf = peer_v[l].astype(BF16)
        hp2 = _modulate(xp, g_ch[l], p_mod[4], p_mod[3], NORM_ROWS, transposed=True)
        peer_p = _peer_block(hp2, wq_t, sk_bf, u_bf, v_bf, PEER_TM, PEER_TE)
        hs2 = _modulate(xs, g_ch[l], s_mod[4], s_mod[3], SAMPLE_ROWS, transposed=True)
        peer_s = _peer_block(hs2, wq_t, sk_bf, u_bf, v_bf, SAMPLE_ROWS, PEER_TE_DECODE)
        last = l == depth - 1
        xp = _final(xp, p_mod[5], peer_p, g_final, FINAL_ROWS, normalize=last)
        xs = _final(xs, s_mod[5], peer_s, g_final, SAMPLE_ROWS, normalize=last)

    st = lambda k: jnp.stack(outs[k])
    y_sample = xs[0, :db].reshape(db, 1, d)
    return (xp, y_sample, st("kp"), st("vp"), st("kip"), st("cp"),
            st("ks"), st("vs"), st("kis"), st("cs"))
```

```python
import functools
import math

import jax
import jax.numpy as jnp
from jax import lax
from jax.experimental import pallas as pl
from jax.experimental.pallas import tpu as pltpu

F32 = jnp.float32
BF16 = jnp.bfloat16
I32 = jnp.int32

HEAD_DIM = 128
N_HEADS = 16
N_KV_HEADS = 4
KV_GROUP = N_HEADS // N_KV_HEADS
ATT_WIDTH = N_HEADS * HEAD_DIM
KV_WIDTH = N_KV_HEADS * HEAD_DIM
IDX_HEADS = 16
IDX_DIM = 64
TOPK_MAX = 256
REL_BUCKETS = 32
REL_MAX_DIST = 128
CONV_W = 31
PEER_HEADS = 8
PEER_NKEYS = 128
PEER_TOPK = 16
EPS = 1e-6
PAGE_SIZE = 128

LANES = 128
SUBLANES = 8
VMEM_LIMIT = 56 * 1024 * 1024
VMEM_LIMIT_PEER = 63 * 1024 * 1024

NEG = -1e30
INT_MIN = -(2 ** 31)
CONV_HALO = 32
SAMPLE_ROWS = 128
ATTN_PAGES_PER_STEP = 32


def _cparams(sem, vmem_limit=VMEM_LIMIT):
    return pltpu.CompilerParams(dimension_semantics=sem, vmem_limit_bytes=vmem_limit)


def _dot_nt(a, b):
    return lax.dot_general(a, b, (((1,), (1,)), ((), ())), preferred_element_type=F32)


def _sortable(x):
    bits = pltpu.bitcast(x, I32)
    return bits ^ ((bits >> 31) & jnp.int32(0x7FFFFFFF))


def _adaln_kernel(c_ref, w_ref, b_ref, o_ref):
    c = c_ref[...]
    a = (c * jax.nn.sigmoid(c)).astype(BF16)
    o_ref[...] = jnp.dot(a, w_ref[...].astype(BF16), preferred_element_type=F32) + b_ref[...]


def _adaln(c, w_ada, b_ada, tn=512):
    r, d = c.shape
    n = w_ada.shape[1]
    return pl.pallas_call(
        _adaln_kernel,
        grid=(n // tn,),
        in_specs=[pl.BlockSpec((r, d), lambda j: (0, 0)),
                  pl.BlockSpec((d, tn), lambda j: (0, j)),
                  pl.BlockSpec((1, tn), lambda j: (0, j))],
        out_specs=pl.BlockSpec((r, tn), lambda j: (0, j)),
        out_shape=jax.ShapeDtypeStruct((r, n), F32),
        compiler_params=_cparams(("arbitrary",)),
        name="adaln",
    )(c, w_ada, b_ada.reshape(1, n))


def _modulate_kernel(x_ref, g_ref, sc_ref, sh_ref, o_ref, *, transposed):
    x = x_ref[0]
    ms = jnp.mean(x * x, axis=-1, keepdims=True)
    y = x * lax.rsqrt(ms + EPS) * g_ref[...]
    y = y * (1.0 + sc_ref[0]) + sh_ref[0]
    if transposed:
        o_ref[...] = y.T.astype(o_ref.dtype)
    else:
        o_ref[0] = y.astype(o_ref.dtype)


def _row_mod_spec(mod, tr):
    d = mod.shape[-1]
    if mod.shape[1] == 1:
        return pl.BlockSpec((1, 1, d), lambda g, r, *_: (g, 0, 0))
    return pl.BlockSpec((1, tr, d), lambda g, r, *_: (g, r, 0))


def _modulate(x3, gain, sc3, sh3, tr, transposed=False):
    g, r, d = x3.shape
    nb = r // tr
    if transposed:
        out_spec = pl.BlockSpec((d, tr), lambda a, b: (0, a * nb + b))
        out_shape = jax.ShapeDtypeStruct((d, g * r), BF16)
    else:
        out_spec = pl.BlockSpec((1, tr, d), lambda a, b: (a, b, 0))
        out_shape = jax.ShapeDtypeStruct((g, r, d), BF16)
    return pl.pallas_call(
        functools.partial(_modulate_kernel, transposed=transposed),
        grid=(g, nb),
        in_specs=[pl.BlockSpec((1, tr, d), lambda a, b: (a, b, 0)),
                  pl.BlockSpec((1, d), lambda a, b: (0, 0)),
                  _row_mod_spec(sc3, tr),
                  _row_mod_spec(sh3, tr)],
        out_specs=out_spec,
        out_shape=out_shape,
        compiler_params=_cparams(("arbitrary", "arbitrary")),
        name="modulate",
    )(x3, gain.reshape(1, d), sc3, sh3)


def _mm_kernel(h_ref, w_ref, o_ref, *, glu, head_major):
    acc = jnp.dot(h_ref[...], w_ref[...], preferred_element_type=F32)
    if glu:
        half = acc.shape[1] // 2
        acc = acc[:, :half] * jax.nn.sigmoid(acc[:, half:])
    if head_major:
        for c in range(acc.shape[1] // LANES):
            o_ref[c] = acc[:, c * LANES:(c + 1) * LANES].astype(o_ref.dtype)
    else:
        o_ref[...] = acc.astype(o_ref.dtype)


def _matmul(h, w, out_dtype, tm, tn, glu=False, head_major=False):
    m, k = h.shape
    n = w.shape[1]
    n_out = n // 2 if glu else n
    tn_out = tn // 2 if glu else tn
    if head_major:
        out_shape = jax.ShapeDtypeStruct((n_out // LANES, m, LANES), out_dtype)
        out_spec = pl.BlockSpec((tn_out // LANES, tm, LANES), lambda i, j: (j, i, 0))
    else:
        out_shape = jax.ShapeDtypeStruct((m, n_out), out_dtype)
        out_spec = pl.BlockSpec((tm, tn_out), lambda i, j: (i, j))
    return pl.pallas_call(
        functools.partial(_mm_kernel, glu=glu, head_major=head_major),
        grid=(m // tm, n // tn),
        in_specs=[pl.BlockSpec((tm, k), lambda i, j: (i, 0)),
                  pl.BlockSpec((k, tn), lambda i, j: (0, j))],
        out_specs=out_spec,
        out_shape=out_shape,
        compiler_params=_cparams(("arbitrary", "arbitrary")),
        name="proj_matmul",
    )(h, w)


def _indexer_kernel(qi_ref, ka_ref, kb_ref, wi_ref, o_ref, keys_ref, cst_ref, *, tq, tk, nk, topk):
    i = pl.program_id(1)
    q0 = i * tq
    nvis = (q0 + tq + tk - 1) // tk
    qpos = q0 + lax.broadcasted_iota(I32, (1, tq), 1)
    w = wi_ref[...]
    score_scale = IDX_DIM ** -0.5 * IDX_HEADS ** -0.5

    def kpos_of(j):
        return j * tk + lax.broadcasted_iota(I32, (tk, 1), 0)

    def score_body(j, carry):
        k0 = pl.multiple_of(j * tk, tk)
        ka = ka_ref[pl.ds(k0, tk), :]
        kb = kb_ref[pl.ds(k0, tk), :]
        acc = jnp.zeros((tk, tq), F32)
        for p in range(IDX_HEADS // 2):
            qp = qi_ref[p]
            sa = _dot_nt(ka, qp)
            sb = _dot_nt(kb, qp)
            acc = acc + jnp.maximum(sa, 0.0) * w[2 * p:2 * p + 1]
            acc = acc + jnp.maximum(sb, 0.0) * w[2 * p + 1:2 * p + 2]
        acc = acc * score_scale
        acc = jnp.where(kpos_of(j) <= qpos, acc, -jnp.inf)
        keys_ref[j] = _sortable(acc)
        return carry

    lax.fori_loop(0, nvis, score_body, 0)

    def count(pred):
        def body(j, c):
            return c + jnp.sum(pred(keys_ref[j], j).reshape(tk // SUBLANES, SUBLANES, tq), axis=0)
        part = lax.fori_loop(0, nvis, body, jnp.zeros((SUBLANES, tq), I32))
        return jnp.sum(part, axis=0, keepdims=True)

    def bit_body(t, thr):
        cand = thr ^ lax.shift_left(jnp.int32(1), 31 - t)
        cnt = count(lambda k, j: jnp.where(k >= cand, 1, 0))
        return jnp.where(cnt >= topk, cand, thr)

    thr = lax.fori_loop(0, 32, bit_body, jnp.full((1, tq), INT_MIN, I32))

    need = topk - count(lambda k, j: jnp.where(k > thr, 1, 0))
    n_eq = count(lambda k, j: jnp.where(k == thr, 1, 0))
    cst_ref[...] = jnp.full((1, tq), nk * tk, I32)

    @pl.when(jnp.max(jnp.where(n_eq > need, 1, 0)) > 0)
    def _():
        def idx_body(t, c):
            cand = c | lax.shift_left(jnp.int32(1), 15 - t)
            f = count(lambda k, j: jnp.where(k == thr, jnp.where(kpos_of(j) < cand, 1, 0), 0))
            return jnp.where(f < need, cand, c)
        cst_ref[...] = lax.fori_loop(0, 16, idx_body, jnp.zeros((1, tq), I32))

    cstar = cst_ref[...]

    def write_body(j, carry):
        k = keys_ref[j]
        kpos = kpos_of(j)
        sel = jnp.where(k > thr, 0.0, jnp.where(k == thr, jnp.where(kpos <= cstar, 0.0, NEG), NEG))
        o_ref[0, j] = jnp.where(kpos <= qpos, sel, NEG).astype(o_ref.dtype)
        return carry

    lax.fori_loop(0, nvis, write_body, 0)

    def fill_body(j, carry):
        o_ref[0, j] = jnp.full((tk, tq), NEG, o_ref.dtype)
        return carry

    lax.fori_loop(nvis, nk, fill_body, 0)


def _indexer(qh, kvb, wi_t, batch, seq, tq, tk, qi_blk, ka_blk):
    nq, nk = seq // tq, seq // tk
    topk = min(TOPK_MAX, seq // 4)
    return pl.pallas_call(
        functools.partial(_indexer_kernel, tq=tq, tk=tk, nk=nk, topk=topk),
        grid=(batch, nq),
        in_specs=[pl.BlockSpec((IDX_HEADS // 2, tq, LANES), lambda b, i: (qi_blk, b * nq + i, 0)),
                  pl.BlockSpec((seq, LANES), lambda b, i: (b, ka_blk)),
                  pl.BlockSpec((seq, LANES), lambda b, i: (b, ka_blk + 1)),
                  pl.BlockSpec((IDX_HEADS, tq), lambda b, i: (0, b * nq + i))],
        out_specs=pl.BlockSpec((1, nk, tk, tq), lambda b, i: (b * nq + i, 0, 0, 0)),
        out_shape=jax.ShapeDtypeStruct((batch * nq, nk, tk, tq), BF16),
        scratch_shapes=[pltpu.VMEM((nk, tk, tq), I32), pltpu.VMEM((1, tq), I32)],
        compiler_params=_cparams(("arbitrary", "arbitrary")),
        name="indexer_topk_mask",
    )(qh, kvb, kvb, wi_t)


def _attn_kernel(q_ref, k_ref, vt_ref, m_ref, bt_ref, o_ref, acc_ref, mx_ref, l_ref, *, tq, tk):
    i = pl.program_id(1)
    exp2_scale = HEAD_DIM ** -0.5 * math.log2(math.e)
    mx_ref[...] = jnp.full(mx_ref.shape, NEG, F32)
    l_ref[...] = jnp.zeros(l_ref.shape, F32)
    acc_ref[...] = jnp.zeros(acc_ref.shape, F32)
    zero_blk = jnp.zeros((LANES, LANES), F32)

    def bias_tile(g, near):
        cols = []
        for r in range(KV_GROUP):
            h = g * KV_GROUP + r
            b0 = bt_ref[0, h]
            b1 = bt_ref[1, h]
            if near == 0:
                top = jnp.concatenate([b0, b1], axis=1)
                bot = jnp.concatenate([zero_blk, b0], axis=1)
            else:
                top = jnp.concatenate([zero_blk, zero_blk], axis=1)
                bot = jnp.concatenate([b1, zero_blk], axis=1)
            cols.append(jnp.concatenate([top, bot], axis=0))
        return jnp.concatenate(cols, axis=1)

    def update(j, near):
        k0 = pl.multiple_of(j * tk, tk)
        mt = m_ref[0, j].astype(F32)
        mt4 = jnp.concatenate([mt] * KV_GROUP, axis=1)
        for g in range(N_KV_HEADS):
            kt = k_ref[pl.ds(k0, tk), g * HEAD_DIM:(g + 1) * HEAD_DIM]
            qs = q_ref[g * KV_GROUP:(g + 1) * KV_GROUP].reshape(KV_GROUP * tq, HEAD_DIM)
            s = _dot_nt(kt, qs) + mt4
            if near is not None:
                s = s + bias_tile(g, near)
            m_old = mx_ref[g]
            m_new = jnp.maximum(m_old, jnp.max(s, axis=0, keepdims=True))
            alpha = jnp.exp2((m_old - m_new) * exp2_scale)
            p = jnp.exp2((s - m_new) * exp2_scale)
            l_ref[g] = alpha * l_ref[g] + jnp.sum(p, axis=0, keepdims=True)
            vt = vt_ref[0, j, g * HEAD_DIM:(g + 1) * HEAD_DIM, :]
            pv = jnp.dot(vt, p.astype(BF16), preferred_element_type=F32)
            acc_ref[g] = alpha * acc_ref[g] + pv
            mx_ref[g] = m_new

    def far_body(j, carry):
        update(j, None)
        return carry

    lax.fori_loop(0, jnp.maximum(i - 1, 0), far_body, 0)

    @pl.when(i >= 1)
    def _():
        update(i - 1, 1)

    update(i, 0)

    for g in range(N_KV_HEADS):
        o = acc_ref[g] / l_ref[g]
        for r in range(KV_GROUP):
            h = g * KV_GROUP + r
            o_ref[:, h * HEAD_DIM:(h + 1) * HEAD_DIM] = o[:, r * tq:(r + 1) * tq].T.astype(o_ref.dtype)


def _attention(qh, kvb, vt_tiles, mask, bias_t, batch, seq, tq, tk):
    nq, nk = seq // tq, seq // tk
    return pl.pallas_call(
        functools.partial(_attn_kernel, tq=tq, tk=tk),
        grid=(batch, nq),
        in_specs=[pl.BlockSpec((N_HEADS, tq, HEAD_DIM), lambda b, i: (0, b * nq + i, 0)),
                  pl.BlockSpec((seq, KV_WIDTH), lambda b, i: (b, 0)),
                  pl.BlockSpec((1, nk, KV_WIDTH, tk), lambda b, i: (b, 0, 0, 0)),
                  pl.BlockSpec((1, nk, tk, tq), lambda b, i: (b * nq + i, 0, 0, 0)),
                  pl.BlockSpec((2, N_HEADS, LANES, LANES), lambda b, i: (0, 0, 0, 0))],
        out_specs=pl.BlockSpec((tq, ATT_WIDTH), lambda b, i: (b * nq + i, 0)),
        out_shape=jax.ShapeDtypeStruct((batch * seq, ATT_WIDTH), BF16),
        scratch_shapes=[pltpu.VMEM((N_KV_HEADS, HEAD_DIM, KV_GROUP * tq), F32),
                        pltpu.VMEM((N_KV_HEADS, 1, KV_GROUP * tq), F32),
                        pltpu.VMEM((N_KV_HEADS, 1, KV_GROUP * tq), F32)],
        compiler_params=_cparams(("arbitrary", "arbitrary")),
        name="masked_attention",
    )(qh, kvb, vt_tiles, mask, bias_t)


def _conv_kernel(halo_ref, cur_ref, w_ref, b_ref, g_ref, bb_ref, o_ref, ext_ref, ph_ref, *, tt, tc, rc, zero_first):
    t = pl.program_id(1)
    halo = halo_ref[0]
    if zero_first:
        halo = jnp.where(t == 0, 0.0, halo)
    ext_ref[0:CONV_HALO, :] = halo
    ext_ref[CONV_HALO:CONV_HALO + tt, :] = cur_ref[0]
    first = CONV_HALO - (CONV_W - 1)
    ph_rows = ph_ref.shape[1]
    for p in range(1, SUBLANES):
        ph_ref[p - 1] = ext_ref[p:p + ph_rows, :]

    def tap_rows(start, cs):
        p, base = start % SUBLANES, start - start % SUBLANES
        if p == 0:
            return ext_ref[base:base + rc, cs]
        return ph_ref[p - 1, base:base + rc, cs]

    for c in range(tc // LANES):
        cs = slice(c * LANES, (c + 1) * LANES)
        for r in range(tt // rc):
            acc = jnp.zeros((rc, LANES), F32) + b_ref[:, cs]
            for j in range(CONV_W):
                acc = acc + w_ref[j:j + 1, cs] * tap_rows(r * rc + first + j, cs)
            mu = jnp.mean(acc, axis=-1, keepdims=True)
            dv = acc - mu
            var = jnp.mean(dv * dv, axis=-1, keepdims=True)
            yn = dv * lax.rsqrt(var + EPS) * g_ref[:, cs] + bb_ref[:, cs]
            o_ref[0, r * rc:(r + 1) * rc, cs] = (yn * jax.nn.sigmoid(yn)).astype(o_ref.dtype)


def _conv(halo_src, cur, conv_w, conv_b, cn_g, cn_b, tt, tc, zero_first):
    b, t, c = cur.shape
    hb = tt // CONV_HALO
    if zero_first:
        halo_spec = pl.BlockSpec((1, CONV_HALO, tc), lambda a, i, j: (a, jnp.maximum(i * hb - 1, 0), j))
    else:
        halo_spec = pl.BlockSpec((1, CONV_HALO, tc), lambda a, i, j: (a, 0, j))
    vec = lambda: pl.BlockSpec((1, tc), lambda a, i, j: (0, j))
    return pl.pallas_call(
        functools.partial(_conv_kernel, tt=tt, tc=tc, rc=min(tt, 64), zero_first=zero_first),
        grid=(b, t // tt, c // tc),
        in_specs=[halo_spec,
                  pl.BlockSpec((1, tt, tc), lambda a, i, j: (a, i, j)),
                  pl.BlockSpec((CONV_W, tc), lambda a, i, j: (0, j)),
                  vec(), vec(), vec()],
        out_specs=pl.BlockSpec((1, tt, tc), lambda a, i, j: (a, i, j)),
        out_shape=jax.ShapeDtypeStruct((b, t, c), BF16),
        scratch_shapes=[pltpu.VMEM((CONV_HALO + tt, tc), F32),
                        pltpu.VMEM((SUBLANES - 1, CONV_HALO + tt - SUBLANES, tc), F32)],
        compiler_params=_cparams(("arbitrary", "arbitrary", "arbitrary")),
        name="conformer_conv",
    )(halo_src, cur, conv_w, conv_b.reshape(1, c), cn_g.reshape(1, c), cn_b.reshape(1, c))


def _outproj_kernel(a_ref, c_ref, wa_ref, wc_ref, x_ref, gt_ref, o_ref):
    acc = jnp.dot(a_ref[0], wa_ref[...], preferred_element_type=F32)
    acc = acc + jnp.dot(c_ref[0], wc_ref[...], preferred_element_type=F32)
    o_ref[0] = x_ref[0] + gt_ref[0] * acc


def _outproj(att3, conv3, wo_a, wo_c, x3, gt3, tm, tn):
    g, r, d = x3.shape
    ka, kc = att3.shape[-1], conv3.shape[-1]
    gt_spec = (pl.BlockSpec((1, 1, tn), lambda a, i, j: (a, 0, j)) if gt3.shape[1] == 1
               else pl.BlockSpec((1, tm, tn), lambda a, i, j: (a, i, j)))
    return pl.pallas_call(
        _outproj_kernel,
        grid=(g, r // tm, d // tn),
        in_specs=[pl.BlockSpec((1, tm, ka), lambda a, i, j: (a, i, 0)),
                  pl.BlockSpec((1, tm, kc), lambda a, i, j: (a, i, 0)),
                  pl.BlockSpec((ka, tn), lambda a, i, j: (0, j)),
                  pl.BlockSpec((kc, tn), lambda a, i, j: (0, j)),
                  pl.BlockSpec((1, tm, tn), lambda a, i, j: (a, i, j)),
                  gt_spec],
        out_specs=pl.BlockSpec((1, tm, tn), lambda a, i, j: (a, i, j)),
        out_shape=jax.ShapeDtypeStruct((g, r, d), F32),
        compiler_params=_cparams(("arbitrary", "arbitrary", "arbitrary")),
        name="outproj_residual",
    )(att3, conv3, wo_a, wo_c, x3, gt3)


def _peer_scores_kernel(ht_ref, wqt_ref, sk_ref, o_ref):
    q_t = jnp.dot(wqt_ref[...], ht_ref[...], preferred_element_type=F32).astype(BF16)
    half = q_t.shape[0] // 2
    o_ref[0:PEER_NKEYS, :] = jnp.dot(sk_ref[0, 0], q_t[:half], preferred_element_type=F32)
    o_ref[PEER_NKEYS:2 * PEER_NKEYS, :] = jnp.dot(sk_ref[0, 1], q_t[half:], preferred_element_type=F32)


def _peer_scores(h2t, wq_t, sk, tm):
    d, n = h2t.shape
    dk = wq_t.shape[0] // PEER_HEADS
    return pl.pallas_call(
        _peer_scores_kernel,
        grid=(n // tm, PEER_HEADS),
        in_specs=[pl.BlockSpec((d, tm), lambda i, h: (0, i)),
                  pl.BlockSpec((dk, d), lambda i, h: (h, 0)),
                  pl.BlockSpec((1, 2, PEER_NKEYS, dk // 2), lambda i, h: (h, 0, 0, 0))],
        out_specs=pl.BlockSpec((2 * PEER_NKEYS, tm), lambda i, h: (h, i)),
        out_shape=jax.ShapeDtypeStruct((PEER_HEADS * 2 * PEER_NKEYS, n), F32),
        compiler_params=_cparams(("arbitrary", "arbitrary")),
        name="peer_subkey_scores",
    )(h2t, wq_t, sk)


def _top16(x):
    rows = x.shape[0]
    rid = lax.broadcasted_iota(I32, x.shape, 0).astype(F32)
    vals = []
    for k in range(PEER_TOPK):
        m = jnp.max(x, axis=0, keepdims=True)
        vals.append(m)
        if k + 1 < PEER_TOPK:
            first = jnp.min(jnp.where(x == m, rid, float(rows)), axis=0, keepdims=True)
            x = jnp.where(rid == first, -jnp.inf, x)
    return vals


def _stack_rows(rows):
    shape = (len(rows), rows[0].shape[1])
    rid = lax.broadcasted_iota(I32, shape, 0)
    out = jnp.zeros(shape, rows[0].dtype)
    for k, row in enumerate(rows):
        out = jnp.where(rid == k, row, out)
    return out


def _peer_route_kernel(s_ref, s1m_ref, s2m_ref, c1_ref, e2_ref, thr_ref):
    s1 = s_ref[0:PEER_NKEYS, :]
    s2 = s_ref[PEER_NKEYS:2 * PEER_NKEYS, :]
    t1 = _top16(s1)
    t2 = _top16(s2)
    t2_all = _stack_rows(t2)
    blocks = [t1[0] + t2_all]
    for a in range(1, 8):
        blocks.append(t1[a] + t2_all[0:8])
    blocks.append(_stack_rows(t1[8:16]) + t2[0])
    top = _top16(jnp.concatenate(blocks, axis=0))
    z = jnp.ones_like(top[0])
    for k in range(1, PEER_TOPK):
        z = z + jnp.exp(top[k] - top[0])
    s1m_ref[...] = jnp.where(s1 >= t1[PEER_TOPK - 1], s1, -jnp.inf)
    s2m_ref[...] = jnp.where(s2 >= t2[PEER_TOPK - 1], s2, -jnp.inf)
    c1_ref[...] = jnp.exp(s1 - t1[0]) / z
    e2_ref[...] = jnp.exp(s2 - t2[0])
    thr_ref[0] = top[PEER_TOPK - 1]


def _peer_route(s_t, tl):
    rows, n = s_t.shape
    big = lambda: pl.BlockSpec((PEER_NKEYS, tl), lambda h, t: (h, t))
    big_shape = jax.ShapeDtypeStruct((PEER_HEADS * PEER_NKEYS, n), F32)
    return pl.pallas_call(
        _peer_route_kernel,
        grid=(PEER_HEADS, n // tl),
        in_specs=[pl.BlockSpec((2 * PEER_NKEYS, tl), lambda h, t: (h, t))],
        out_specs=[big(), big(), big(), big(), pl.BlockSpec((1, 1, tl), lambda h, t: (h, 0, t))],
        out_shape=[big_shape, big_shape, big_shape, big_shape,
                   jax.ShapeDtypeStruct((PEER_HEADS, 1, n), F32)],
        compiler_params=_cparams(("arbitrary", "arbitrary")),
        name="peer_route",
    )(s_t)


GATE_ROWS = 64


MXU_COLS = 256


def _peer_dense_kernel(ht_ref, u_ref, v_ref, s1_ref, c1_ref, s2_ref, e2_ref, thr_ref, o_ref, *, te):
    e = pl.program_id(1)
    tm = ht_ref.shape[1]
    rows_per_tile = te // PEER_NKEYS

    @pl.when(e == 0)
    def _():
        o_ref[...] = jnp.zeros(o_ref.shape, F32)

    chunk = min(MXU_COLS, tm)
    a_chunks = [jnp.dot(u_ref[...], ht_ref[:, c * chunk:(c + 1) * chunk], preferred_element_type=F32)
                for c in range(tm // chunk)]

    s1rows = [[s1_ref[pl.ds(h * PEER_NKEYS + e * rows_per_tile + r, 1), :] for r in range(rows_per_tile)]
              for h in range(PEER_HEADS)]
    c1rows = [[c1_ref[pl.ds(h * PEER_NKEYS + e * rows_per_tile + r, 1), :] for r in range(rows_per_tile)]
              for h in range(PEER_HEADS)]
    for c in range(tm // chunk):
        w_cols = []
        for lc in range(chunk // LANES):
            ls = slice(c * chunk + lc * LANES, c * chunk + (lc + 1) * LANES)
            als = slice(lc * LANES, (lc + 1) * LANES)
            blocks = [[None] * (PEER_NKEYS // GATE_ROWS) for _ in range(rows_per_tile)]
            for part in range(PEER_NKEYS // GATE_ROWS):
                rs = [slice(r * PEER_NKEYS + part * GATE_ROWS, r * PEER_NKEYS + (part + 1) * GATE_ROWS)
                      for r in range(rows_per_tile)]
                acc = [jnp.zeros((GATE_ROWS, LANES), F32) for _ in range(rows_per_tile)]
                for h in range(PEER_HEADS):
                    row0 = h * PEER_NKEYS + part * GATE_ROWS
                    s2 = s2_ref[row0:row0 + GATE_ROWS, ls]
                    e2 = e2_ref[row0:row0 + GATE_ROWS, ls]
                    thr = thr_ref[h, :, ls]
                    for r in range(rows_per_tile):
                        cand = s2 + s1rows[h][r][:, ls]
                        acc[r] = acc[r] + jnp.where(cand >= thr, e2, 0.0) * c1rows[h][r][:, ls]
                for r in range(rows_per_tile):
                    blocks[r][part] = (acc[r] * jax.nn.gelu(a_chunks[c][rs[r], als])).astype(BF16)
            w_cols.append(jnp.concatenate([b for row in blocks for b in row], axis=0))
        cs = slice(c * chunk, (c + 1) * chunk)
        o_ref[:, cs] += lax.dot_general(v_ref[...], jnp.concatenate(w_cols, axis=1),
                                        (((0,), (0,)), ((), ())), preferred_element_type=F32)


def _peer_dense(h2t, u_bf, v_bf, s1m, c1, s2m, e2, thr, tm, te):
    d, n = h2t.shape
    rows = PEER_HEADS * PEER_NKEYS
    once = pl.Buffered(1)
    tok = lambda: pl.BlockSpec((rows, tm), lambda i, e: (0, i), pipeline_mode=once)
    return pl.pallas_call(
        functools.partial(_peer_dense_kernel, te=te),
        grid=(n // tm, u_bf.shape[0] // te),
        in_specs=[pl.BlockSpec((d, tm), lambda i, e: (0, i), pipeline_mode=once),
                  pl.BlockSpec((te, d), lambda i, e: (e, 0)),
                  pl.BlockSpec((te, d), lambda i, e: (e, 0)),
                  tok(), tok(), tok(), tok(),
                  pl.BlockSpec((PEER_HEADS, 1, tm), lambda i, e: (0, 0, i), pipeline_mode=once)],
        out_specs=pl.BlockSpec((d, tm), lambda i, e: (0, i), pipeline_mode=once),
        out_shape=jax.ShapeDtypeStruct((d, n), F32),
        compiler_params=_cparams(("arbitrary", "arbitrary"), VMEM_LIMIT_PEER),
        name="peer_dense_experts",
    )(h2t, u_bf, v_bf, s1m, c1, s2m, e2, thr)


def _final_kernel(x_ref, gt_ref, p_ref, g_ref, o_ref, *, normalize):
    xx = x_ref[0] + gt_ref[0] * p_ref[...].T
    if normalize:
        ms = jnp.mean(xx * xx, axis=-1, keepdims=True)
        xx = xx * lax.rsqrt(ms + EPS) * g_ref[...]
    o_ref[0] = xx


def _final(x3, gt3, peer_t, g_final, tm, normalize):
    g, r, d = x3.shape
    nb = r // tm
    return pl.pallas_call(
        functools.partial(_final_kernel, normalize=normalize),
        grid=(g, nb),
        in_specs=[pl.BlockSpec((1, tm, d), lambda a, i: (a, i, 0)),
                  _row_mod_spec(gt3, tm),
                  pl.BlockSpec((d, tm), lambda a, i: (0, a * nb + i)),
                  pl.BlockSpec((1, d), lambda a, i: (0, 0))],
        out_specs=pl.BlockSpec((1, tm, d), lambda a, i: (a, i, 0)),
        out_shape=jax.ShapeDtypeStruct((g, r, d), F32),
        compiler_params=_cparams(("arbitrary", "arbitrary")),
        name="final_norm",
    )(x3, gt3, peer_t, g_final.reshape(1, d))


def _sample_index_kernel(pt_ref, qi_ref, wi_ref, kself_ref, expand_ref, cki_ref, mask_ref, self_ref,
                         kbuf_ref, sem_ref, sc_ref, scself_ref, *, n_pages, topk, page_base):
    b = pl.program_id(0)
    slot = b % 2
    score_scale = IDX_DIM ** -0.5 * IDX_HEADS ** -0.5

    def page_copy(seq, k, sl):
        page = page_base + pt_ref[seq, k]
        return pltpu.make_async_copy(cki_ref.at[page], kbuf_ref.at[sl, :, pl.ds(k * PAGE_SIZE, PAGE_SIZE)],
                                     sem_ref.at[sl])

    @pl.when(b == 0)
    def _():
        for k in range(n_pages):
            page_copy(0, k, 0).start(priority=k % 2)

    @pl.when(b + 1 < pl.num_programs(0))
    def _():
        for k in range(n_pages):
            page_copy(b + 1, k, 1 - slot).start(priority=k % 2)

    for k in range(n_pages):
        page_copy(b, k, slot).wait()

    qi = qi_ref[0]
    wi = wi_ref[0]
    s = jnp.dot(qi, kbuf_ref[slot].astype(BF16), preferred_element_type=F32)
    sc_ref[pl.ds(b, 1), :] = jnp.sum(jnp.maximum(s, 0.0) * wi, axis=0, keepdims=True) * score_scale
    ks = kself_ref[0].astype(BF16).astype(F32)
    s_self = jnp.sum(qi.astype(F32) * ks, axis=1, keepdims=True)
    self_score = jnp.sum(jnp.maximum(s_self, 0.0) * wi, axis=0, keepdims=True) * score_scale
    scself_ref[pl.ds(b, 1), :] = jnp.zeros((1, LANES), F32) + self_score

    @pl.when(b == pl.num_programs(0) - 1)
    def _():
        keys = _sortable(sc_ref[...])
        kself = _sortable(scself_ref[:, 0:1])
        pos = lax.broadcasted_iota(I32, keys.shape, 1)
        self_pos = n_pages * PAGE_SIZE

        def total(x, xs):
            return jnp.sum(x, axis=1, keepdims=True) + xs

        def bit_body(t, thr):
            cand = thr ^ lax.shift_left(jnp.int32(1), 31 - t)
            cnt = total(jnp.where(keys >= cand, 1, 0), jnp.where(kself >= cand, 1, 0))
            return jnp.where(cnt >= topk, cand, thr)

        thr = lax.fori_loop(0, 32, bit_body, jnp.full(kself.shape, INT_MIN, I32))
        need = topk - total(jnp.where(keys > thr, 1, 0), jnp.where(kself > thr, 1, 0))

        def idx_body(t, c):
            cand = c | lax.shift_left(jnp.int32(1), 15 - t)
            f = total(jnp.where(keys == thr, jnp.where(pos < cand, 1, 0), 0),
                      jnp.where(kself == thr, jnp.where(self_pos < cand, 1, 0), 0))
            return jnp.where(f < need, cand, c)

        cstar = lax.fori_loop(0, 16, idx_body, jnp.zeros(kself.shape, I32))
        picked = jnp.where(keys > thr, 1.0, jnp.where(keys == thr, jnp.where(pos <= cstar, 1.0, 0.0), 0.0))
        for k in range(n_pages):
            flags = picked[:, k * PAGE_SIZE:(k + 1) * PAGE_SIZE].astype(BF16)
            rows = jnp.dot(flags, expand_ref[...], preferred_element_type=F32)
            mask_ref[k] = jnp.where(rows > 0.5, 0.0, NEG)
        ssel = jnp.where(kself > thr, 0.0, jnp.where(kself == thr, jnp.where(self_pos <= cstar, 0.0, NEG), NEG))
        self_ref[...] = jnp.zeros(self_ref.shape, F32) + ssel


def _sample_index(page_table, qi3, wi3, kself3, cki_t, page_base):
    db, n_pages = page_table.shape
    topk = min(TOPK_MAX, (n_pages * PAGE_SIZE + 1) // 4)
    page_rows = PAGE_SIZE * N_KV_HEADS
    expand = (jnp.arange(page_rows, dtype=I32)[None, :] // N_KV_HEADS
              == jnp.arange(PAGE_SIZE, dtype=I32)[:, None]).astype(BF16)
    grid_spec = pltpu.PrefetchScalarGridSpec(
        num_scalar_prefetch=1,
        grid=(db,),
        in_specs=[pl.BlockSpec((1, IDX_HEADS, IDX_DIM), lambda b, pt: (b, 0, 0)),
                  pl.BlockSpec((1, IDX_HEADS, 1), lambda b, pt: (b, 0, 0)),
                  pl.BlockSpec((1, 1, IDX_DIM), lambda b, pt: (b, 0, 0)),
                  pl.BlockSpec((PAGE_SIZE, page_rows), lambda b, pt: (0, 0)),
                  pl.BlockSpec(memory_space=pl.ANY)],
        out_specs=[pl.BlockSpec((n_pages, db, page_rows), lambda b, pt: (0, 0, 0)),
                   pl.BlockSpec((db, LANES), lambda b, pt: (0, 0))],
        scratch_shapes=[pltpu.VMEM((2, IDX_DIM, n_pages * PAGE_SIZE), F32),
                        pltpu.SemaphoreType.DMA((2,)),
                        pltpu.VMEM((db, n_pages * PAGE_SIZE), F32),
                        pltpu.VMEM((db, LANES), F32)],
    )
    mask_t, selfsel = pl.pallas_call(
        functools.partial(_sample_index_kernel, n_pages=n_pages, topk=topk, page_base=page_base),
        grid_spec=grid_spec,
        out_shape=[jax.ShapeDtypeStruct((n_pages, db, page_rows), F32),
                   jax.ShapeDtypeStruct((db, LANES), F32)],
        compiler_params=_cparams(("arbitrary",)),
        name="sample_indexer",
    )(page_table, qi3, wi3, kself3, expand, cki_t)
    return mask_t.transpose(1, 0, 2), jnp.broadcast_to(selfsel[:, None, :], (db, SUBLANES, LANES))


def _sample_attn_kernel(pt_ref, q_ref, kself_ref, vself_ref, mask_ref, self_ref, bias_ref, bself_ref,
                        *rest, n_pages):
    pps = ATTN_PAGES_PER_STEP
    kpages = rest[:pps]
    vpages = rest[pps:2 * pps]
    o_ref, acc_ref, mx_ref, l_ref = rest[2 * pps:]
    step = pl.program_id(1)
    n_steps = n_pages // pps
    page_rows = PAGE_SIZE * N_KV_HEADS
    sm_scale = HEAD_DIM ** -0.5
    q = q_ref[0]
    head_group = lax.broadcasted_iota(I32, (N_HEADS, HEAD_DIM), 0) // KV_GROUP

    def own_group_rows(x_ref):
        out = jnp.zeros((N_HEADS, HEAD_DIM), F32)
        for g in range(N_KV_HEADS):
            out = jnp.where(head_group == g, x_ref[0, g:g + 1, :].astype(BF16).astype(F32), out)
        return out

    @pl.when(step == 0)
    def _():
        logit = jnp.sum(q.astype(F32) * own_group_rows(kself_ref), axis=1, keepdims=True)
        mx_ref[...] = logit * sm_scale + bself_ref[...] + self_ref[0, 0:1, 0:1]
        l_ref[...] = jnp.ones(l_ref.shape, F32)
        acc_ref[...] = own_group_rows(vself_ref)

    kcat = jnp.concatenate([kpages[k][0].astype(BF16) for k in range(pps)], axis=0)
    vcat = jnp.concatenate([vpages[k][0].astype(BF16) for k in range(pps)], axis=0)
    s = _dot_nt(q, kcat) * sm_scale + bias_ref[step]
    s = jnp.concatenate([s[:, k * page_rows:(k + 1) * page_rows] + mask_ref[0, pl.ds(step * pps + k, 1), :]
                         for k in range(pps)], axis=1)
    m_old = mx_ref[...]
    m_new = jnp.maximum(m_old, jnp.max(s, axis=1, keepdims=True))
    alpha = jnp.exp(m_old - m_new)
    p = jnp.exp(s - m_new)
    l_ref[...] = alpha * l_ref[...] + jnp.sum(p, axis=1, keepdims=True)
    acc_ref[...] = alpha * acc_ref[...] + jnp.dot(p.astype(BF16), vcat, preferred_element_type=F32)
    mx_ref[...] = m_new

    @pl.when(step == n_steps - 1)
    def _():
        o_ref[0] = (acc_ref[...] / l_ref[...]).astype(o_ref.dtype)


def _sample_attn(page_table, q3, kself3, vself3, mask, selfsel, bias_steps, bias_self, ck, cv, page_base):
    db, n_pages = page_table.shape
    pps = ATTN_PAGES_PER_STEP
    n_steps = n_pages // pps
    page_rows = PAGE_SIZE * N_KV_HEADS
    step_rows = pps * page_rows

    def page_spec(k):
        return pl.BlockSpec((1, page_rows, HEAD_DIM),
                            lambda b, s, pt: (page_base + pt[b, s * pps + k], 0, 0))

    per_b = lambda shape: pl.BlockSpec((1,) + shape, lambda b, s, pt: (b, 0, 0))
    grid_spec = pltpu.PrefetchScalarGridSpec(
        num_scalar_prefetch=1,
        grid=(db, n_steps),
        in_specs=[per_b((N_HEADS, HEAD_DIM)), per_b((N_KV_HEADS, HEAD_DIM)), per_b((N_KV_HEADS, HEAD_DIM)),
                  per_b((n_pages, page_rows)), per_b((SUBLANES, LANES)),
                  pl.BlockSpec((n_steps, N_HEADS, step_rows), lambda b, s, pt: (0, 0, 0)),
                  pl.BlockSpec((N_HEADS, 1), lambda b, s, pt: (0, 0))]
                 + [page_spec(k) for k in range(pps)] * 2,
        out_specs=per_b((N_HEADS, HEAD_DIM)),
        scratch_shapes=[pltpu.VMEM((N_HEADS, HEAD_DIM), F32),
                        pltpu.VMEM((N_HEADS, 1), F32),
                        pltpu.VMEM((N_HEADS, 1), F32)],
    )
    return pl.pallas_call(
        functools.partial(_sample_attn_kernel, n_pages=n_pages),
        grid_spec=grid_spec,
        out_shape=jax.ShapeDtypeStruct((db, N_HEADS, HEAD_DIM), BF16),
        compiler_params=_cparams(("arbitrary", "arbitrary")),
        name="sample_attention",
    )(page_table, q3, kself3, vself3, mask, selfsel, bias_steps, bias_self,
      *([ck] * pps), *([cv] * pps))


def _rel_bucket(dist):
    n = jnp.maximum(dist, 0)
    max_exact = REL_BUCKETS // 2
    nf = jnp.maximum(n, 1).astype(F32)
    large = max_exact + (jnp.log(nf / max_exact) / math.log(REL_MAX_DIST / max_exact)
                         * (REL_BUCKETS - max_exact)).astype(I32)
    large = jnp.minimum(large, REL_BUCKETS - 1)
    return jnp.where(n < max_exact, n, large)


def _bias_of_dist(rel_bias, dist):
    onehot = (_rel_bucket(dist)[..., None] == jnp.arange(REL_BUCKETS, dtype=I32)).astype(F32)
    return jnp.einsum("...b,bh->...h", onehot, rel_bias.astype(F32), precision=lax.Precision.HIGHEST)


def _prompt_bias_tiles(rel_bias):
    kc = jnp.arange(LANES, dtype=I32)[:, None]
    qr = jnp.arange(LANES, dtype=I32)[None, :]
    far = rel_bias[REL_BUCKETS - 1]
    tiles = []
    for off in (0, LANES):
        dist = off + qr - kc
        t = (_bias_of_dist(rel_bias, dist) - far) * HEAD_DIM ** 0.5
        tiles.append(jnp.where((dist >= 0)[..., None], t, 0.0).transpose(2, 0, 1))
    return jnp.stack(tiles).astype(F32)


def _sample_bias_steps(rel_bias, n_pages):
    past = n_pages * PAGE_SIZE
    bias = _bias_of_dist(rel_bias, past - jnp.arange(past, dtype=I32))
    own = (jnp.arange(N_HEADS, dtype=I32)[:, None] // KV_GROUP) == jnp.arange(N_KV_HEADS, dtype=I32)[None, :]
    rows = jnp.where(own[None], bias[:, :, None], NEG)
    rows = rows.transpose(1, 0, 2).reshape(N_HEADS, n_pages // ATTN_PAGES_PER_STEP, -1)
    return rows.transpose(1, 0, 2).astype(F32)


def _split_in_proj(w_in_l):
    sizes = (ATT_WIDTH, KV_WIDTH, KV_WIDTH, IDX_HEADS * IDX_DIM, IDX_DIM, IDX_HEADS)
    offs = [0]
    for s in sizes:
        offs.append(offs[-1] + s)
    conv_ch = (w_in_l.shape[1] - offs[-1]) // 2
    d = w_in_l.shape[0]
    wq, wk, wv, wqi, wki, wwi = (w_in_l[:, offs[i]:offs[i + 1]] for i in range(6))
    wua = w_in_l[:, offs[-1]:offs[-1] + conv_ch]
    wub = w_in_l[:, offs[-1] + conv_ch:]
    z = lambda n: jnp.zeros((d, n), w_in_l.dtype)
    w_a = jnp.concatenate([wk, wv, wki, z(LANES - IDX_DIM), z(LANES - IDX_DIM), wki,
                           wwi, z(LANES - IDX_HEADS), z(LANES)], axis=1)
    w_b = jnp.concatenate([wq, wqi], axis=1)
    chunk = 256
    parts = []
    for c in range(conv_ch // chunk):
        parts += [wua[:, c * chunk:(c + 1) * chunk], wub[:, c * chunk:(c + 1) * chunk]]
    w_c = jnp.concatenate(parts, axis=1)
    return w_a.astype(BF16), w_b.astype(BF16), w_c.astype(BF16)


COL_K, COL_V, COL_KA, COL_WI = 0, KV_WIDTH, 2 * KV_WIDTH, 2 * KV_WIDTH + 2 * LANES


def _mixer_projections(h, w_a, w_b, w_c, tm):
    z_a = _matmul(h, w_a, F32, tm, 512)
    qh = _matmul(h, w_b, BF16, tm, 512, head_major=True)
    u = _matmul(h, w_c, F32, tm, 512, glu=True)
    return z_a, qh, u


ATTN_TILE = 256
PROJ_TM, PROJ_TN = 1024, 512
NORM_ROWS = 512
FINAL_ROWS = 256
CONV_TT, CONV_TC = 256, 512
PEER_TM, PEER_TE = 512, 1024
PEER_TE_DECODE = 512
ROUTE_LANES = 256


def _peer_block(h2t, wq_t, sk_bf, u_bf, v_bf, tm, te):
    s_t = _peer_scores(h2t, wq_t, sk_bf, tm)
    s1m, s2m, c1, e2, thr = _peer_route(s_t, min(tm, ROUTE_LANES))
    return _peer_dense(h2t, u_bf, v_bf, s1m, c1, s2m, e2, thr, tm, te)


def kernel(x_prompt, x_sample, cache_k, cache_v, cache_kidx, state_conv, page_table, c_prompt, c_sample,
           rel_bias, w_ada, b_ada, g_mix, w_in, conv_w, conv_b, cn_g, cn_b, w_o, g_ch, peer_wq,
           peer_subkeys, peer_u, peer_v, g_final):
    batch, seq, d = x_prompt.shape
    db = x_sample.shape[0]
    depth = w_ada.shape[0]
    n_pages = page_table.shape[1]
    conv_ch = conv_w.shape[-1]
    n_prompt = batch * seq
    tq = tk = ATTN_TILE

    xp = x_prompt
    xs = jnp.pad(x_sample.reshape(1, db, d), ((0, 0), (0, SAMPLE_ROWS - db), (0, 0)))
    c_rows = batch + db
    c_pad = (-c_rows) % 16
    c_all = jnp.pad(jnp.concatenate([c_prompt, c_sample], axis=0), ((0, c_pad), (0, 0)))
    bias_t = _prompt_bias_tiles(rel_bias)
    bias_steps = _sample_bias_steps(rel_bias, n_pages)
    n_pool = cache_k.shape[1]
    page_rows = PAGE_SIZE * N_KV_HEADS
    ck_rows = cache_k.reshape(depth * n_pool, page_rows, HEAD_DIM)
    cv_rows = cache_v.reshape(depth * n_pool, page_rows, HEAD_DIM)
    cki_t = jnp.swapaxes(cache_kidx, 2, 3).reshape(depth * n_pool, IDX_DIM, PAGE_SIZE)
    bias_self = rel_bias[_rel_bucket(jnp.zeros((), I32))].reshape(N_HEADS, 1)

    outs = {k: [] for k in ("kp", "vp", "kip", "cp", "ks", "vs", "kis", "cs")}
    for l in range(depth):
        mods = _adaln(c_all, w_ada[l], b_ada[l])
        p_mod = [m.reshape(batch, 1, d) for m in jnp.split(mods[:batch], 6, axis=-1)]
        s_mod = [jnp.pad(m.reshape(1, db, d), ((0, 0), (0, SAMPLE_ROWS - db), (0, 0)))
                 for m in jnp.split(mods[batch:c_rows], 6, axis=-1)]
        w_a, w_b, w_c = _split_in_proj(w_in[l])
        wo_a = w_o[l][:ATT_WIDTH].astype(BF16)
        wo_c = w_o[l][ATT_WIDTH:].astype(BF16)

        hp = _modulate(xp, g_mix[l], p_mod[1], p_mod[0], NORM_ROWS).reshape(n_prompt, d)
        z_a, qh, u = _mixer_projections(hp, w_a, w_b, w_c, PROJ_TM)
        kvb = z_a[:, :COL_WI].astype(BF16)
        wi_t = z_a[:, COL_WI:COL_WI + IDX_HEADS].T
        vt_tiles = (kvb[:, COL_V:COL_V + KV_WIDTH].reshape(batch, seq // tk, tk, KV_WIDTH)
                    .transpose(0, 1, 3, 2))
        mask = _indexer(qh, kvb, wi_t, batch, seq, tq, tk, qi_blk=N_HEADS // (IDX_HEADS // 2),
                        ka_blk=COL_KA // LANES)
        att = _attention(qh, kvb, vt_tiles, mask, bias_t, batch, seq, tq, tk)
        u3 = u.reshape(batch, seq, conv_ch)
        conv = _conv(u3, u3, conv_w[l], conv_b[l], cn_g[l], cn_b[l], CONV_TT, CONV_TC, zero_first=True)
        xp = _outproj(att.reshape(batch, seq, ATT_WIDTH), conv, wo_a, wo_c, xp, p_mod[2], PROJ_TM, PROJ_TN)
        outs["kp"].append(z_a[:, COL_K:COL_K + KV_WIDTH].reshape(batch, seq, N_KV_HEADS, HEAD_DIM))
        outs["vp"].append(z_a[:, COL_V:COL_V + KV_WIDTH].reshape(batch, seq, N_KV_HEADS, HEAD_DIM))
        outs["kip"].append(z_a[:, COL_KA:COL_KA + IDX_DIM].reshape(batch, seq, IDX_DIM))
        outs["cp"].append(u3[:, seq - (CONV_W - 1):])

        hs = _modulate(xs, g_mix[l], s_mod[1], s_mod[0], SAMPLE_ROWS).reshape(SAMPLE_ROWS, d)
        zs_a, qhs, us = _mixer_projections(hs, w_a, w_b, w_c, SAMPLE_ROWS)
        k_new = zs_a[:db, COL_K:COL_K + KV_WIDTH]
        v_new = zs_a[:db, COL_V:COL_V + KV_WIDTH]
        ki_new = zs_a[:db, COL_KA:COL_KA + IDX_DIM]
        wi_new = zs_a[:db, COL_WI:COL_WI + IDX_HEADS]
        q_s = qhs[:N_HEADS, :db].transpose(1, 0, 2)
        qi_s = (qhs[N_HEADS:, :db].transpose(1, 0, 2)
                .reshape(db, IDX_HEADS // 2, 2, IDX_DIM).reshape(db, IDX_HEADS, IDX_DIM))
        smask, sself = _sample_index(page_table, qi_s, wi_new.reshape(db, IDX_HEADS, 1),
                                     ki_new.reshape(db, 1, IDX_DIM), cki_t, l * n_pool)
        att_s = _sample_attn(page_table, q_s, k_new.reshape(db, N_KV_HEADS, HEAD_DIM),
                             v_new.reshape(db, N_KV_HEADS, HEAD_DIM), smask, sself, bias_steps, bias_self,
                             ck_rows, cv_rows, l * n_pool)
        att_s = jnp.pad(att_s.reshape(1, db, ATT_WIDTH), ((0, 0), (0, SAMPLE_ROWS - db), (0, 0)))
        u_new = us[:db]
        state = state_conv[l].astype(F32)
        halo = jnp.pad(state, ((0, 0), (CONV_HALO - (CONV_W - 1), 0), (0, 0)))
        cur = jnp.pad(u_new.reshape(db, 1, conv_ch), ((0, 0), (0, SUBLANES - 1), (0, 0)))
        conv_s = _conv(halo, cur, conv_w[l], conv_b[l], cn_g[l], cn_b[l], SUBLANES, CONV_TC, zero_first=False)
        conv_s = jnp.pad(conv_s[:, 0].reshape(1, db, conv_ch), ((0, 0), (0, SAMPLE_ROWS - db), (0, 0)))
        xs = _outproj(att_s, conv_s, wo_a, wo_c, xs, s_mod[2], SAMPLE_ROWS, PROJ_TN)
        outs["ks"].append(k_new.reshape(db, 1, N_KV_HEADS, HEAD_DIM))
        outs["vs"].append(v_new.reshape(db, 1, N_KV_HEADS, HEAD_DIM))
        outs["kis"].append(ki_new.reshape(db, 1, IDX_DIM))
        outs["cs"].append(jnp.concatenate([state[:, 1:], u_new.reshape(db, 1, conv_ch)], axis=1))

        wq_t = peer_wq[l].astype(BF16).T
        sk_bf = peer_subkeys[l].astype(BF16)
        u_bf = peer_u[l].astype(BF16)
        v_bf = peer_v[l].astype(BF16)
        hp2 = _modulate(xp, g_ch[l], p_mod[4], p_mod[3], NORM_ROWS, transposed=True)
        peer_p = _peer_block(hp2, wq_t, sk_bf, u_bf, v_bf, PEER_TM, PEER_TE)
        hs2 = _modulate(xs, g_ch[l], s_mod[4], s_mod[3], SAMPLE_ROWS, transposed=True)
        peer_s = _peer_block(hs2, wq_t, sk_bf, u_bf, v_bf, SAMPLE_ROWS, PEER_TE_DECODE)
        last = l == depth - 1
        xp = _final(xp, p_mod[5], peer_p, g_final, FINAL_ROWS, normalize=last)
        xs = _final(xs, s_mod[5], peer_s, g_final, SAMPLE_ROWS, normalize=last)

    st = lambda k: jnp.stack(outs[k])
    y_sample = xs[0, :db].reshape(db, 1, d)
    return (xp, y_sample, st("kp"), st("vp"), st("kip"), st("cp"),
            st("ks"), st("vs"), st("kis"), st("cs"))
```
